```python
import math
import jax, jax.numpy as jnp
from jax import lax
import numpy as np

D_MODEL = 1024
BATCH = 32
SEQ = 2048
DEPTH = 1

CTX_LEN = 256
GRID_W = 64
S5_WIDTH = 512
S5_GROUP = 16
S5_GROUPS = S5_WIDTH // S5_GROUP
S5_STATE = 64
RW_WIDTH = D_MODEL - S5_WIDTH
RW_HEAD = 64
RW_HEADS = RW_WIDTH // RW_HEAD
RW_DECAY_LORA = 32
RW_AAA_LORA = 32
RW_GATE_LORA = 96
RW_COLS = 3 * RW_WIDTH + RW_DECAY_LORA + RW_AAA_LORA + RW_GATE_LORA
IN_COLS = S5_WIDTH + RW_COLS
PEER_HEADS = 8
PEER_NKEYS = 128
PEER_EXPERTS = PEER_NKEYS * PEER_NKEYS
PEER_QDIM = 256
PEER_HALF = PEER_QDIM // 2
PEER_TOPK = 16
PEER_BLOCK = 128
NORM_EPS = 1e-6
GN_EPS = 64e-5

kernel_name = "hybrid_s5_rwkv7_peer_dit_block"

F32 = jnp.float32


def rmsnorm(x, g):
    xf = x.astype(F32)
    y = xf * lax.rsqrt(jnp.mean(xf * xf, axis=-1, keepdims=True) + NORM_EPS)
    return y.astype(x.dtype) * g


def modulate(h, shift, scale):
    return h * (1 + scale) + shift


def s5_discretize(a_re, a_im, log_dt, b_re, b_im):
    a_re, a_im = a_re.astype(F32), a_im.astype(F32)
    b_re, b_im = b_re.astype(F32), b_im.astype(F32)
    dt = jnp.exp(log_dt.astype(F32))[:, None]
    mag = jnp.exp(dt * a_re)
    ab_re, ab_im = mag * jnp.cos(dt * a_im), mag * jnp.sin(dt * a_im)
    nr, ni = ab_re - 1.0, ab_im
    den = a_re * a_re + a_im * a_im
    cf_re = (nr * a_re + ni * a_im) / den
    cf_im = (ni * a_re - nr * a_im) / den
    bb_re = cf_re[..., None] * b_re - cf_im[..., None] * b_im
    bb_im = cf_re[..., None] * b_im + cf_im[..., None] * b_re
    return ab_re, ab_im, bb_re, bb_im


def _complex_affine_combine(e1, e2):
    a1r, a1i, b1r, b1i = e1
    a2r, a2i, b2r, b2i = e2
    return (a2r * a1r - a2i * a1i, a2r * a1i + a2i * a1r,
            a2r * b1r - a2i * b1i + b2r, a2r * b1i + a2i * b1r + b2i)


def s5_scan(u, disc, h0):
    ab_re, ab_im, bb_re, bb_im = disc
    bn, length, _ = u.shape
    uf = u.astype(F32).reshape(bn, length, S5_GROUPS, S5_GROUP)
    bu_re = jnp.einsum('blgh,gph->blgp', uf, bb_re)
    bu_im = jnp.einsum('blgh,gph->blgp', uf, bb_im)
    if h0 is not None:
        h0_re, h0_im = h0
        bu_re = bu_re.at[:, 0].add(ab_re * h0_re - ab_im * h0_im)
        bu_im = bu_im.at[:, 0].add(ab_re * h0_im + ab_im * h0_re)
    a_re = jnp.broadcast_to(ab_re, (1, length) + ab_re.shape)
    a_im = jnp.broadcast_to(ab_im, (1, length) + ab_im.shape)
    _, _, h_re, h_im = lax.associative_scan(_complex_affine_combine, (a_re, a_im, bu_re, bu_im), axis=1)
    return h_re, h_im


def s5_readout(h, c_re, c_im):
    h_re, h_im = h
    y = (jnp.einsum('blgp,ghp->blgh', h_re, c_re.astype(F32))
         - jnp.einsum('blgp,ghp->blgh', h_im, c_im.astype(F32)))
    return y.reshape(y.shape[0], y.shape[1], S5_WIDTH)


def s5_glu(y, w, b):
    y1 = jax.nn.gelu(y, approximate=False)
    return y1 * jax.nn.sigmoid(y1 @ w + b)


def s5_mixer(u_ctx, u_lat, p, need_ctx):
    y_lat = p['s5_d'] * u_lat
    y_ctx = p['s5_d'] * u_ctx if need_ctx else None
    for d in range(2):
        disc = s5_discretize(p['s5_a_re'][d], p['s5_a_im'][d], p['s5_log_dt'][d],
                             p['s5_b_re'][d], p['s5_b_im'][d])
        flip = (lambda t: jnp.flip(t, axis=1)) if d == 1 else (lambda t: t)
        hc = s5_scan(flip(u_ctx), disc, None)
        hl = s5_scan(flip(u_lat), disc, (hc[0][:, -1], hc[1][:, -1]))
        y_lat = y_lat + flip(s5_readout(hl, p['s5_c_re'][d], p['s5_c_im'][d]))
        if need_ctx:
            y_ctx = y_ctx + flip(s5_readout(hc, p['s5_c_re'][d], p['s5_c_im'][d]))
    out_lat = s5_glu(y_lat, p['s5_w_glu'], p['s5_b_glu']).astype(u_lat.dtype)
    out_ctx = s5_glu(y_ctx, p['s5_w_glu'], p['s5_b_glu']).astype(u_ctx.dtype) if need_ctx else None
    return out_lat, out_ctx


def shift_grid(z, rows):
    bn, length, ch = z.shape
    zg = z.reshape(bn, rows, GRID_W, ch)
    left = jnp.pad(zg[:, :, :-1], ((0, 0), (0, 0), (1, 0), (0, 0)))
    right = jnp.pad(zg[:, :, 1:], ((0, 0), (0, 0), (0, 1), (0, 0)))
    up = jnp.pad(zg[:, :-1], ((0, 0), (1, 0), (0, 0), (0, 0)))
    down = jnp.pad(zg[:, 1:], ((0, 0), (0, 1), (0, 0), (0, 0)))
    sel = jnp.arange(ch) % 4
    out = jnp.where(sel == 0, left, jnp.where(sel == 1, right, jnp.where(sel == 2, up, down)))
    return out.reshape(bn, length, ch)


def shift_seq(z):
    prev = jnp.pad(z[:, :-1], ((0, 0), (1, 0), (0, 0)))
    nxt = jnp.pad(z[:, 1:], ((0, 0), (0, 1), (0, 0)))
    return jnp.where(jnp.arange(z.shape[-1]) % 2 == 0, prev, nxt)


def _heads(t):
    return t.reshape(t.shape[0], t.shape[1], RW_HEADS, RW_HEAD).astype(F32)


def rwkv_streams(z, shifted, p):
    z = z + (shifted - z) * p['rw_mu']
    cuts = [RW_WIDTH, 2 * RW_WIDTH, 3 * RW_WIDTH, 3 * RW_WIDTH + RW_DECAY_LORA,
            3 * RW_WIDTH + RW_DECAY_LORA + RW_AAA_LORA]
    r, k, v, wl, al, gl = jnp.split(z, cuts, axis=-1)
    g = jax.nn.sigmoid(gl) @ p['rw_w_g2']
    kk = _heads(k * p['rw_k_k'])
    kk = kk * lax.rsqrt(jnp.sum(kk * kk, axis=-1, keepdims=True) + 1e-12)
    return {'r': _heads(r), 'k': k, 'v': _heads(v), 'wl': wl, 'al': al, 'g': g, 'kk': kk}


def rwkv_direction(st, p, d):
    w = -jax.nn.softplus(-(p['rw_w0'][d] + jnp.tanh(st['wl']) @ p['rw_w_w2'][d])) - 0.5
    decay = jnp.exp(-jnp.exp(w.astype(F32)))
    a = jax.nn.sigmoid(p['rw_a0'][d] + st['al'] @ p['rw_w_a2'][d])
    kd = _heads(st['k'] * (1 + (a - 1) * p['rw_k_a']))
    bonus = jnp.sum(st['r'] * kd * p['rw_r_k'].astype(F32), axis=-1, keepdims=True) * st['v']
    tm = lambda t: jnp.moveaxis(t, 1, 0)
    xs = (tm(st['r']), tm(_heads(decay)), tm(kd), tm(st['v']), tm(st['kk']), tm(_heads(a)))
    return xs, bonus


def rwkv_scan(xs, s0, reverse):
    def step(s, inp):
        r, w, k, v, kk, a = inp
        sa = jnp.einsum('bhvk,bhk->bhv', s, -kk)
        s = s * w[:, :, None, :] + sa[..., None] * (kk * a)[:, :, None, :] + v[..., None] * k[:, :, None, :]
        return s, jnp.einsum('bhvk,bhk->bhv', s, r)
    return lax.scan(step, s0, xs, reverse=reverse)


def rwkv_output(ys_f, ys_b, bonus, g, p, dtype):
    y = jnp.moveaxis(ys_f + ys_b, 0, 1)
    mu = jnp.mean(y, axis=-1, keepdims=True)
    var = jnp.mean(jnp.square(y - mu), axis=-1, keepdims=True)
    yn = ((y - mu) * lax.rsqrt(var + GN_EPS) * p['rw_ln_w'].reshape(RW_HEADS, RW_HEAD).astype(F32)
          + p['rw_ln_b'].reshape(RW_HEADS, RW_HEAD).astype(F32))
    out = (yn + bonus).reshape(y.shape[0], y.shape[1], RW_WIDTH) * g
    return out.astype(dtype)


def rwkv7_mixer(z_ctx, z_lat, rows, p, need_ctx):
    st_c = rwkv_streams(z_ctx, shift_seq(z_ctx), p)
    st_l = rwkv_streams(z_lat, shift_grid(z_lat, rows), p)
    s0 = jnp.zeros((z_lat.shape[0], RW_HEADS, RW_HEAD, RW_HEAD), F32)
    ys_l, ys_c, bon_l, bon_c = [], [], [], []
    for d in range(2):
        xs_c, b_c = rwkv_direction(st_c, p, d)
        xs_l, b_l = rwkv_direction(st_l, p, d)
        s_ctx, y_c = rwkv_scan(xs_c, s0, d == 1)
        _, y_l = rwkv_scan(xs_l, s_ctx, d == 1)
        ys_l.append(y_l); bon_l.append(b_l); ys_c.append(y_c); bon_c.append(b_c)
    out_lat = rwkv_output(ys_l[0], ys_l[1], bon_l[0] + bon_l[1], st_l['g'], p, z_lat.dtype)
    out_ctx = rwkv_output(ys_c[0], ys_c[1], bon_c[0] + bon_c[1], st_c['g'], p, z_ctx.dtype) if need_ctx else None
    return out_lat, out_ctx


def peer_ffn(h, p):
    bn, length, dm = h.shape
    w_q, keys, u_tab, v_tab = p['peer_w_q'], p['peer_keys'], p['peer_u'], p['peer_v']

    def block(hb):
        q = (hb @ w_q).reshape(PEER_BLOCK, PEER_HEADS, 2, PEER_HALF)
        s = jnp.einsum('thcd,hcnd->thcn', q, keys)
        s1, i1 = lax.top_k(s[:, :, 0], PEER_TOPK)
        s2, i2 = lax.top_k(s[:, :, 1], PEER_TOPK)
        cand_s = (s1[..., :, None] + s2[..., None, :]).reshape(PEER_BLOCK, PEER_HEADS, PEER_TOPK * PEER_TOPK)
        cand_i = (i1[..., :, None] * PEER_NKEYS + i2[..., None, :]).reshape(PEER_BLOCK, PEER_HEADS, PEER_TOPK * PEER_TOPK)
        best_s, pos = lax.top_k(cand_s, PEER_TOPK)
        idx = jnp.take_along_axis(cand_i, pos, axis=-1)
        gate = jax.nn.softmax(best_s.astype(F32), axis=-1)
        u = jnp.take(u_tab, idx, axis=0)
        act = jax.nn.gelu(jnp.einsum('td,thkd->thk', hb, u).astype(F32), approximate=False)
        v = jnp.take(v_tab, idx, axis=0)
        return jnp.einsum('thk,thkd->td', (gate * act).astype(hb.dtype), v)

    out = lax.map(block, h.reshape(-1, PEER_BLOCK, dm))
    return out.reshape(bn, length, dm).astype(h.dtype)


def hybrid_layer(h, hc, c, c_ctx, p, rows, update_ctx):
    mod = jax.nn.silu(c) @ p['w_ada'] + p['b_ada']
    mod_c = jax.nn.silu(c_ctx) @ p['w_ada'] + p['b_ada']
    sh1, sc1, g1, sh2, sc2, g2 = jnp.split(mod[:, None, :], 6, axis=-1)
    csh1, csc1, cg1, csh2, csc2, cg2 = jnp.split(mod_c, 6, axis=-1)

    hn = modulate(rmsnorm(h, p['norm1_g']), sh1, sc1)
    hcn = modulate(rmsnorm(hc, p['norm1_g']), csh1, csc1)
    proj = hn @ p['w_in']
    proj_c = hcn @ p['w_in']
    s5_l, s5_c = s5_mixer(proj_c[..., :S5_WIDTH], proj[..., :S5_WIDTH], p, update_ctx)
    rw_l, rw_c = rwkv7_mixer(proj_c[..., S5_WIDTH:], proj[..., S5_WIDTH:], rows, p, update_ctx)
    h = h + g1 * (jnp.concatenate([s5_l, rw_l], axis=-1) @ p['w_out'])
    h = h + g2 * peer_ffn(modulate(rmsnorm(h, p['norm2_g']), sh2, sc2), p)
    if update_ctx:
        hc = hc + cg1 * (jnp.concatenate([s5_c, rw_c], axis=-1) @ p['w_out'])
        hc = hc + cg2 * peer_ffn(modulate(rmsnorm(hc, p['norm2_g']), csh2, csc2), p)
    return h, hc


def setup_inputs(seed: int = 0) -> dict:
    key = jax.random.key(seed)
    ks = jax.random.split(key, 40)
    n = lambda i, shape, s: jax.random.normal(ks[i], shape, F32) * s
    G, P, Hc = S5_GROUPS, S5_STATE, S5_GROUP
    a_im_base = jnp.pi * jnp.arange(P, dtype=F32)
    return {
        'x': n(0, (BATCH, SEQ, D_MODEL), 1.0),
        'c': n(1, (BATCH, D_MODEL), 1.0),
        'ctx': n(2, (BATCH, CTX_LEN, D_MODEL), 1.0),
        'c_ctx': n(3, (D_MODEL,), 1.0),
        'w_ada': n(4, (DEPTH, D_MODEL, 6 * D_MODEL), 0.5 * D_MODEL ** -0.5),
        'b_ada': n(5, (DEPTH, 6 * D_MODEL), 0.02),
        'norm1_g': 1.0 + n(6, (DEPTH, D_MODEL), 0.02),
        'norm2_g': 1.0 + n(7, (DEPTH, D_MODEL), 0.02),
        'w_in': n(8, (DEPTH, D_MODEL, IN_COLS), D_MODEL ** -0.5),
        's5_a_re': -0.5 + n(9, (DEPTH, 2, G, P), 0.01),
        's5_a_im': a_im_base + n(10, (DEPTH, 2, G, P), 0.01),
        's5_log_dt': jax.random.uniform(ks[11], (DEPTH, 2, G), F32, math.log(1e-3), math.log(1e-1)),
        's5_b_re': n(12, (DEPTH, 2, G, P, Hc), (2 * Hc) ** -0.5),
        's5_b_im': n(13, (DEPTH, 2, G, P, Hc), (2 * Hc) ** -0.5),
        's5_c_re': n(14, (DEPTH, 2, G, Hc, P), (2 * P) ** -0.5),
        's5_c_im': n(15, (DEPTH, 2, G, Hc, P), (2 * P) ** -0.5),
        's5_d': n(16, (DEPTH, S5_WIDTH), 1.0),
        's5_w_glu': n(17, (DEPTH, S5_WIDTH, S5_WIDTH), S5_WIDTH ** -0.5),
        's5_b_glu': n(18, (DEPTH, S5_WIDTH), 0.02),
        'rw_mu': jax.random.uniform(ks[19], (DEPTH, RW_COLS), F32, 0.0, 1.0),
        'rw_w0': jax.random.uniform(ks[20], (DEPTH, 2, RW_WIDTH), F32, -6.5, -1.5),
        'rw_w_w2': n(21, (DEPTH, 2, RW_DECAY_LORA, RW_WIDTH), 0.1 * RW_DECAY_LORA ** -0.5),
        'rw_a0': n(22, (DEPTH, 2, RW_WIDTH), 0.1),
        'rw_w_a2': n(23, (DEPTH, 2, RW_AAA_LORA, RW_WIDTH), 0.1 * RW_AAA_LORA ** -0.5),
        'rw_w_g2': n(24, (DEPTH, RW_GATE_LORA, RW_WIDTH), RW_GATE_LORA ** -0.5),
        'rw_k_k': 0.85 + n(25, (DEPTH, RW_WIDTH), 0.02),
        'rw_k_a': 1.0 + n(26, (DEPTH, RW_WIDTH), 0.02),
        'rw_r_k': n(27, (DEPTH, RW_HEADS, RW_HEAD), 0.1),
        'rw_ln_w': 1.0 + n(28, (DEPTH, RW_WIDTH), 0.02),
        'rw_ln_b': n(29, (DEPTH, RW_WIDTH), 0.02),
        'w_out': n(30, (DEPTH, D_MODEL, D_MODEL), D_MODEL ** -0.5),
        'peer_w_q': n(31, (DEPTH, D_MODEL, PEER_HEADS * PEER_QDIM), D_MODEL ** -0.5),
        'peer_keys': n(32, (DEPTH, PEER_HEADS, 2, PEER_NKEYS, PEER_HALF), PEER_HALF ** -0.5),
        'peer_u': n(33, (DEPTH, PEER_EXPERTS, D_MODEL), D_MODEL ** -0.5),
        'peer_v': n(34, (DEPTH, PEER_EXPERTS, D_MODEL), 0.5 * PEER_HEADS ** -0.5),
        'norm_f_g': 1.0 + n(35, (D_MODEL,), 0.02),
    }


def reference(x, c, ctx, c_ctx, w_ada, b_ada, norm1_g, norm2_g, w_in, s5_a_re, s5_a_im, s5_log_dt,
              s5_b_re, s5_b_im, s5_c_re, s5_c_im, s5_d, s5_w_glu, s5_b_glu, rw_mu, rw_w0, rw_w_w2,
              rw_a0, rw_w_a2, rw_w_g2, rw_k_k, rw_k_a, rw_r_k, rw_ln_w, rw_ln_b, w_out, peer_w_q,
              peer_keys, peer_u, peer_v, norm_f_g):
    rows = x.shape[1] // GRID_W
    h, hc = x, ctx
    for l in range(DEPTH):
        p = {
            'w_ada': w_ada[l], 'b_ada': b_ada[l], 'norm1_g': norm1_g[l], 'norm2_g': norm2_g[l],
            'w_in': w_in[l], 's5_a_re': s5_a_re[l], 's5_a_im': s5_a_im[l], 's5_log_dt': s5_log_dt[l],
            's5_b_re': s5_b_re[l], 's5_b_im': s5_b_im[l], 's5_c_re': s5_c_re[l], 's5_c_im': s5_c_im[l],
            's5_d': s5_d[l], 's5_w_glu': s5_w_glu[l], 's5_b_glu': s5_b_glu[l], 'rw_mu': rw_mu[l],
            'rw_w0': rw_w0[l], 'rw_w_w2': rw_w_w2[l], 'rw_a0': rw_a0[l], 'rw_w_a2': rw_w_a2[l],
            'rw_w_g2': rw_w_g2[l], 'rw_k_k': rw_k_k[l], 'rw_k_a': rw_k_a[l], 'rw_r_k': rw_r_k[l],
            'rw_ln_w': rw_ln_w[l], 'rw_ln_b': rw_ln_b[l], 'w_out': w_out[l], 'peer_w_q': peer_w_q[l],
            'peer_keys': peer_keys[l], 'peer_u': peer_u[l], 'peer_v': peer_v[l],
        }
        h, hc = hybrid_layer(h, hc, c, c_ctx, p, rows, l < DEPTH - 1)
    return rmsnorm(h, norm_f_g)
```

```python
import functools
import math

import jax
import jax.numpy as jnp
import numpy as np
from jax import lax
from jax.experimental import pallas as pl
from jax.experimental.pallas import tpu as pltpu

F32 = jnp.float32
BF16 = jnp.bfloat16
HIGHEST = lax.Precision.HIGHEST

SUBLANES = 8
LANES = 128
VREG_WORDS = SUBLANES * LANES

GRID_W = 64
NORM_EPS = 1e-6
GN_EPS = 64e-5
KK_EPS = 1e-12
PEER_TOPK = 16

S5_CHUNK = 64
SCAN_TILE = 32
PEER_TILE = 64


def _dot(a, b):
    return jnp.dot(a, b, preferred_element_type=F32)


def _cparams(sem, vmem_mb=48):
    return pltpu.CompilerParams(dimension_semantics=sem, vmem_limit_bytes=vmem_mb * 1024 * 1024)


def _sds(shape, dtype=F32):
    return jax.ShapeDtypeStruct(shape, dtype)


def _mods_kernel(c_ref, w_ref, b_ref, o_ref):
    c = c_ref[...]
    s = c * jax.nn.sigmoid(c)
    o_ref[...] = jnp.dot(s, w_ref[...], precision=HIGHEST, preferred_element_type=F32) + b_ref[...]


def _ada_mods(cc, w_ada, b_ada):
    rows, d = cc.shape
    n = w_ada.shape[1]
    bn = d if n % d == 0 else n
    return pl.pallas_call(
        _mods_kernel,
        grid=(n // bn,),
        in_specs=[pl.BlockSpec((rows, d), lambda j: (0, 0)),
                  pl.BlockSpec((d, bn), lambda j: (0, j)),
                  pl.BlockSpec((1, bn), lambda j: (0, j))],
        out_specs=pl.BlockSpec((rows, bn), lambda j: (0, j)),
        out_shape=_sds((rows, n)),
        compiler_params=_cparams(("arbitrary",)),
        name="ada_mods",
    )(cc, w_ada, b_ada.reshape(1, n))


def _inproj_kernel(x_ref, sh_ref, sc_ref, g_ref, ws_ref, wz_ref, wl_ref, u_ref, z_ref, l_ref):
    x = x_ref[0]
    ms = jnp.mean(x * x, axis=-1, keepdims=True)
    hn = (x * lax.rsqrt(ms + NORM_EPS) * g_ref[...]) * (1.0 + sc_ref[0]) + sh_ref[0]
    hb = hn.astype(BF16)
    u_ref[0] = _dot(hb, ws_ref[...])
    z_ref[0] = _dot(hb, wz_ref[...])
    l_ref[0] = _dot(hb, wl_ref[...])


def _in_proj(x, shift, scale, gain, w_s5, w_main, w_lora):
    b, l, d = x.shape
    tm = min(512, l)
    bm = shift.shape[0]
    mod_map = (lambda i, j: (i, 0, 0)) if bm > 1 else (lambda i, j: (0, 0, 0))
    full = lambda a: pl.BlockSpec(a.shape, lambda i, j: (0,) * a.ndim)
    return pl.pallas_call(
        _inproj_kernel,
        grid=(b, l // tm),
        in_specs=[pl.BlockSpec((1, tm, d), lambda i, j: (i, j, 0)),
                  pl.BlockSpec((1, 1, d), mod_map), pl.BlockSpec((1, 1, d), mod_map),
                  full(gain), full(w_s5), full(w_main), full(w_lora)],
        out_specs=[pl.BlockSpec((1, tm, w.shape[1]), lambda i, j: (i, j, 0)) for w in (w_s5, w_main, w_lora)],
        out_shape=[_sds((b, l, w.shape[1])) for w in (w_s5, w_main, w_lora)],
        compiler_params=_cparams(("parallel", "parallel")),
        name="in_proj",
    )(x, shift, scale, gain, w_s5, w_main, w_lora)


def _seg_matrix(width, seg):
    r = lax.broadcasted_iota(jnp.int32, (width, width), 0) // seg
    c = lax.broadcasted_iota(jnp.int32, (width, width), 1) // seg
    return jnp.where(r == c, 1.0, 0.0).astype(BF16)


def _seg_sum(x, seg_m):
    hi = x.astype(BF16)
    lo = (x - hi.astype(F32)).astype(BF16)
    return _dot(hi, seg_m) + _dot(lo, seg_m)


def _softplus(x):
    return jnp.maximum(x, 0.0) + jnp.log1p(jnp.exp(-jnp.abs(x)))


def _rwkv_streams(zz, zl, p, head, outs, with_gate):
    (wg_ref, ww_ref, wa_ref, w0_ref, a0_ref, kk_ref, ka_ref, rk_ref) = p
    width = kk_ref.shape[-1]
    r = zz[:, :width]
    k = zz[:, width:2 * width]
    v = zz[:, 2 * width:3 * width]
    seg_m = _seg_matrix(width, head)
    kk = k * kk_ref[...]
    kk = kk * lax.rsqrt(_seg_sum(kk * kk, seg_m) + KK_EPS)
    outs["r"][0] = r
    outs["v"][0] = v
    outs["kk"][0] = kk
    tl = jnp.tanh(zl).astype(BF16)
    zb = zl.astype(BF16)
    bonus = None
    for d in range(2):
        w = -_softplus(-(w0_ref[d] + _dot(tl, ww_ref[d]))) - 0.5
        outs["dec"][d, 0] = jnp.exp(-jnp.exp(w))
        a = jax.nn.sigmoid(a0_ref[d] + _dot(zb, wa_ref[d]))
        kd = k * (1.0 + (a - 1.0) * ka_ref[...])
        outs["kd"][d, 0] = kd
        outs["bb"][d, 0] = kk * a
        if with_gate:
            bd = _seg_sum(r * kd * rk_ref[...], seg_m) * v
            bonus = bd if bonus is None else bonus + bd
    if with_gate:
        outs["bonus"][0] = bonus
        outs["g"][0] = _dot(jax.nn.sigmoid(zl).astype(BF16), wg_ref[...])


def _lerp_shift(z, shifted, mu):
    return z + (shifted - z) * mu


def _grid_shift(zc, zp, zn, has_prev, has_next):
    t, c = zc.shape
    tok = lax.broadcasted_iota(jnp.int32, (t, 1), 0) % GRID_W
    col = lax.broadcasted_iota(jnp.int32, (1, c), 1) % 4
    left = jnp.where(tok == 0, 0.0, pltpu.roll(zc, 1, 0))
    right = jnp.where(tok == GRID_W - 1, 0.0, pltpu.roll(zc, t - 1, 0))
    zp = jnp.where(has_prev, zp, 0.0)
    zn = jnp.where(has_next, zn, 0.0)
    if t > GRID_W:
        up = jnp.concatenate([zp, zc[:t - GRID_W]], axis=0)
        down = jnp.concatenate([zc[GRID_W:], zn], axis=0)
    else:
        up, down = zp, zn
    return jnp.where(col == 0, left, jnp.where(col == 1, right, jnp.where(col == 2, up, down)))


def _seq_shift(zc):
    t, c = zc.shape
    tok = lax.broadcasted_iota(jnp.int32, (t, 1), 0)
    col = lax.broadcasted_iota(jnp.int32, (1, c), 1) % 2
    prev = jnp.where(tok == 0, 0.0, pltpu.roll(zc, 1, 0))
    nxt = jnp.where(tok == t - 1, 0.0, pltpu.roll(zc, t - 1, 0))
    return jnp.where(col == 0, prev, nxt)


_STREAMS = ("r", "v", "kk", "dec", "kd", "bb")


def _rwkv_prep_lat_kernel(zc_ref, zp_ref, zn_ref, lc_ref, lp_ref, ln_ref, mum_ref, mul_ref, *rest, head):
    p, o = rest[:8], rest[8:]
    outs = dict(zip(_STREAMS + ("g", "bonus"), o))
    j = pl.program_id(1)
    has_prev = j > 0
    has_next = j < pl.num_programs(1) - 1
    zc = zc_ref[0]
    lc = lc_ref[0]
    zz = _lerp_shift(zc, _grid_shift(zc, zp_ref[0], zn_ref[0], has_prev, has_next), mum_ref[...])
    zl = _lerp_shift(lc, _grid_shift(lc, lp_ref[0], ln_ref[0], has_prev, has_next), mul_ref[...])
    _rwkv_streams(zz, zl, p, head, outs, True)


def _rwkv_prep_ctx_kernel(zc_ref, lc_ref, mum_ref, mul_ref, *rest, head):
    p, o = rest[:8], rest[8:]
    outs = dict(zip(_STREAMS, o))
    zc = zc_ref[0]
    lc = lc_ref[0]
    zz = _lerp_shift(zc, _seq_shift(zc), mum_ref[...])
    zl = _lerp_shift(lc, _seq_shift(lc), mul_ref[...])
    _rwkv_streams(zz, zl, p, head, outs, False)


def _rwkv_prep(z_main, z_lora, params, head, grid_mode):
    b, l, cm = z_main.shape
    cl = z_lora.shape[-1]
    width = cm // 3
    full = lambda a: pl.BlockSpec(a.shape, lambda i, j: (0,) * a.ndim)
    if grid_mode:
        rows_per_tile = min(8, l // GRID_W)
        tt = rows_per_tile * GRID_W
        nt = l // tt
        last_row = l // GRID_W - 1
        cur = lambda c: pl.BlockSpec((1, tt, c), lambda i, j: (i, j, 0))
        prev = lambda c: pl.BlockSpec((1, GRID_W, c), lambda i, j: (i, jnp.maximum(j * rows_per_tile - 1, 0), 0))
        nxt = lambda c: pl.BlockSpec((1, GRID_W, c), lambda i, j: (i, jnp.minimum((j + 1) * rows_per_tile, last_row), 0))
        data = [z_main, z_main, z_main, z_lora, z_lora, z_lora]
        data_specs = [cur(cm), prev(cm), nxt(cm), cur(cl), prev(cl), nxt(cl)]
        kern = functools.partial(_rwkv_prep_lat_kernel, head=head)
        n_single = 5
    else:
        tt, nt = l, 1
        data = [z_main, z_lora]
        data_specs = [pl.BlockSpec((1, tt, cm), lambda i, j: (i, j, 0)), pl.BlockSpec((1, tt, cl), lambda i, j: (i, j, 0))]
        kern = functools.partial(_rwkv_prep_ctx_kernel, head=head)
        n_single = 3
    single = pl.BlockSpec((1, tt, width), lambda i, j: (i, j, 0))
    double = pl.BlockSpec((2, 1, tt, width), lambda i, j: (0, i, j, 0))
    out_specs = [single] * 3 + [double] * 3 + [single] * (n_single - 3)
    out_shape = [_sds((b, l, width))] * 3 + [_sds((2, b, l, width))] * 3 + [_sds((b, l, width))] * (n_single - 3)
    return pl.pallas_call(
        kern,
        grid=(b, nt),
        in_specs=data_specs + [full(a) for a in params],
        out_specs=out_specs,
        out_shape=out_shape,
        compiler_params=_cparams(("parallel", "parallel")),
        name="rwkv_prep_lat" if grid_mode else "rwkv_prep_ctx",
    )(*data, *params)


def _scan_kernel(r_ref, v_ref, kk_ref, w_ref, kd_ref, bb_ref, y_ref, s_ref, *, tt, head):
    d = pl.program_id(0)

    @pl.when(pl.program_id(2) == 0)
    def _():
        s_ref[...] = jnp.zeros_like(s_ref)

    def token(i, carry):
        t = jnp.where(d == 0, i, tt - 1 - i)
        r = r_ref[t]
        kk = kk_ref[t]
        w = w_ref[t]
        kd = kd_ref[t]
        bb = bb_ref[t]
        for vi in range(head):
            s = s_ref[vi]
            sa = -jnp.sum(s * kk, axis=0, keepdims=True)
            vrow = v_ref[t, pl.ds(vi, 1), :]
            s2 = s * w + sa * bb + vrow * kd
            s_ref[vi] = s2
            y_ref[t, pl.ds(vi, 1), :] = jnp.sum(s2 * r, axis=0, keepdims=True)
        return carry

    lax.fori_loop(0, tt, token, 0)


def _rwkv_scan(shared, directional, n_ctx_tok, head):
    ltot, _, chains = shared[0].shape
    tt = SCAN_TILE
    lb = min(chains, LANES)
    nlg = chains // lb
    nt = ltot // tt
    nc = n_ctx_tok // tt

    def tile(d, j):
        back = jnp.where(j < nc, nc - 1 - j, nt + nc - 1 - j)
        return jnp.where(d == 0, j, back)

    sh_spec = pl.BlockSpec((tt, head, lb), lambda d, g, j: (tile(d, j), 0, g))
    di_spec = pl.BlockSpec((None, tt, head, lb), lambda d, g, j: (d, tile(d, j), 0, g))
    return pl.pallas_call(
        functools.partial(_scan_kernel, tt=tt, head=head),
        grid=(2, nlg, nt),
        in_specs=[sh_spec] * 3 + [di_spec] * 3,
        out_specs=di_spec,
        out_shape=_sds((2, ltot, head, chains)),
        scratch_shapes=[pltpu.VMEM((head, head, lb), F32)],
        compiler_params=_cparams(("arbitrary", "arbitrary", "arbitrary")),
        name="rwkv_scan",
    )(*shared, *directional)


def _s5_kernel(x_ref, d_ref, m_ref, pin_ref, pout_ref, ar_ref, ai_ref, y_ref, hall_ref, *, nb, n_chunks, n_ctx_chunks):
    x = x_ref[0]
    xb = x.astype(BF16)
    lat0 = n_ctx_chunks * nb
    xl = xb[lat0:]
    y = d_ref[0] * x[lat0:]
    half = ar_ref.shape[-1] // 2
    fwd = list(range(n_chunks))
    bwd = list(range(n_ctx_chunks - 1, -1, -1)) + list(range(n_chunks - 1, n_ctx_chunks - 1, -1))
    for d, order in enumerate((fwd, bwd)):
        v = _dot(xb, pin_ref[d, 0])
        ar = ar_ref[d, 0]
        ai = ai_ref[d, 0]
        h = jnp.zeros((nb, 2 * half), F32)
        for c in order:
            hall_ref[c * nb:(c + 1) * nb, :] = h
            h = h * ar + pltpu.roll(h, half, 1) * ai + v[c * nb:(c + 1) * nb]
        y = y + _dot(xl, m_ref[d, 0]) + _dot(hall_ref[lat0:, :].astype(BF16), pout_ref[d, 0])
    y_ref[0] = y


def _s5_tables(a_re, a_im, log_dt, b_re, b_im, c_re, c_im, tc):
    a_re, a_im, b_re, b_im = (t.astype(F32) for t in (a_re, a_im, b_re, b_im))
    c_re, c_im = c_re.astype(F32), c_im.astype(F32)
    dt = jnp.exp(log_dt.astype(F32))[..., None]
    mag = jnp.exp(dt * a_re)
    ab_re, ab_im = mag * jnp.cos(dt * a_im), mag * jnp.sin(dt * a_im)
    nr, ni = ab_re - 1.0, ab_im
    den = a_re * a_re + a_im * a_im
    cf_re = (nr * a_re + ni * a_im) / den
    cf_im = (ni * a_re - nr * a_im) / den
    bb_re = cf_re[..., None] * b_re - cf_im[..., None] * b_im
    bb_im = cf_re[..., None] * b_im + cf_im[..., None] * b_re

    k = jnp.arange(tc + 1, dtype=F32)[:, None, None, None]
    pmag = jnp.exp(k * (dt * a_re)[None])
    pang = k * (dt * a_im)[None]
    pw_re, pw_im = pmag * jnp.cos(pang), pmag * jnp.sin(pang)

    ab_k_re = pw_re[..., None] * bb_re[None] - pw_im[..., None] * bb_im[None]
    ab_k_im = pw_re[..., None] * bb_im[None] + pw_im[..., None] * bb_re[None]
    ca_k_re = c_re[None] * pw_re[:, :, :, None, :] - c_im[None] * pw_im[:, :, :, None, :]
    ca_k_im = c_re[None] * pw_im[:, :, :, None, :] + c_im[None] * pw_re[:, :, :, None, :]

    taps = (jnp.einsum('dghp,kdgpj->kdghj', c_re, ab_k_re[:tc], precision=HIGHEST)
            - jnp.einsum('dghp,kdgpj->kdghj', c_im, ab_k_im[:tc], precision=HIGHEST))
    s_idx = jnp.arange(tc)[:, None]
    t_idx = jnp.arange(tc)[None, :]
    hc = b_re.shape[-1]
    g = a_re.shape[1]
    p = a_re.shape[2]

    def toeplitz(tp, lag, valid):
        m = jnp.where(valid[:, :, None, None, None], tp[jnp.clip(lag, 0, tc - 1)], 0.0)
        return m.transpose(2, 0, 4, 1, 3).reshape(g, tc * hc, tc * hc)

    m_f = toeplitz(taps[:, 0], t_idx - s_idx, t_idx >= s_idx)
    m_b = toeplitz(taps[:, 1], s_idx - t_idx, s_idx >= t_idx)
    m = jnp.stack([m_f, m_b]).astype(BF16)

    def pin_of(re, im):
        both = jnp.concatenate([re, im], axis=2)
        return both.transpose(1, 0, 3, 2).reshape(g, tc * hc, 2 * p)

    pin = jnp.stack([pin_of(ab_k_re[:tc, 0][::-1], ab_k_im[:tc, 0][::-1]),
                     pin_of(ab_k_re[:tc, 1], ab_k_im[:tc, 1])]).astype(BF16)

    def pout_of(re, im):
        both = jnp.concatenate([re, -im], axis=3)
        return both.transpose(1, 3, 0, 2).reshape(g, 2 * p, tc * hc)

    pout = jnp.stack([pout_of(ca_k_re[1:tc + 1, 0], ca_k_im[1:tc + 1, 0]),
                      pout_of(ca_k_re[1:tc + 1, 1][::-1], ca_k_im[1:tc + 1, 1][::-1])]).astype(BF16)

    at_re, at_im = pw_re[tc], pw_im[tc]
    ar = jnp.concatenate([at_re, at_re], axis=-1)[:, :, None, :]
    ai = jnp.concatenate([-at_im, at_im], axis=-1)[:, :, None, :]
    return m, pin, pout, ar, ai


def _s5_mix(u_ctx, u_lat, s5_d, tables):
    m, pin, pout, ar, ai = tables
    b, lc, width = u_ctx.shape
    l = u_lat.shape[1]
    g = m.shape[1]
    hc = width // g
    tc = S5_CHUNK
    w = tc * hc
    ncc, nlc = lc // tc, l // tc
    nch = ncc + nlc
    u = jnp.concatenate([u_ctx, u_lat], axis=1)
    xg = u.reshape(b, nch, tc, g, hc).transpose(3, 1, 0, 2, 4).reshape(g, nch * b, w)
    dg = jnp.tile(s5_d.reshape(g, 1, hc), (1, tc, 1)).reshape(g, 1, w)
    pgrp = lambda a: pl.BlockSpec((2, 1) + a.shape[2:], lambda i: (0, i, 0, 0))
    y = pl.pallas_call(
        functools.partial(_s5_kernel, nb=b, n_chunks=nch, n_ctx_chunks=ncc),
        grid=(g,),
        in_specs=[pl.BlockSpec((1, nch * b, w), lambda i: (i, 0, 0)),
                  pl.BlockSpec((1, 1, w), lambda i: (i, 0, 0)),
                  pgrp(m), pgrp(pin), pgrp(pout), pgrp(ar), pgrp(ai)],
        out_specs=pl.BlockSpec((1, nlc * b, w), lambda i: (i, 0, 0)),
        out_shape=_sds((g, nlc * b, w)),
        scratch_shapes=[pltpu.VMEM((nch * b, pin.shape[-1]), F32)],
        compiler_params=_cparams(("parallel",), 56),
        name="s5_mix",
    )(xg, dg, m, pin, pout, ar, ai)
    return y.reshape(g, nlc, b, tc, hc).transpose(2, 1, 3, 0, 4).reshape(b, l, width)


def _gelu(x):
    return 0.5 * x * (1.0 + lax.erf(x * (1.0 / math.sqrt(2.0))))


def _mix_out_kernel(s5_ref, rw_ref, bon_ref, g_ref, x_ref, g1_ref, sh2_ref, sc2_ref, n2_ref,
                    wglu_ref, bglu_ref, lnw_ref, lnb_ref, wos_ref, wor_ref, h1_ref, hn_ref, *, head):
    y1 = _gelu(s5_ref[0])
    s5o = y1 * jax.nn.sigmoid(_dot(y1.astype(BF16), wglu_ref[...]) + bglu_ref[...])
    y = rw_ref[0]
    seg_m = _seg_matrix(y.shape[-1], head)
    inv = 1.0 / head
    mu = _seg_sum(y, seg_m) * inv
    yc = y - mu
    var = _seg_sum(yc * yc, seg_m) * inv
    yn = yc * lax.rsqrt(var + GN_EPS) * lnw_ref[...] + lnb_ref[...]
    rwo = (yn + bon_ref[0]) * g_ref[0]
    mix = _dot(s5o.astype(BF16), wos_ref[...]) + _dot(rwo.astype(BF16), wor_ref[...])
    h1 = x_ref[0] + g1_ref[0] * mix
    h1_ref[0] = h1
    ms = jnp.mean(h1 * h1, axis=-1, keepdims=True)
    hn_ref[0] = (h1 * lax.rsqrt(ms + NORM_EPS) * n2_ref[...]) * (1.0 + sc2_ref[0]) + sh2_ref[0]


def _mix_out(s5y, rwy, bonus, gate, x, g1, sh2, sc2, n2g, w_glu, b_glu, ln_w, ln_b, w_out_s, w_out_r, head):
    b, l, d = x.shape
    width = s5y.shape[-1]
    tm = min(512, l)
    tok = lambda c: pl.BlockSpec((1, tm, c), lambda i, j: (i, j, 0))
    mod = pl.BlockSpec((1, 1, d), lambda i, j: (i, 0, 0))
    full = lambda a: pl.BlockSpec(a.shape, lambda i, j: (0,) * a.ndim)
    consts = (n2g, w_glu, b_glu, ln_w, ln_b, w_out_s, w_out_r)
    return pl.pallas_call(
        functools.partial(_mix_out_kernel, head=head),
        grid=(b, l // tm),
        in_specs=[tok(width)] * 4 + [tok(d), mod, mod, mod] + [full(a) for a in consts],
        out_specs=[tok(d), tok(d)],
        out_shape=[_sds((b, l, d)), _sds((b, l, d))],
        compiler_params=_cparams(("parallel", "parallel")),
        name="mix_out",
    )(s5y, rwy, bonus, gate, x, g1, sh2, sc2, *consts)


def _staircase(k):
    return [(a, b) for a in range(k) for b in range(k) if (a + 1) * (b + 1) <= k]


def _topk_rows(s, k):
    rows = s.shape[0]
    iota = lax.broadcasted_iota(jnp.int32, s.shape, 0)
    vals, idxs = [], []
    for _ in range(k):
        m = jnp.max(s, axis=0, keepdims=True)
        idx = jnp.min(jnp.where(s == m, iota, rows), axis=0, keepdims=True)
        vals.append(m)
        idxs.append(idx)
        s = jnp.where(iota == idx, -jnp.inf, s)
    return jnp.concatenate(vals, axis=0), jnp.concatenate(idxs, axis=0).astype(F32)


def _peer_topk_kernel(hn_ref, wq_ref, keys_ref, sel1_ref, sel2_ref, pad_ref, idx_ref, gate_ref, *, heads, nkeys, half):
    hb = hn_ref[...].astype(BF16)
    nt = (((1,), (1,)), ((), ()))
    sel1 = sel1_ref[...]
    sel2 = sel2_ref[...]
    pick = lambda sel, a: jnp.dot(sel, a, precision=HIGHEST, preferred_element_type=F32)
    ids, gates = [], []
    for h in range(heads):
        top = []
        for c in range(2):
            row0 = (h * 2 + c) * half
            qt = lax.dot_general(wq_ref[row0:row0 + half, :], hb, nt, preferred_element_type=F32)
            st = _dot(keys_ref[h * 2 + c], qt.astype(BF16))
            top.append(_topk_rows(st, PEER_TOPK))
        (s1, i1), (s2, i2) = top
        cand = pick(sel1, s1) + pick(sel2, s2) + pad_ref[...]
        cid = pick(sel1, i1) * nkeys + pick(sel2, i2)
        best, pos = _topk_rows(cand, PEER_TOPK)
        riota = lax.broadcasted_iota(jnp.int32, cid.shape, 0).astype(F32)
        eid = jnp.concatenate(
            [jnp.sum(jnp.where(riota == pos[r:r + 1], cid, 0.0), axis=0, keepdims=True) for r in range(PEER_TOPK)], axis=0)
        e = jnp.exp(best - jnp.max(best, axis=0, keepdims=True))
        gates.append(e / jnp.sum(e, axis=0, keepdims=True))
        ids.append(eid)
    idx_ref[...] = jnp.concatenate(ids, axis=0).T.astype(jnp.int32)
    gate_ref[...] = jnp.concatenate(gates, axis=0).T


def _peer_topk(hn, wq_t, keys):
    n, d = hn.shape
    heads = keys.shape[0] // 2
    nkeys, half = keys.shape[1], keys.shape[2]
    tm = min(256, n)
    pairs = _staircase(PEER_TOPK)
    rows = -(-len(pairs) // SUBLANES) * SUBLANES
    sel1 = np.zeros((rows, PEER_TOPK), np.float32)
    sel2 = np.zeros((rows, PEER_TOPK), np.float32)
    pad = np.zeros((rows, 1), np.float32)
    for i, (a, b) in enumerate(pairs):
        sel1[i, a] = 1.0
        sel2[i, b] = 1.0
    pad[len(pairs):] = -np.inf
    slots = heads * PEER_TOPK
    full = lambda a: pl.BlockSpec(a.shape, lambda i: (0,) * a.ndim)
    consts = (wq_t, keys, jnp.asarray(sel1), jnp.asarray(sel2), jnp.asarray(pad))
    return pl.pallas_call(
        functools.partial(_peer_topk_kernel, heads=heads, nkeys=nkeys, half=half),
        grid=(n // tm,),
        in_specs=[pl.BlockSpec((tm, d), lambda i: (i, 0))] + [full(a) for a in consts],
        out_specs=[pl.BlockSpec((tm, slots), lambda i: (i, 0)), pl.BlockSpec((tm, slots), lambda i: (i, 0))],
        out_shape=[_sds((n, slots), jnp.int32), _sds((n, slots))],
        compiler_params=_cparams(("parallel",)),
        name="peer_topk",
    )(hn, *consts)


def _pack_table(tab):
    e, d = tab.shape
    bits = lax.bitcast_convert_type(tab.astype(BF16), jnp.uint16).astype(jnp.uint32).reshape(e // 2, 2, d)
    return ((bits[:, 0] << 16) | bits[:, 1]).reshape(e // 2, d // LANES, LANES)


def _expert_row(tab_ref, e):
    bits = tab_ref[e >> 1]
    sh = ((e & 1) << 4).astype(jnp.uint32)
    return pltpu.bitcast(lax.shift_left(bits, jnp.broadcast_to(sh, bits.shape)) & jnp.uint32(0xFFFF0000), F32)


_BFLY_ROW = (1, 5, 3, 7, 0, 4, 2, 6)


def _sublane_sums(ps, sub):
    a = []
    for i in range(0, 8, 2):
        x = ps[i] + pltpu.roll(ps[i], 4, 0)
        y = ps[i + 1] + pltpu.roll(ps[i + 1], 4, 0)
        a.append(jnp.where(sub < 4, x, y))
    b = []
    for i in range(0, 4, 2):
        x = a[i] + pltpu.roll(a[i], 6, 0)
        y = a[i + 1] + pltpu.roll(a[i + 1], 6, 0)
        b.append(jnp.where(sub % 4 < 2, x, pltpu.roll(y, 2, 0)))
    x = b[0] + pltpu.roll(b[0], 1, 0)
    y = b[1] + pltpu.roll(b[1], 1, 0)
    return jnp.where(sub % 2 == 1, x, pltpu.roll(y, 7, 0))


def _peer_u_kernel(idx_ref, h_ref, gate_ref, tab_ref, w_ref, q_ref, *, tq, slots):
    sub = lax.broadcasted_iota(jnp.int32, (SUBLANES, LANES), 0)
    ones = jnp.ones((SUBLANES, LANES), BF16)
    nt = (((1,), (1,)), ((), ()))

    def token(t, carry):
        h = h_ref[t]
        for g8 in range(slots // 8):
            ps = []
            for p in range(8):
                e = idx_ref[0, 0, t * slots + g8 * 8 + _BFLY_ROW[p]]
                ps.append(_expert_row(tab_ref, e) * h)
            q_ref[g8 * 8:(g8 + 1) * 8, :] = _sublane_sums(ps, sub)
        q = q_ref[...]
        qh = q.astype(BF16)
        ql = (q - qh.astype(F32)).astype(BF16)
        act = (lax.dot_general(ones, qh, nt, preferred_element_type=F32)
               + lax.dot_general(ones, ql, nt, preferred_element_type=F32))[0:1]
        w_ref[pl.ds(t, 1), :] = gate_ref[pl.ds(t, 1), :] * _gelu(act)
        return carry

    lax.fori_loop(0, tq, token, 0)


def _peer_u(idx_blocks, hn3, gate, tab):
    n, rows, lanes = hn3.shape
    slots = gate.shape[1]
    tq = idx_blocks.shape[-1] // slots
    return pl.pallas_call(
        functools.partial(_peer_u_kernel, tq=tq, slots=slots),
        grid=(n // tq,),
        in_specs=[pl.BlockSpec((1, 1, tq * slots), lambda i: (i, 0, 0), memory_space=pltpu.SMEM),
                  pl.BlockSpec((tq, rows, lanes), lambda i: (i, 0, 0)),
                  pl.BlockSpec((tq, slots), lambda i: (i, 0)),
                  pl.BlockSpec(tab.shape, lambda i: (0, 0, 0), pipeline_mode=pl.Buffered(1))],
        out_specs=pl.BlockSpec((tq, slots), lambda i: (i, 0)),
        out_shape=_sds((n, slots)),
        scratch_shapes=[pltpu.VMEM((slots, lanes), F32)],
        compiler_params=_cparams(("parallel",), 52),
        name="peer_u",
    )(idx_blocks, hn3, gate, tab)


def _peer_v_kernel(idx_ref, w_ref, tab_ref, o_ref, *, tq, slots):
    def token(t, carry):
        acc = [jnp.zeros(o_ref.shape[1:], F32), jnp.zeros(o_ref.shape[1:], F32)]
        for s in range(slots):
            e = idx_ref[0, 0, t * slots + s]
            acc[s % 2] = acc[s % 2] + _expert_row(tab_ref, e) * w_ref[0, 0, t * slots + s]
        o_ref[t] = acc[0] + acc[1]
        return carry

    lax.fori_loop(0, tq, token, 0)


def _peer_v(idx_blocks, w_blocks, tab, n, slots):
    tq = idx_blocks.shape[-1] // slots
    rows, lanes = tab.shape[1:]
    smem = pl.BlockSpec((1, 1, tq * slots), lambda i: (i, 0, 0), memory_space=pltpu.SMEM)
    return pl.pallas_call(
        functools.partial(_peer_v_kernel, tq=tq, slots=slots),
        grid=(n // tq,),
        in_specs=[smem, smem, pl.BlockSpec(tab.shape, lambda i: (0, 0, 0), pipeline_mode=pl.Buffered(1))],
        out_specs=pl.BlockSpec((tq, rows, lanes), lambda i: (i, 0, 0)),
        out_shape=_sds((n, rows, lanes)),
        compiler_params=_cparams(("parallel",), 52),
        name="peer_v",
    )(idx_blocks, w_blocks, tab)


def _final_kernel(h1_ref, p_ref, g2_ref, nf_ref, o_ref):
    h = h1_ref[0] + g2_ref[0] * p_ref[0]
    ms = jnp.mean(h * h, axis=-1, keepdims=True)
    o_ref[0] = h * lax.rsqrt(ms + NORM_EPS) * nf_ref[...]


def _final_norm(h1, peer, g2, nfg):
    b, l, d = h1.shape
    tm = min(512, l)
    tok = pl.BlockSpec((1, tm, d), lambda i, j: (i, j, 0))
    return pl.pallas_call(
        _final_kernel,
        grid=(b, l // tm),
        in_specs=[tok, tok, pl.BlockSpec((1, 1, d), lambda i, j: (i, 0, 0)), pl.BlockSpec((1, d), lambda i, j: (0, 0))],
        out_specs=tok,
        out_shape=_sds((b, l, d)),
        compiler_params=_cparams(("parallel", "parallel")),
        name="final_norm",
    )(h1, peer, g2, nfg)


def _pad_cols(a, n):
    return jnp.pad(a, [(0, 0)] * (a.ndim - 1) + [(0, n - a.shape[-1])])


def _to_scan(a, heads):
    *lead, b, t, w = a.shape
    n = w // heads
    a = a.reshape(*lead, b, t, heads, n)
    nl = len(lead)
    perm = tuple(range(nl)) + (nl + 1, nl + 3, nl + 2, nl)
    return a.transpose(perm).reshape(*lead, t, n, heads * b)


def _from_scan(y, heads):
    t, n, c = y.shape
    b = c // heads
    return y.reshape(t, n, heads, b).transpose(3, 0, 2, 1).reshape(b, t, heads * n)


def _layer(h, hc, c, c_ctx, p, norm_f_g):
    b, l, d = h.shape
    lc = hc.shape[1]
    s5w = p['s5_d'].shape[-1]
    rww = p['rw_w0'].shape[-1]
    heads, head = p['rw_r_k'].shape
    n_dec, n_aaa, n_gate = p['rw_w_w2'].shape[1], p['rw_w_a2'].shape[1], p['rw_w_g2'].shape[0]
    n_lora = n_dec + n_aaa + n_gate
    lora_pad = -(-n_lora // LANES) * LANES

    rows = -(-(b + 1) // SUBLANES) * SUBLANES
    cc = jnp.zeros((rows, d), F32).at[:b].set(c).at[b].set(c_ctx)
    mods = _ada_mods(cc, p['w_ada'], p['b_ada'])
    sh1, sc1, g1, sh2, sc2, g2 = [m.reshape(rows, 1, d) for m in jnp.split(mods, 6, axis=-1)]
    lat = lambda m: m[:b]
    ctxm = lambda m: m[b:b + 1]

    w_in = p['w_in']
    w_s5 = w_in[:, :s5w].astype(BF16)
    w_main = w_in[:, s5w:s5w + 3 * rww].astype(BF16)
    w_lora = _pad_cols(w_in[:, s5w + 3 * rww:], lora_pad).astype(BF16)
    n1g = p['norm1_g'].reshape(1, d)
    u_l, zm_l, zl_l = _in_proj(h, lat(sh1), lat(sc1), n1g, w_s5, w_main, w_lora)
    u_c, zm_c, zl_c = _in_proj(hc, ctxm(sh1), ctxm(sc1), n1g, w_s5, w_main, w_lora)

    mu = p['rw_mu']
    mu_main = mu[:3 * rww].reshape(1, -1)
    mu_lora = _pad_cols(mu[3 * rww:], lora_pad).reshape(1, -1)
    place = lambda w, r0: jnp.zeros(w.shape[:-2] + (lora_pad, rww), F32).at[..., r0:r0 + w.shape[-2], :].set(w).astype(BF16)
    rw_params = (place(p['rw_w_g2'], n_dec + n_aaa), place(p['rw_w_w2'], 0), place(p['rw_w_a2'], n_dec),
                 p['rw_w0'].reshape(2, 1, rww), p['rw_a0'].reshape(2, 1, rww),
                 p['rw_k_k'].reshape(1, rww), p['rw_k_a'].reshape(1, rww), p['rw_r_k'].reshape(1, rww))
    r_l, v_l, kk_l, dec_l, kd_l, bb_l, gate_l, bonus_l = _rwkv_prep(zm_l, zl_l, (mu_main, mu_lora) + rw_params, head, True)
    r_c, v_c, kk_c, dec_c, kd_c, bb_c = _rwkv_prep(zm_c, zl_c, (mu_main, mu_lora) + rw_params, head, False)
    cat = lambda a_c, a_l: _to_scan(jnp.concatenate([a_c, a_l], axis=-2), heads)
    ys = _rwkv_scan((cat(r_c, r_l), cat(v_c, v_l), cat(kk_c, kk_l)),
                    (cat(dec_c, dec_l), cat(kd_c, kd_l), cat(bb_c, bb_l)), lc, head)
    rw_y = _from_scan((ys[0] + ys[1])[lc:], heads)

    tables = _s5_tables(p['s5_a_re'], p['s5_a_im'], p['s5_log_dt'], p['s5_b_re'], p['s5_b_im'],
                        p['s5_c_re'], p['s5_c_im'], S5_CHUNK)
    s5_y = _s5_mix(u_c, u_l, p['s5_d'], tables)

    w_out = p['w_out']
    h1, hn2 = _mix_out(s5_y, rw_y, bonus_l, gate_l, h, lat(g1), lat(sh2), lat(sc2), p['norm2_g'].reshape(1, d),
                       p['s5_w_glu'].astype(BF16), p['s5_b_glu'].reshape(1, s5w),
                       p['rw_ln_w'].reshape(1, rww), p['rw_ln_b'].reshape(1, rww),
                       w_out[:s5w].astype(BF16), w_out[s5w:].astype(BF16), head)

    n = b * l
    keys = p['peer_keys']
    pheads, _, nkeys, half = keys.shape
    idx, gate = _peer_topk(hn2.reshape(n, d), p['peer_w_q'].T.astype(BF16),
                           keys.reshape(pheads * 2, nkeys, half).astype(BF16))
    slots = idx.shape[1]
    tq = min(PEER_TILE, n)
    idx_blocks = idx.reshape(n // tq, 1, tq * slots)
    w = _peer_u(idx_blocks, hn2.reshape(n, d // LANES, LANES), gate, _pack_table(p['peer_u']))
    peer = _peer_v(idx_blocks, w.reshape(n // tq, 1, tq * slots), _pack_table(p['peer_v']), n, slots)

    return _final_norm(h1, peer.reshape(b, l, d), lat(g2), norm_f_g.reshape(1, d))


def kernel(x, c, ctx, c_ctx, w_ada, b_ada, norm1_g, norm2_g, w_in, s5_a_re, s5_a_im, s5_log_dt, s5_b_re, s5_b_im, s5_c_re, s5_c_im, s5_d, s5_w_glu, s5_b_glu, rw_mu, rw_w0, rw_w_w2, rw_a0, rw_w_a2, rw_w_g2, rw_k_k, rw_k_a, rw_r_k, rw_ln_w, rw_ln_b, w_out, peer_w_q, peer_keys, peer_u, peer_v, norm_f_g):
    assert w_ada.shape[0] == 1, "single-layer block: the context stream is never updated"
    p = dict(w_ada=w_ada[0], b_ada=b_ada[0], norm1_g=norm1_g[0], norm2_g=norm2_g[0], w_in=w_in[0],
             s5_a_re=s5_a_re[0], s5_a_im=s5_a_im[0], s5_log_dt=s5_log_dt[0], s5_b_re=s5_b_re[0], s5_b_im=s5_b_im[0],
             s5_c_re=s5_c_re[0], s5_c_im=s5_c_im[0], s5_d=s5_d[0], s5_w_glu=s5_w_glu[0], s5_b_glu=s5_b_glu[0],
             rw_mu=rw_mu[0], rw_w0=rw_w0[0], rw_w_w2=rw_w_w2[0], rw_a0=rw_a0[0], rw_w_a2=rw_w_a2[0],
             rw_w_g2=rw_w_g2[0], rw_k_k=rw_k_k[0], rw_k_a=rw_k_a[0], rw_r_k=rw_r_k[0], rw_ln_w=rw_ln_w[0],
             rw_ln_b=rw_ln_b[0], w_out=w_out[0], peer_w_q=peer_w_q[0], peer_keys=peer_keys[0],
             peer_u=peer_u[0], peer_v=peer_v[0])
    return _layer(x, ctx, c, c_ctx, p, norm_f_g)
```

```python
import functools
import math

import jax
import jax.numpy as jnp
import numpy as np
from jax import lax
from jax.experimental import pallas as pl
from jax.experimental.pallas import tpu as pltpu

F32 = jnp.float32
BF16 = jnp.bfloat16
HIGHEST = lax.Precision.HIGHEST

SUBLANES = 8
LANES = 128
VREG_WORDS = SUBLANES * LANES

GRID_W = 64
NORM_EPS = 1e-6
GN_EPS = 64e-5
KK_EPS = 1e-12
PEER_TOPK = 16

S5_CHUNK = 64
SCAN_TILE = 32
PEER_TILE = 64


def _dot(a, b):
    return jnp.dot(a, b, preferred_element_type=F32)


def _cparams(sem, vmem_mb=48):
    return pltpu.CompilerParams(dimension_semantics=sem, vmem_limit_bytes=vmem_mb * 1024 * 1024)


def _sds(shape, dtype=F32):
    return jax.ShapeDtypeStruct(shape, dtype)


def _mods_kernel(c_ref, w_ref, b_ref, o_ref):
    c = c_ref[...]
    s = c * jax.nn.sigmoid(c)
    o_ref[...] = jnp.dot(s, w_ref[...], precision=HIGHEST, preferred_element_type=F32) + b_ref[...]


def _ada_mods(cc, w_ada, b_ada):
    rows, d = cc.shape
    n = w_ada.shape[1]
    bn = d if n % d == 0 else n
    return pl.pallas_call(
        _mods_kernel,
        grid=(n // bn,),
        in_specs=[pl.BlockSpec((rows, d), lambda j: (0, 0)),
                  pl.BlockSpec((d, bn), lambda j: (0, j)),
                  pl.BlockSpec((1, bn), lambda j: (0, j))],
        out_specs=pl.BlockSpec((rows, bn), lambda j: (0, j)),
        out_shape=_sds((rows, n)),
        compiler_params=_cparams(("arbitrary",)),
        name="ada_mods",
    )(cc, w_ada, b_ada.reshape(1, n))


def _inproj_kernel(x_ref, sh_ref, sc_ref, g_ref, ws_ref, wz_ref, wl_ref, u_ref, z_ref, l_ref):
    x = x_ref[0]
    ms = jnp.mean(x * x, axis=-1, keepdims=True)
    hn = (x * lax.rsqrt(ms + NORM_EPS) * g_ref[...]) * (1.0 + sc_ref[0]) + sh_ref[0]
    hb = hn.astype(BF16)
    u_ref[0] = _dot(hb, ws_ref[...])
    z_ref[0] = _dot(hb, wz_ref[...])
    l_ref[0] = _dot(hb, wl_ref[...])


def _in_proj(x, shift, scale, gain, w_s5, w_main, w_lora):
    b, l, d = x.shape
    tm = min(512, l)
    bm = shift.shape[0]
    mod_map = (lambda i, j: (i, 0, 0)) if bm > 1 else (lambda i, j: (0, 0, 0))
    full = lambda a: pl.BlockSpec(a.shape, lambda i, j: (0,) * a.ndim)
    return pl.pallas_call(
        _inproj_kernel,
        grid=(b, l // tm),
        in_specs=[pl.BlockSpec((1, tm, d), lambda i, j: (i, j, 0)),
                  pl.BlockSpec((1, 1, d), mod_map), pl.BlockSpec((1, 1, d), mod_map),
                  full(gain), full(w_s5), full(w_main), full(w_lora)],
        out_specs=[pl.BlockSpec((1, tm, w.shape[1]), lambda i, j: (i, j, 0)) for w in (w_s5, w_main, w_lora)],
        out_shape=[_sds((b, l, w.shape[1])) for w in (w_s5, w_main, w_lora)],
        compiler_params=_cparams(("parallel", "parallel")),
        name="in_proj",
    )(x, shift, scale, gain, w_s5, w_main, w_lora)


def _seg_matrix(width, seg):
    r = lax.broadcasted_iota(jnp.int32, (width, width), 0) // seg
    c = lax.broadcasted_iota(jnp.int32, (width, width), 1) // seg
    return jnp.where(r == c, 1.0, 0.0).astype(BF16)


def _seg_sum(x, seg_m):
    hi = x.astype(BF16)
    lo = (x - hi.astype(F32)).astype(BF16)
    return _dot(hi, seg_m) + _dot(lo, seg_m)


def _softplus(x):
    return jnp.maximum(x, 0.0) + jnp.log1p(jnp.exp(-jnp.abs(x)))


def _rwkv_streams(zz, zl, p, head, outs, with_gate):
    (wg_ref, ww_ref, wa_ref, w0_ref, a0_ref, kk_ref, ka_ref, rk_ref) = p
    width = kk_ref.shape[-1]
    r = zz[:, :width]
    k = zz[:, width:2 * width]
    v = zz[:, 2 * width:3 * width]
    seg_m = _seg_matrix(width, head)
    kk = k * kk_ref[...]
    kk = kk * lax.rsqrt(_seg_sum(kk * kk, seg_m) + KK_EPS)
    outs["r"][0] = r
    outs["v"][0] = v
    outs["kk"][0] = kk
    tl = jnp.tanh(zl).astype(BF16)
    zb = zl.astype(BF16)
    bonus = None
    for d in range(2):
        w = -_softplus(-(w0_ref[d] + _dot(tl, ww_ref[d]))) - 0.5
        outs["dec"][d, 0] = jnp.exp(-jnp.exp(w))
        a = jax.nn.sigmoid(a0_ref[d] + _dot(zb, wa_ref[d]))
        kd = k * (1.0 + (a - 1.0) * ka_ref[...])
        outs["kd"][d, 0] = kd
        outs["bb"][d, 0] = kk * a
        if with_gate:
            bd = _seg_sum(r * kd * rk_ref[...], seg_m) * v
            bonus = bd if bonus is None else bonus + bd
    if with_gate:
        outs["bonus"][0] = bonus
        outs["g"][0] = _dot(jax.nn.sigmoid(zl).astype(BF16), wg_ref[...])


def _lerp_shift(z, shifted, mu):
    return z + (shifted - z) * mu


def _grid_shift(zc, zp, zn, has_prev, has_next):
    t, c = zc.shape
    tok = lax.broadcasted_iota(jnp.int32, (t, 1), 0) % GRID_W
    col = lax.broadcasted_iota(jnp.int32, (1, c), 1) % 4
    left = jnp.where(tok == 0, 0.0, pltpu.roll(zc, 1, 0))
    right = jnp.where(tok == GRID_W - 1, 0.0, pltpu.roll(zc, t - 1, 0))
    zp = jnp.where(has_prev, zp, 0.0)
    zn = jnp.where(has_next, zn, 0.0)
    if t > GRID_W:
        up = jnp.concatenate([zp, zc[:t - GRID_W]], axis=0)
        down = jnp.concatenate([zc[GRID_W:], zn], axis=0)
    else:
        up, down = zp, zn
    return jnp.where(col == 0, left, jnp.where(col == 1, right, jnp.where(col == 2, up, down)))


def _seq_shift(zc):
    t, c = zc.shape
    tok = lax.broadcasted_iota(jnp.int32, (t, 1), 0)
    col = lax.broadcasted_iota(jnp.int32, (1, c), 1) % 2
    prev = jnp.where(tok == 0, 0.0, pltpu.roll(zc, 1, 0))
    nxt = jnp.where(tok == t - 1, 0.0, pltpu.roll(zc, t - 1, 0))
    return jnp.where(col == 0, prev, nxt)


_STREAMS = ("r", "v", "kk", "dec", "kd", "bb")


def _rwkv_prep_lat_kernel(zc_ref, zp_ref, zn_ref, lc_ref, lp_ref, ln_ref, mum_ref, mul_ref, *rest, head):
    p, o = rest[:8], rest[8:]
    outs = dict(zip(_STREAMS + ("g", "bonus"), o))
    j = pl.program_id(1)
    has_prev = j > 0
    has_next = j < pl.num_programs(1) - 1
    zc = zc_ref[0]
    lc = lc_ref[0]
    zz = _lerp_shift(zc, _grid_shift(zc, zp_ref[0], zn_ref[0], has_prev, has_next), mum_ref[...])
    zl = _lerp_shift(lc, _grid_shift(lc, lp_ref[0], ln_ref[0], has_prev, has_next), mul_ref[...])
    _rwkv_streams(zz, zl, p, head, outs, True)


def _rwkv_prep_ctx_kernel(zc_ref, lc_ref, mum_ref, mul_ref, *rest, head):
    p, o = rest[:8], rest[8:]
    outs = dict(zip(_STREAMS, o))
    zc = zc_ref[0]
    lc = lc_ref[0]
    zz = _lerp_shift(zc, _seq_shift(zc), mum_ref[...])
    zl = _lerp_shift(lc, _seq_shift(lc), mul_ref[...])
    _rwkv_streams(zz, zl, p, head, outs, False)


def _rwkv_prep(z_main, z_lora, params, head, grid_mode):
    b, l, cm = z_main.shape
    cl = z_lora.shape[-1]
    width = cm // 3
    full = lambda a: pl.BlockSpec(a.shape, lambda i, j: (0,) * a.ndim)
    if grid_mode:
        rows_per_tile = min(8, l // GRID_W)
        tt = rows_per_tile * GRID_W
        nt = l // tt
        last_row = l // GRID_W - 1
        cur = lambda c: pl.BlockSpec((1, tt, c), lambda i, j: (i, j, 0))
        prev = lambda c: pl.BlockSpec((1, GRID_W, c), lambda i, j: (i, jnp.maximum(j * rows_per_tile - 1, 0), 0))
        nxt = lambda c: pl.BlockSpec((1, GRID_W, c), lambda i, j: (i, jnp.minimum((j + 1) * rows_per_tile, last_row), 0))
        data = [z_main, z_main, z_main, z_lora, z_lora, z_lora]
        data_specs = [cur(cm), prev(cm), nxt(cm), cur(cl), prev(cl), nxt(cl)]
        kern = functools.partial(_rwkv_prep_lat_kernel, head=head)
        n_single = 5
    else:
        tt, nt = l, 1
        data = [z_main, z_lora]
        data_specs = [pl.BlockSpec((1, tt, cm), lambda i, j: (i, j, 0)), pl.BlockSpec((1, tt, cl), lambda i, j: (i, j, 0))]
        kern = functools.partial(_rwkv_prep_ctx_kernel, head=head)
        n_single = 3
    single = pl.BlockSpec((1, tt, width), lambda i, j: (i, j, 0))
    double = pl.BlockSpec((2, 1, tt, width), lambda i, j: (0, i, j, 0))
    out_specs = [single] * 3 + [double] * 3 + [single] * (n_single - 3)
    out_shape = [_sds((b, l, width))] * 3 + [_sds((2, b, l, width))] * 3 + [_sds((b, l, width))] * (n_single - 3)
    return pl.pallas_call(
        kern,
        grid=(b, nt),
        in_specs=data_specs + [full(a) for a in params],
        out_specs=out_specs,
        out_shape=out_shape,
        compiler_params=_cparams(("parallel", "parallel")),
        name="rwkv_prep_lat" if grid_mode else "rwkv_prep_ctx",
    )(*data, *params)


def _scan_kernel(r_ref, v_ref, kk_ref, w_ref, kd_ref, bb_ref, y_ref, s_ref, *, tt, head):
    d = pl.program_id(0)

    @pl.when(pl.program_id(2) == 0)
    def _():
        s_ref[...] = jnp.zeros_like(s_ref)

    def token(i, carry):
        t = jnp.where(d == 0, i, tt - 1 - i)
        r = r_ref[t]
        kk = kk_ref[t]
        w = w_ref[t]
        kd = kd_ref[t]
        bb = bb_ref[t]
        for vi in range(head):
            s = s_ref[vi]
            sa = -jnp.sum(s * kk, axis=0, keepdims=True)
            vrow = v_ref[t, pl.ds(vi, 1), :]
            s2 = s * w + sa * bb + vrow * kd
            s_ref[vi] = s2
            y_ref[t, pl.ds(vi, 1), :] = jnp.sum(s2 * r, axis=0, keepdims=True)
        return carry

    lax.fori_loop(0, tt, token, 0)


def _rwkv_scan(shared, directional, n_ctx_tok, head):
    ltot, _, chains = shared[0].shape
    tt = SCAN_TILE
    lb = min(chains, LANES)
    nlg = chains // lb
    nt = ltot // tt
    nc = n_ctx_tok // tt

    def tile(d, j):
        back = jnp.where(j < nc, nc - 1 - j, nt + nc - 1 - j)
        return jnp.where(d == 0, j, back)

    sh_spec = pl.BlockSpec((tt, head, lb), lambda d, g, j: (tile(d, j), 0, g))
    di_spec = pl.BlockSpec((None, tt, head, lb), lambda d, g, j: (d, tile(d, j), 0, g))
    return pl.pallas_call(
        functools.partial(_scan_kernel, tt=tt, head=head),
        grid=(2, nlg, nt),
        in_specs=[sh_spec] * 3 + [di_spec] * 3,
        out_specs=di_spec,
        out_shape=_sds((2, ltot, head, chains)),
        scratch_shapes=[pltpu.VMEM((head, head, lb), F32)],
        compiler_params=_cparams(("arbitrary", "arbitrary", "arbitrary")),
        name="rwkv_scan",
    )(*shared, *directional)


def _s5_kernel(x_ref, d_ref, m_ref, pin_ref, pout_ref, ar_ref, ai_ref, y_ref, hall_ref, *, nb, n_chunks, n_ctx_chunks):
    x = x_ref[0]
    xb = x.astype(BF16)
    lat0 = n_ctx_chunks * nb
    xl = xb[lat0:]
    y = d_ref[0] * x[lat0:]
    half = ar_ref.shape[-1] // 2
    fwd = list(range(n_chunks))
    bwd = list(range(n_ctx_chunks - 1, -1, -1)) + list(range(n_chunks - 1, n_ctx_chunks - 1, -1))
    for d, order in enumerate((fwd, bwd)):
        v = _dot(xb, pin_ref[d, 0])
        ar = ar_ref[d, 0]
        ai = ai_ref[d, 0]
        h = jnp.zeros((nb, 2 * half), F32)
        for c in order:
            hall_ref[c * nb:(c + 1) * nb, :] = h
            h = h * ar + pltpu.roll(h, half, 1) * ai + v[c * nb:(c + 1) * nb]
        y = y + _dot(xl, m_ref[d, 0]) + _dot(hall_ref[lat0:, :].astype(BF16), pout_ref[d, 0])
    y_ref[0] = y


def _s5_tables(a_re, a_im, log_dt, b_re, b_im, c_re, c_im, tc):
    a_re, a_im, b_re, b_im = (t.astype(F32) for t in (a_re, a_im, b_re, b_im))
    c_re, c_im = c_re.astype(F32), c_im.astype(F32)
    dt = jnp.exp(log_dt.astype(F32))[..., None]
    mag = jnp.exp(dt * a_re)
    ab_re, ab_im = mag * jnp.cos(dt * a_im), mag * jnp.sin(dt * a_im)
    nr, ni = ab_re - 1.0, ab_im
    den = a_re * a_re + a_im * a_im
    cf_re = (nr * a_re + ni * a_im) / den
    cf_im = (ni * a_re - nr * a_im) / den
    bb_re = cf_re[..., None] * b_re - cf_im[..., None] * b_im
    bb_im = cf_re[..., None] * b_im + cf_im[..., None] * b_re

    k = jnp.arange(tc + 1, dtype=F32)[:, None, None, None]
    pmag = jnp.exp(k * (dt * a_re)[None])
    pang = k * (dt * a_im)[None]
    pw_re, pw_im = pmag * jnp.cos(pang), pmag * jnp.sin(pang)

    ab_k_re = pw_re[..., None] * bb_re[None] - pw_im[..., None] * bb_im[None]
    ab_k_im = pw_re[..., None] * bb_im[None] + pw_im[..., None] * bb_re[None]
    ca_k_re = c_re[None] * pw_re[:, :, :, None, :] - c_im[None] * pw_im[:, :, :, None, :]
    ca_k_im = c_re[None] * pw_im[:, :, :, None, :] + c_im[None] * pw_re[:, :, :, None, :]

    taps = (jnp.einsum('dghp,kdgpj->kdghj', c_re, ab_k_re[:tc], precision=HIGHEST)
            - jnp.einsum('dghp,kdgpj->kdghj', c_im, ab_k_im[:tc], precision=HIGHEST))
    s_idx = jnp.arange(tc)[:, None]
    t_idx = jnp.arange(tc)[None, :]
    hc = b_re.shape[-1]
    g = a_re.shape[1]
    p = a_re.shape[2]

    def toeplitz(tp, lag, valid):
        m = jnp.where(valid[:, :, None, None, None], tp[jnp.clip(lag, 0, tc - 1)], 0.0)
        return m.transpose(2, 0, 4, 1, 3).reshape(g, tc * hc, tc * hc)

    m_f = toeplitz(taps[:, 0], t_idx - s_idx, t_idx >= s_idx)
    m_b = toeplitz(taps[:, 1], s_idx - t_idx, s_idx >= t_idx)
    m = jnp.stack([m_f, m_b]).astype(BF16)

    def pin_of(re, im):
        both = jnp.concatenate([re, im], axis=2)
        return both.transpose(1, 0, 3, 2).reshape(g, tc * hc, 2 * p)

    pin = jnp.stack([pin_of(ab_k_re[:tc, 0][::-1], ab_k_im[:tc, 0][::-1]),
                     pin_of(ab_k_re[:tc, 1], ab_k_im[:tc, 1])]).astype(BF16)

    def pout_of(re, im):
        both = jnp.concatenate([re, -im], axis=3)
        return both.transpose(1, 3, 0, 2).reshape(g, 2 * p, tc * hc)

    pout = jnp.stack([pout_of(ca_k_re[1:tc + 1, 0], ca_k_im[1:tc + 1, 0]),
                      pout_of(ca_k_re[1:tc + 1, 1][::-1], ca_k_im[1:tc + 1, 1][::-1])]).astype(BF16)

    at_re, at_im = pw_re[tc], pw_im[tc]
    ar = jnp.concatenate([at_re, at_re], axis=-1)[:, :, None, :]
    ai = jnp.concatenate([-at_im, at_im], axis=-1)[:, :, None, :]
    return m, pin, pout, ar, ai


def _s5_mix(u_ctx, u_lat, s5_d, tables):
    m, pin, pout, ar, ai = tables
    b, lc, width = u_ctx.shape
    l = u_lat.shape[1]
    g = m.shape[1]
    hc = width // g
    tc = S5_CHUNK
    w = tc * hc
    ncc, nlc = lc // tc, l // tc
    nch = ncc + nlc
    u = jnp.concatenate([u_ctx, u_lat], axis=1)
    xg = u.reshape(b, nch, tc, g, hc).transpose(3, 1, 0, 2, 4).reshape(g, nch * b, w)
    dg = jnp.tile(s5_d.reshape(g, 1, hc), (1, tc, 1)).reshape(g, 1, w)
    pgrp = lambda a: pl.BlockSpec((2, 1) + a.shape[2:], lambda i: (0, i, 0, 0))
    y = pl.pallas_call(
        functools.partial(_s5_kernel, nb=b, n_chunks=nch, n_ctx_chunks=ncc),
        grid=(g,),
        in_specs=[pl.BlockSpec((1, nch * b, w), lambda i: (i, 0, 0)),
                  pl.BlockSpec((1, 1, w), lambda i: (i, 0, 0)),
                  pgrp(m), pgrp(pin), pgrp(pout), pgrp(ar), pgrp(ai)],
        out_specs=pl.BlockSpec((1, nlc * b, w), lambda i: (i, 0, 0)),
        out_shape=_sds((g, nlc * b, w)),
        scratch_shapes=[pltpu.VMEM((nch * b, pin.shape[-1]), F32)],
        compiler_params=_cparams(("parallel",), 56),
        name="s5_mix",
    )(xg, dg, m, pin, pout, ar, ai)
    return y.reshape(g, nlc, b, tc, hc).transpose(2, 1, 3, 0, 4).reshape(b, l, width)


def _gelu(x):
    return 0.5 * x * (1.0 + lax.erf(x * (1.0 / math.sqrt(2.0))))


def _mix_out_kernel(s5_ref, rw_ref, bon_ref, g_ref, x_ref, g1_ref, sh2_ref, sc2_ref, n2_ref,
                    wglu_ref, bglu_ref, lnw_ref, lnb_ref, wos_ref, wor_ref, h1_ref, hn_ref, *, head):
    y1 = _gelu(s5_ref[0])
    s5o = y1 * jax.nn.sigmoid(_dot(y1.astype(BF16), wglu_ref[...]) + bglu_ref[...])
    y = rw_ref[0]
    seg_m = _seg_matrix(y.shape[-1], head)
    inv = 1.0 / head
    mu = _seg_sum(y, seg_m) * inv
    yc = y - mu
    var = _seg_sum(yc * yc, seg_m) * inv
    yn = yc * lax.rsqrt(var + GN_EPS) * lnw_ref[...] + lnb_ref[...]
    rwo = (yn + bon_ref[0]) * g_ref[0]
    mix = _dot(s5o.astype(BF16), wos_ref[...]) + _dot(rwo.astype(BF16), wor_ref[...])
    h1 = x_ref[0] + g1_ref[0] * mix
    h1_ref[0] = h1
    ms = jnp.mean(h1 * h1, axis=-1, keepdims=True)
    hn_ref[0] = (h1 * lax.rsqrt(ms + NORM_EPS) * n2_ref[...]) * (1.0 + sc2_ref[0]) + sh2_ref[0]


def _mix_out(s5y, rwy, bonus, gate, x, g1, sh2, sc2, n2g, w_glu, b_glu, ln_w, ln_b, w_out_s, w_out_r, head):
    b, l, d = x.shape
    width = s5y.shape[-1]
    tm = min(512, l)
    tok = lambda c: pl.BlockSpec((1, tm, c), lambda i, j: (i, j, 0))
    mod = pl.BlockSpec((1, 1, d), lambda i, j: (i, 0, 0))
    full = lambda a: pl.BlockSpec(a.shape, lambda i, j: (0,) * a.ndim)
    consts = (n2g, w_glu, b_glu, ln_w, ln_b, w_out_s, w_out_r)
    return pl.pallas_call(
        functools.partial(_mix_out_kernel, head=head),
        grid=(b, l // tm),
        in_specs=[tok(width)] * 4 + [tok(d), mod, mod, mod] + [full(a) for a in consts],
        out_specs=[tok(d), tok(d)],
        out_shape=[_sds((b, l, d)), _sds((b, l, d))],
        compiler_params=_cparams(("parallel", "parallel")),
        name="mix_out",
    )(s5y, rwy, bonus, gate, x, g1, sh2, sc2, *consts)


def _staircase(k):
    return [(a, b) for a in range(k) for b in range(k) if (a + 1) * (b + 1) <= k]


def _topk_rows(s, k):
    rows = s.shape[0]
    iota = lax.broadcasted_iota(jnp.int32, s.shape, 0)
    vals, idxs = [], []
    for _ in range(k):
        m = jnp.max(s, axis=0, keepdims=True)
        idx = jnp.min(jnp.where(s == m, iota, rows), axis=0, keepdims=True)
        vals.append(m)
        idxs.append(idx)
        s = jnp.where(iota == idx, -jnp.inf, s)
    return jnp.concatenate(vals, axis=0), jnp.concatenate(idxs, axis=0).astype(F32)


def _peer_topk_kernel(hn_ref, wq_ref, keys_ref, sel1_ref, sel2_ref, pad_ref, off_ref, gate_ref, *, heads, nkeys, half,
                      rows_per_expert):
    hb = hn_ref[...].astype(BF16)
    nt = (((1,), (1,)), ((), ()))
    sel1 = sel1_ref[...]
    sel2 = sel2_ref[...]
    pick = lambda sel, a: jnp.dot(sel, a, precision=HIGHEST, preferred_element_type=F32)
    ids, gates = [], []
    for h in range(heads):
        top = []
        for c in range(2):
            row0 = (h * 2 + c) * half
            qt = lax.dot_general(wq_ref[row0:row0 + half, :], hb, nt, preferred_element_type=F32)
            st = _dot(keys_ref[h * 2 + c], qt.astype(BF16))
            top.append(_topk_rows(st, PEER_TOPK))
        (s1, i1), (s2, i2) = top
        cand = pick(sel1, s1) + pick(sel2, s2) + pad_ref[...]
        cid = pick(sel1, i1) * nkeys + pick(sel2, i2)
        best, pos = _topk_rows(cand, PEER_TOPK)
        riota = lax.broadcasted_iota(jnp.int32, cid.shape, 0).astype(F32)
        eid = jnp.concatenate(
            [jnp.sum(jnp.where(riota == pos[r:r + 1], cid, 0.0), axis=0, keepdims=True) for r in range(PEER_TOPK)], axis=0)
        e = jnp.exp(best - jnp.max(best, axis=0, keepdims=True))
        gates.append(e / jnp.sum(e, axis=0, keepdims=True))
        ids.append(eid)
    eid = jnp.concatenate(ids, axis=0).T.astype(jnp.int32)
    off_ref[...] = eid * rows_per_expert
    gate_ref[...] = jnp.concatenate(gates, axis=0).T


def _peer_topk(hn, wq_t, keys, rows_per_expert):
    n, d = hn.shape
    heads = keys.shape[0] // 2
    nkeys, half = keys.shape[1], keys.shape[2]
    tm = min(256, n)
    pairs = _staircase(PEER_TOPK)
    rows = -(-len(pairs) // SUBLANES) * SUBLANES
    sel1 = np.zeros((rows, PEER_TOPK), np.float32)
    sel2 = np.zeros((rows, PEER_TOPK), np.float32)
    pad = np.zeros((rows, 1), np.float32)
    for i, (a, b) in enumerate(pairs):
        sel1[i, a] = 1.0
        sel2[i, b] = 1.0
    pad[len(pairs):] = -np.inf
    slots = heads * PEER_TOPK
    full = lambda a: pl.BlockSpec(a.shape, lambda i: (0,) * a.ndim)
    consts = (wq_t, keys, jnp.asarray(sel1), jnp.asarray(sel2), jnp.asarray(pad))
    return pl.pallas_call(
        functools.partial(_peer_topk_kernel, heads=heads, nkeys=nkeys, half=half, rows_per_expert=rows_per_expert),
        grid=(n // tm,),
        in_specs=[pl.BlockSpec((tm, d), lambda i: (i, 0))] + [full(a) for a in consts],
        out_specs=[pl.BlockSpec((tm, slots), lambda i: (i, 0))] * 2,
        out_shape=[_sds((n, slots), jnp.int32), _sds((n, slots))],
        compiler_params=_cparams(("parallel",)),
        name="peer_topk",
    )(hn, *consts)


def _pack_table(tab):
    e, d = tab.shape
    half = d // 2
    bits = lax.bitcast_convert_type(tab.astype(BF16), jnp.uint16).astype(jnp.uint32)
    packed = ((bits[:, :half] << 16) | bits[:, half:]).reshape(e * (half // LANES), LANES)
    return jnp.pad(packed, ((0, SUBLANES), (0, 0)))


def _expert_halves(tab_ref, off):
    bits = tab_ref[pl.ds(off, SUBLANES), :]
    return pltpu.bitcast(bits & jnp.uint32(0xFFFF0000), F32), pltpu.bitcast(bits << 16, F32)


def _peer_u_kernel(off_ref, h_ref, gate_ref, tab_ref, w_ref, q_ref, r_ref, *, tq, slots):
    half_rows = SUBLANES // 2
    sub = lax.broadcasted_iota(jnp.int32, (SUBLANES, LANES), 0)
    low = sub < half_rows
    groups = slots // SUBLANES
    qrows = slots * half_rows

    def token(t, carry):
        h = h_ref[t]
        h_first = jnp.where(low, h, 0.0)
        h_second = jnp.where(low, pltpu.roll(h, half_rows, 0), 0.0)

        def group(g8, c2):
            base = t * slots + g8 * SUBLANES
            ps = []
            for r in range(SUBLANES):
                hi, lo = _expert_halves(tab_ref, off_ref[0, 0, base + r])
                ps.append(hi * h_first + lo * h_second)
            pairs = [ps[i] + pltpu.roll(ps[i + 1], half_rows, 0) for i in range(0, SUBLANES, 2)]
            row0 = pl.multiple_of(base * half_rows, SUBLANES * half_rows)
            q_ref[pl.ds(row0, SUBLANES * half_rows), :] = jnp.concatenate(pairs, axis=0).astype(BF16)
            return c2

        return lax.fori_loop(0, groups, group, carry, unroll=4)

    lax.fori_loop(0, tq, token, 0)

    ones = jnp.ones((SUBLANES, LANES), BF16)
    nt = (((1,), (1,)), ((), ()))
    for t in range(tq):
        lane_sums = lax.dot_general(ones, q_ref[t * qrows:(t + 1) * qrows, :], nt, preferred_element_type=F32)
        r_ref[t:t + 1, :] = lane_sums[0:1]
    fold = (lax.broadcasted_iota(jnp.int32, (qrows, slots), 0) // half_rows
            == lax.broadcasted_iota(jnp.int32, (qrows, slots), 1))
    fold = jnp.where(fold, 1.0, 0.0).astype(BF16)
    rs = r_ref[...]
    rh = rs.astype(BF16)
    rl = (rs - rh.astype(F32)).astype(BF16)
    act = _dot(rh, fold) + _dot(rl, fold)
    w_ref[...] = gate_ref[...] * _gelu(act)


def _smem_block(tq, slots):
    return pl.BlockSpec((1, 1, tq * slots), lambda i: (i, 0, 0), memory_space=pltpu.SMEM)


def _peer_u(off_blocks, hn3, gate, tab):
    n, rows, lanes = hn3.shape
    slots = gate.shape[1]
    tq = off_blocks.shape[-1] // slots
    return pl.pallas_call(
        functools.partial(_peer_u_kernel, tq=tq, slots=slots),
        grid=(n // tq,),
        in_specs=[_smem_block(tq, slots),
                  pl.BlockSpec((tq, rows, lanes), lambda i: (i, 0, 0)),
                  pl.BlockSpec((tq, slots), lambda i: (i, 0)),
                  pl.BlockSpec(tab.shape, lambda i: (0, 0), pipeline_mode=pl.Buffered(1))],
        out_specs=pl.BlockSpec((tq, slots), lambda i: (i, 0)),
        out_shape=_sds((n, slots)),
        scratch_shapes=[pltpu.VMEM((tq * slots * (SUBLANES // 2), lanes), BF16),
                        pltpu.VMEM((tq, slots * (SUBLANES // 2)), F32)],
        compiler_params=_cparams(("parallel",), 52),
        name="peer_u",
    )(off_blocks, hn3, gate, tab)


def _peer_v_kernel(off_ref, w_ref, tab_ref, o_ref, ws_ref, *, tq, slots):
    sub = lax.broadcasted_iota(jnp.int32, (SUBLANES, LANES), 0)
    low = sub < SUBLANES // 2
    groups = slots // SUBLANES

    eye = lax.broadcasted_iota(jnp.int32, (slots, slots), 0) == lax.broadcasted_iota(jnp.int32, (slots, slots), 1)
    ones = jnp.ones((slots, LANES), BF16)

    def splat(t, carry):
        wrow = jnp.broadcast_to(w_ref[pl.ds(t, 1), :], (slots, slots))
        ws_ref[pl.ds(pl.multiple_of(t * slots, slots), slots), :] = _dot(jnp.where(eye, wrow, 0.0).astype(BF16), ones)
        return carry

    lax.fori_loop(0, tq, splat, 0, unroll=4)

    def token(t, carry):
        def group(g8, accs):
            first, second = list(accs[:2]), list(accs[2:])
            base = t * slots + g8 * SUBLANES
            for r in range(SUBLANES):
                hi, lo = _expert_halves(tab_ref, off_ref[0, 0, base + r])
                w = jnp.broadcast_to(ws_ref[pl.ds(base + r, 1), :], (SUBLANES, LANES))
                first[r % 2] = first[r % 2] + hi * w
                second[r % 2] = second[r % 2] + lo * w
            return tuple(first + second)

        zero = jnp.zeros((SUBLANES, LANES), F32)
        f0, f1, s0, s1 = lax.fori_loop(0, groups, group, (zero,) * 4, unroll=4)
        o_ref[t] = jnp.where(low, f0 + f1, pltpu.roll(s0 + s1, SUBLANES // 2, 0))
        return carry

    lax.fori_loop(0, tq, token, 0)


def _peer_v(off_blocks, w, tab):
    n, slots = w.shape
    tq = off_blocks.shape[-1] // slots
    lanes = tab.shape[-1]
    assert slots == lanes, "the weight splat uses one (slots, lanes) identity tile"
    return pl.pallas_call(
        functools.partial(_peer_v_kernel, tq=tq, slots=slots),
        grid=(n // tq,),
        in_specs=[_smem_block(tq, slots),
                  pl.BlockSpec((tq, slots), lambda i: (i, 0)),
                  pl.BlockSpec(tab.shape, lambda i: (0, 0), pipeline_mode=pl.Buffered(1))],
        out_specs=pl.BlockSpec((tq, SUBLANES, lanes), lambda i: (i, 0, 0)),
        out_shape=_sds((n, SUBLANES, lanes)),
        scratch_shapes=[pltpu.VMEM((tq * slots, lanes), F32)],
        compiler_params=_cparams(("parallel",), 52),
        name="peer_v",
    )(off_blocks, w, tab)


def _final_kernel(h1_ref, p_ref, g2_ref, nf_ref, o_ref):
    h = h1_ref[0] + g2_ref[0] * p_ref[0]
    ms = jnp.mean(h * h, axis=-1, keepdims=True)
    o_ref[0] = h * lax.rsqrt(ms + NORM_EPS) * nf_ref[...]


def _final_norm(h1, peer, g2, nfg):
    b, l, d = h1.shape
    tm = min(512, l)
    tok = pl.BlockSpec((1, tm, d), lambda i, j: (i, j, 0))
    return pl.pallas_call(
        _final_kernel,
        grid=(b, l // tm),
        in_specs=[tok, tok, pl.BlockSpec((1, 1, d), lambda i, j: (i, 0, 0)), pl.BlockSpec((1, d), lambda i, j: (0, 0))],
        out_specs=tok,
        out_shape=_sds((b, l, d)),
        compiler_params=_cparams(("parallel", "parallel")),
        name="final_norm",
    )(h1, peer, g2, nfg)


def _pad_cols(a, n):
    return jnp.pad(a, [(0, 0)] * (a.ndim - 1) + [(0, n - a.shape[-1])])


def _to_scan(a, heads):
    *lead, b, t, w = a.shape
    n = w // heads
    a = a.reshape(*lead, b, t, heads, n)
    nl = len(lead)
    perm = tuple(range(nl)) + (nl + 1, nl + 3, nl + 2, nl)
    return a.transpose(perm).reshape(*lead, t, n, heads * b)


def _from_scan(y, heads):
    t, n, c = y.shape
    b = c // heads
    return y.reshape(t, n, heads, b).transpose(3, 0, 2, 1).reshape(b, t, heads * n)


def _layer(h, hc, c, c_ctx, p, norm_f_g):
    b, l, d = h.shape
    lc = hc.shape[1]
    s5w = p['s5_d'].shape[-1]
    rww = p['rw_w0'].shape[-1]
    heads, head = p['rw_r_k'].shape
    n_dec, n_aaa, n_gate = p['rw_w_w2'].shape[1], p['rw_w_a2'].shape[1], p['rw_w_g2'].shape[0]
    n_lora = n_dec + n_aaa + n_gate
    lora_pad = -(-n_lora // LANES) * LANES

    rows = -(-(b + 1) // SUBLANES) * SUBLANES
    cc = jnp.zeros((rows, d), F32).at[:b].set(c).at[b].set(c_ctx)
    mods = _ada_mods(cc, p['w_ada'], p['b_ada'])
    sh1, sc1, g1, sh2, sc2, g2 = [m.reshape(rows, 1, d) for m in jnp.split(mods, 6, axis=-1)]
    lat = lambda m: m[:b]
    ctxm = lambda m: m[b:b + 1]

    w_in = p['w_in']
    w_s5 = w_in[:, :s5w].astype(BF16)
    w_main = w_in[:, s5w:s5w + 3 * rww].astype(BF16)
    w_lora = _pad_cols(w_in[:, s5w + 3 * rww:], lora_pad).astype(BF16)
    n1g = p['norm1_g'].reshape(1, d)
    u_l, zm_l, zl_l = _in_proj(h, lat(sh1), lat(sc1), n1g, w_s5, w_main, w_lora)
    u_c, zm_c, zl_c = _in_proj(hc, ctxm(sh1), ctxm(sc1), n1g, w_s5, w_main, w_lora)

    mu = p['rw_mu']
    mu_main = mu[:3 * rww].reshape(1, -1)
    mu_lora = _pad_cols(mu[3 * rww:], lora_pad).reshape(1, -1)
    place = lambda w, r0: jnp.zeros(w.shape[:-2] + (lora_pad, rww), F32).at[..., r0:r0 + w.shape[-2], :].set(w).astype(BF16)
    rw_params = (place(p['rw_w_g2'], n_dec + n_aaa), place(p['rw_w_w2'], 0), place(p['rw_w_a2'], n_dec),
                 p['rw_w0'].reshape(2, 1, rww), p['rw_a0'].reshape(2, 1, rww),
                 p['rw_k_k'].reshape(1, rww), p['rw_k_a'].reshape(1, rww), p['rw_r_k'].reshape(1, rww))
    r_l, v_l, kk_l, dec_l, kd_l, bb_l, gate_l, bonus_l = _rwkv_prep(zm_l, zl_l, (mu_main, mu_lora) + rw_params, head, True)
    r_c, v_c, kk_c, dec_c, kd_c, bb_c = _rwkv_prep(zm_c, zl_c, (mu_main, mu_lora) + rw_params, head, False)
    cat = lambda a_c, a_l: _to_scan(jnp.concatenate([a_c, a_l], axis=-2), heads)
    ys = _rwkv_scan((cat(r_c, r_l), cat(v_c, v_l), cat(kk_c, kk_l)),
                    (cat(dec_c, dec_l), cat(kd_c, kd_l), cat(bb_c, bb_l)), lc, head)
    rw_y = _from_scan((ys[0] + ys[1])[lc:], heads)

    tables = _s5_tables(p['s5_a_re'], p['s5_a_im'], p['s5_log_dt'], p['s5_b_re'], p['s5_b_im'],
                        p['s5_c_re'], p['s5_c_im'], S5_CHUNK)
    s5_y = _s5_mix(u_c, u_l, p['s5_d'], tables)

    w_out = p['w_out']
    h1, hn2 = _mix_out(s5_y, rw_y, bonus_l, gate_l, h, lat(g1), lat(sh2), lat(sc2), p['norm2_g'].reshape(1, d),
                       p['s5_w_glu'].astype(BF16), p['s5_b_glu'].reshape(1, s5w),
                       p['rw_ln_w'].reshape(1, rww), p['rw_ln_b'].reshape(1, rww),
                       w_out[:s5w].astype(BF16), w_out[s5w:].astype(BF16), head)

    n = b * l
    keys = p['peer_keys']
    pheads, _, nkeys, half = keys.shape
    assert d == VREG_WORDS, "peer_u / peer_v hold one token's features in a single (8, 128) tile"
    off, gate = _peer_topk(hn2.reshape(n, d), p['peer_w_q'].T.astype(BF16),
                           keys.reshape(pheads * 2, nkeys, half).astype(BF16), d // (2 * LANES))
    slots = gate.shape[1]
    tq = min(PEER_TILE, n)
    blocks = lambda a: a.reshape(n // tq, 1, tq * slots)
    w = _peer_u(blocks(off), hn2.reshape(n, SUBLANES, LANES), gate, _pack_table(p['peer_u']))
    peer = _peer_v(blocks(off), w, _pack_table(p['peer_v']))

    return _final_norm(h1, peer.reshape(b, l, d), lat(g2), norm_f_g.reshape(1, d))


def kernel(x, c, ctx, c_ctx, w_ada, b_ada, norm1_g, norm2_g, w_in, s5_a_re, s5_a_im, s5_log_dt, s5_b_re, s5_b_im, s5_c_re, s5_c_im, s5_d, s5_w_glu, s5_b_glu, rw_mu, rw_w0, rw_w_w2, rw_a0, rw_w_a2, rw_w_g2, rw_k_k, rw_k_a, rw_r_k, rw_ln_w, rw_ln_b, w_out, peer_w_q, peer_keys, peer_u, peer_v, norm_f_g):
    assert w_ada.shape[0] == 1, "single-layer block: the context stream is never updated"
    p = dict(w_ada=w_ada[0], b_ada=b_ada[0], norm1_g=norm1_g[0], norm2_g=norm2_g[0], w_in=w_in[0],
             s5_a_re=s5_a_re[0], s5_a_im=s5_a_im[0], s5_log_dt=s5_log_dt[0], s5_b_re=s5_b_re[0], s5_b_im=s5_b_im[0],
             s5_c_re=s5_c_re[0], s5_c_im=s5_c_im[0], s5_d=s5_d[0], s5_w_glu=s5_w_glu[0], s5_b_glu=s5_b_glu[0],
             rw_mu=rw_mu[0], rw_w0=rw_w0[0], rw_w_w2=rw_w_w2[0], rw_a0=rw_a0[0], rw_w_a2=rw_w_a2[0],
             rw_w_g2=rw_w_g2[0], rw_k_k=rw_k_k[0], rw_k_a=rw_k_a[0], rw_r_k=rw_r_k[0], rw_ln_w=rw_ln_w[0],
             rw_ln_b=rw_ln_b[0], w_out=w_out[0], peer_w_q=peer_w_q[0], peer_keys=peer_keys[0],
             peer_u=peer_u[0], peer_v=peer_v[0])
    return _layer(x, ctx, c, c_ctx, p, norm_f_g)
```

```python
import functools
import math

import jax
import jax.numpy as jnp
import numpy as np
from jax import lax
from jax.experimental import pallas as pl
from jax.experimental.pallas import tpu as pltpu

F32 = jnp.float32
BF16 = jnp.bfloat16
HIGHEST = lax.Precision.HIGHEST

SUBLANES = 8
LANES = 128
VREG_WORDS = SUBLANES * LANES

GRID_W = 64
NORM_EPS = 1e-6
GN_EPS = 64e-5
KK_EPS = 1e-12
PEER_TOPK = 16

S5_CHUNK = 64
SCAN_TILE = 64
PEER_TILE = 64


def _dot(a, b):
    return jnp.dot(a, b, preferred_element_type=F32)


def _cparams(sem, vmem_mb=48):
    return pltpu.CompilerParams(dimension_semantics=sem, vmem_limit_bytes=vmem_mb * 1024 * 1024)


def _sds(shape, dtype=F32):
    return jax.ShapeDtypeStruct(shape, dtype)


def _mods_kernel(c_ref, w_ref, b_ref, o_ref):
    c = c_ref[...]
    s = c * jax.nn.sigmoid(c)
    o_ref[...] = jnp.dot(s, w_ref[...], precision=HIGHEST, preferred_element_type=F32) + b_ref[...]


def _ada_mods(cc, w_ada, b_ada):
    rows, d = cc.shape
    n = w_ada.shape[1]
    bn = d if n % d == 0 else n
    return pl.pallas_call(
        _mods_kernel,
        grid=(n // bn,),
        in_specs=[pl.BlockSpec((rows, d), lambda j: (0, 0)),
                  pl.BlockSpec((d, bn), lambda j: (0, j)),
                  pl.BlockSpec((1, bn), lambda j: (0, j))],
        out_specs=pl.BlockSpec((rows, bn), lambda j: (0, j)),
        out_shape=_sds((rows, n)),
        compiler_params=_cparams(("arbitrary",)),
        name="ada_mods",
    )(cc, w_ada, b_ada.reshape(1, n))


def _inproj_kernel(x_ref, sh_ref, sc_ref, g_ref, ws_ref, wz_ref, wl_ref, u_ref, z_ref, l_ref):
    x = x_ref[0]
    ms = jnp.mean(x * x, axis=-1, keepdims=True)
    hn = (x * lax.rsqrt(ms + NORM_EPS) * g_ref[...]) * (1.0 + sc_ref[0]) + sh_ref[0]
    hb = hn.astype(BF16)
    u_ref[0] = _dot(hb, ws_ref[...])
    z_ref[0] = _dot(hb, wz_ref[...])
    l_ref[0] = _dot(hb, wl_ref[...])


def _in_proj(x, shift, scale, gain, w_s5, w_main, w_lora):
    b, l, d = x.shape
    tm = min(512, l)
    bm = shift.shape[0]
    mod_map = (lambda i, j: (i, 0, 0)) if bm > 1 else (lambda i, j: (0, 0, 0))
    full = lambda a: pl.BlockSpec(a.shape, lambda i, j: (0,) * a.ndim)
    return pl.pallas_call(
        _inproj_kernel,
        grid=(b, l // tm),
        in_specs=[pl.BlockSpec((1, tm, d), lambda i, j: (i, j, 0)),
                  pl.BlockSpec((1, 1, d), mod_map), pl.BlockSpec((1, 1, d), mod_map),
                  full(gain), full(w_s5), full(w_main), full(w_lora)],
        out_specs=[pl.BlockSpec((1, tm, w.shape[1]), lambda i, j: (i, j, 0)) for w in (w_s5, w_main, w_lora)],
        out_shape=[_sds((b, l, w.shape[1])) for w in (w_s5, w_main, w_lora)],
        compiler_params=_cparams(("parallel", "parallel")),
        name="in_proj",
    )(x, shift, scale, gain, w_s5, w_main, w_lora)


def _seg_matrix(width, seg):
    r = lax.broadcasted_iota(jnp.int32, (width, width), 0) // seg
    c = lax.broadcasted_iota(jnp.int32, (width, width), 1) // seg
    return jnp.where(r == c, 1.0, 0.0).astype(BF16)


def _seg_sum(x, seg_m):
    hi = x.astype(BF16)
    lo = (x - hi.astype(F32)).astype(BF16)
    return _dot(hi, seg_m) + _dot(lo, seg_m)


def _softplus(x):
    return jnp.maximum(x, 0.0) + jnp.log1p(jnp.exp(-jnp.abs(x)))


def _rwkv_streams(zz, zl, p, head, outs, with_gate):
    (wg_ref, ww_ref, wa_ref, w0_ref, a0_ref, kk_ref, ka_ref, rk_ref) = p
    width = kk_ref.shape[-1]
    r = zz[:, :width]
    k = zz[:, width:2 * width]
    v = zz[:, 2 * width:3 * width]
    seg_m = _seg_matrix(width, head)
    kk = k * kk_ref[...]
    kk = kk * lax.rsqrt(_seg_sum(kk * kk, seg_m) + KK_EPS)
    outs["r"][0] = r
    outs["v"][0] = v
    outs["kk"][0] = kk
    tl = jnp.tanh(zl).astype(BF16)
    zb = zl.astype(BF16)
    bonus = None
    for d in range(2):
        w = -_softplus(-(w0_ref[d] + _dot(tl, ww_ref[d]))) - 0.5
        outs["dec"][d, 0] = jnp.exp(-jnp.exp(w))
        a = jax.nn.sigmoid(a0_ref[d] + _dot(zb, wa_ref[d]))
        kd = k * (1.0 + (a - 1.0) * ka_ref[...])
        outs["kd"][d, 0] = kd
        outs["bb"][d, 0] = kk * a
        if with_gate:
            bd = _seg_sum(r * kd * rk_ref[...], seg_m) * v
            bonus = bd if bonus is None else bonus + bd
    if with_gate:
        outs["bonus"][0] = bonus
        outs["g"][0] = _dot(jax.nn.sigmoid(zl).astype(BF16), wg_ref[...])


def _lerp_shift(z, shifted, mu):
    return z + (shifted - z) * mu


def _grid_shift(zc, zp, zn, has_prev, has_next):
    t, c = zc.shape
    tok = lax.broadcasted_iota(jnp.int32, (t, 1), 0) % GRID_W
    col = lax.broadcasted_iota(jnp.int32, (1, c), 1) % 4
    left = jnp.where(tok == 0, 0.0, pltpu.roll(zc, 1, 0))
    right = jnp.where(tok == GRID_W - 1, 0.0, pltpu.roll(zc, t - 1, 0))
    zp = jnp.where(has_prev, zp, 0.0)
    zn = jnp.where(has_next, zn, 0.0)
    if t > GRID_W:
        up = jnp.concatenate([zp, zc[:t - GRID_W]], axis=0)
        down = jnp.concatenate([zc[GRID_W:], zn], axis=0)
    else:
        up, down = zp, zn
    return jnp.where(col == 0, left, jnp.where(col == 1, right, jnp.where(col == 2, up, down)))


def _seq_shift(zc):
    t, c = zc.shape
    tok = lax.broadcasted_iota(jnp.int32, (t, 1), 0)
    col = lax.broadcasted_iota(jnp.int32, (1, c), 1) % 2
    prev = jnp.where(tok == 0, 0.0, pltpu.roll(zc, 1, 0))
    nxt = jnp.where(tok == t - 1, 0.0, pltpu.roll(zc, t - 1, 0))
    return jnp.where(col == 0, prev, nxt)


_STREAMS = ("r", "v", "kk", "dec", "kd", "bb")


def _rwkv_prep_lat_kernel(zc_ref, zp_ref, zn_ref, lc_ref, lp_ref, ln_ref, mum_ref, mul_ref, *rest, head):
    p, o = rest[:8], rest[8:]
    outs = dict(zip(_STREAMS + ("g", "bonus"), o))
    j = pl.program_id(1)
    has_prev = j > 0
    has_next = j < pl.num_programs(1) - 1
    zc = zc_ref[0]
    lc = lc_ref[0]
    zz = _lerp_shift(zc, _grid_shift(zc, zp_ref[0], zn_ref[0], has_prev, has_next), mum_ref[...])
    zl = _lerp_shift(lc, _grid_shift(lc, lp_ref[0], ln_ref[0], has_prev, has_next), mul_ref[...])
    _rwkv_streams(zz, zl, p, head, outs, True)


def _rwkv_prep_ctx_kernel(zc_ref, lc_ref, mum_ref, mul_ref, *rest, head):
    p, o = rest[:8], rest[8:]
    outs = dict(zip(_STREAMS, o))
    zc = zc_ref[0]
    lc = lc_ref[0]
    zz = _lerp_shift(zc, _seq_shift(zc), mum_ref[...])
    zl = _lerp_shift(lc, _seq_shift(lc), mul_ref[...])
    _rwkv_streams(zz, zl, p, head, outs, False)


def _rwkv_prep(z_main, z_lora, params, head, grid_mode):
    b, l, cm = z_main.shape
    cl = z_lora.shape[-1]
    width = cm // 3
    full = lambda a: pl.BlockSpec(a.shape, lambda i, j: (0,) * a.ndim)
    if grid_mode:
        rows_per_tile = min(8, l // GRID_W)
        tt = rows_per_tile * GRID_W
        nt = l // tt
        last_row = l // GRID_W - 1
        cur = lambda c: pl.BlockSpec((1, tt, c), lambda i, j: (i, j, 0))
        prev = lambda c: pl.BlockSpec((1, GRID_W, c), lambda i, j: (i, jnp.maximum(j * rows_per_tile - 1, 0), 0))
        nxt = lambda c: pl.BlockSpec((1, GRID_W, c), lambda i, j: (i, jnp.minimum((j + 1) * rows_per_tile, last_row), 0))
        data = [z_main, z_main, z_main, z_lora, z_lora, z_lora]
        data_specs = [cur(cm), prev(cm), nxt(cm), cur(cl), prev(cl), nxt(cl)]
        kern = functools.partial(_rwkv_prep_lat_kernel, head=head)
        n_single = 5
    else:
        tt, nt = l, 1
        data = [z_main, z_lora]
        data_specs = [pl.BlockSpec((1, tt, cm), lambda i, j: (i, j, 0)), pl.BlockSpec((1, tt, cl), lambda i, j: (i, j, 0))]
        kern = functools.partial(_rwkv_prep_ctx_kernel, head=head)
        n_single = 3
    single = pl.BlockSpec((1, tt, width), lambda i, j: (i, j, 0))
    double = pl.BlockSpec((2, 1, tt, width), lambda i, j: (0, i, j, 0))
    out_specs = [single] * 3 + [double] * 3 + [single] * (n_single - 3)
    out_shape = [_sds((b, l, width))] * 3 + [_sds((2, b, l, width))] * 3 + [_sds((b, l, width))] * (n_single - 3)
    return pl.pallas_call(
        kern,
        grid=(b, nt),
        in_specs=data_specs + [full(a) for a in params],
        out_specs=out_specs,
        out_shape=out_shape,
        compiler_params=_cparams(("parallel", "parallel")),
        name="rwkv_prep_lat" if grid_mode else "rwkv_prep_ctx",
    )(*data, *params)


def _scan_kernel(r_ref, v_ref, kk_ref, w_ref, kd_ref, bb_ref, y_ref, s_ref, *, tt, head):
    d = pl.program_id(0)

    @pl.when(pl.program_id(2) == 0)
    def _():
        s_ref[...] = jnp.zeros_like(s_ref)

    def tok(i):
        return jnp.where(d == 0, i, tt - 1 - i)

    def row(ref, t, k):
        return jnp.broadcast_to(ref[t, pl.ds(k, 1), :], s_ref.shape[1:])

    t0 = tok(0)
    sa0 = jnp.zeros(s_ref.shape[1:], F32)
    for k in range(head):
        sa0 = sa0 - s_ref[k] * row(kk_ref, t0, k)

    def token(i, sa):
        t = tok(i)
        tn = tok(jnp.minimum(i + 1, tt - 1))
        vv = v_ref[t]
        y = jnp.zeros_like(sa)
        sa_next = jnp.zeros_like(sa)
        for k in range(head):
            s = s_ref[k] * row(w_ref, t, k) + sa * row(bb_ref, t, k) + vv * row(kd_ref, t, k)
            s_ref[k] = s
            y = y + s * row(r_ref, t, k)
            sa_next = sa_next - s * row(kk_ref, tn, k)
        y_ref[t] = y
        return sa_next

    lax.fori_loop(0, tt, token, sa0)


def _rwkv_scan(shared, directional, n_ctx_tok, head):
    ltot, _, chains = shared[0].shape
    tt = SCAN_TILE
    lb = min(chains, LANES)
    nlg = chains // lb
    nt = ltot // tt
    nc = n_ctx_tok // tt

    def tile(d, j):
        back = jnp.where(j < nc, nc - 1 - j, nt + nc - 1 - j)
        return jnp.where(d == 0, j, back)

    sh_spec = pl.BlockSpec((tt, head, lb), lambda d, g, j: (tile(d, j), 0, g))
    di_spec = pl.BlockSpec((None, tt, head, lb), lambda d, g, j: (d, tile(d, j), 0, g))
    return pl.pallas_call(
        functools.partial(_scan_kernel, tt=tt, head=head),
        grid=(2, nlg, nt),
        in_specs=[sh_spec] * 3 + [di_spec] * 3,
        out_specs=di_spec,
        out_shape=_sds((2, ltot, head, chains)),
        scratch_shapes=[pltpu.VMEM((head, head, lb), F32)],
        compiler_params=_cparams(("arbitrary", "arbitrary", "arbitrary")),
        name="rwkv_scan",
    )(*shared, *directional)


def _s5_kernel(x_ref, d_ref, m_ref, pin_ref, pout_ref, ar_ref, ai_ref, y_ref, hall_ref, *, nb, n_chunks, n_ctx_chunks):
    x = x_ref[0]
    xb = x.astype(BF16)
    lat0 = n_ctx_chunks * nb
    xl = xb[lat0:]
    y = d_ref[0] * x[lat0:]
    half = ar_ref.shape[-1] // 2
    fwd = list(range(n_chunks))
    bwd = list(range(n_ctx_chunks - 1, -1, -1)) + list(range(n_chunks - 1, n_ctx_chunks - 1, -1))
    for d, order in enumerate((fwd, bwd)):
        v = _dot(xb, pin_ref[d, 0])
        ar = ar_ref[d, 0]
        ai = ai_ref[d, 0]
        h = jnp.zeros((nb, 2 * half), F32)
        for c in order:
            hall_ref[c * nb:(c + 1) * nb, :] = h
            h = h * ar + pltpu.roll(h, half, 1) * ai + v[c * nb:(c + 1) * nb]
        y = y + _dot(xl, m_ref[d, 0]) + _dot(hall_ref[lat0:, :].astype(BF16), pout_ref[d, 0])
    y_ref[0] = y


def _s5_tables(a_re, a_im, log_dt, b_re, b_im, c_re, c_im, tc):
    a_re, a_im, b_re, b_im = (t.astype(F32) for t in (a_re, a_im, b_re, b_im))
    c_re, c_im = c_re.astype(F32), c_im.astype(F32)
    dt = jnp.exp(log_dt.astype(F32))[..., None]
    mag = jnp.exp(dt * a_re)
    ab_re, ab_im = mag * jnp.cos(dt * a_im), mag * jnp.sin(dt * a_im)
    nr, ni = ab_re - 1.0, ab_im
    den = a_re * a_re + a_im * a_im
    cf_re = (nr * a_re + ni * a_im) / den
    cf_im = (ni * a_re - nr * a_im) / den
    bb_re = cf_re[..., None] * b_re - cf_im[..., None] * b_im
    bb_im = cf_re[..., None] * b_im + cf_im[..., None] * b_re

    k = jnp.arange(tc + 1, dtype=F32)[:, None, None, None]
    pmag = jnp.exp(k * (dt * a_re)[None])
    pang = k * (dt * a_im)[None]
    pw_re, pw_im = pmag * jnp.cos(pang), pmag * jnp.sin(pang)

    ab_k_re = pw_re[..., None] * bb_re[None] - pw_im[..., None] * bb_im[None]
    ab_k_im = pw_re[..., None] * bb_im[None] + pw_im[..., None] * bb_re[None]
    ca_k_re = c_re[None] * pw_re[:, :, :, None, :] - c_im[None] * pw_im[:, :, :, None, :]
    ca_k_im = c_re[None] * pw_im[:, :, :, None, :] + c_im[None] * pw_re[:, :, :, None, :]

    taps = (jnp.einsum('dghp,kdgpj->kdghj', c_re, ab_k_re[:tc], precision=HIGHEST)
            - jnp.einsum('dghp,kdgpj->kdghj', c_im, ab_k_im[:tc], precision=HIGHEST))
    s_idx = jnp.arange(tc)[:, None]
    t_idx = jnp.arange(tc)[None, :]
    hc = b_re.shape[-1]
    g = a_re.shape[1]
    p = a_re.shape[2]

    def toeplitz(tp, lag, valid):
        m = jnp.where(valid[:, :, None, None, None], tp[jnp.clip(lag, 0, tc - 1)], 0.0)
        return m.transpose(2, 0, 4, 1, 3).reshape(g, tc * hc, tc * hc)

    m_f = toeplitz(taps[:, 0], t_idx - s_idx, t_idx >= s_idx)
    m_b = toeplitz(taps[:, 1], s_idx - t_idx, s_idx >= t_idx)
    m = jnp.stack([m_f, m_b]).astype(BF16)

    def pin_of(re, im):
        both = jnp.concatenate([re, im], axis=2)
        return both.transpose(1, 0, 3, 2).reshape(g, tc * hc, 2 * p)

    pin = jnp.stack([pin_of(ab_k_re[:tc, 0][::-1], ab_k_im[:tc, 0][::-1]),
                     pin_of(ab_k_re[:tc, 1], ab_k_im[:tc, 1])]).astype(BF16)

    def pout_of(re, im):
        both = jnp.concatenate([re, -im], axis=3)
        return both.transpose(1, 3, 0, 2).reshape(g, 2 * p, tc * hc)

    pout = jnp.stack([pout_of(ca_k_re[1:tc + 1, 0], ca_k_im[1:tc + 1, 0]),
                      pout_of(ca_k_re[1:tc + 1, 1][::-1], ca_k_im[1:tc + 1, 1][::-1])]).astype(BF16)

    at_re, at_im = pw_re[tc], pw_im[tc]
    ar = jnp.concatenate([at_re, at_re], axis=-1)[:, :, None, :]
    ai = jnp.concatenate([-at_im, at_im], axis=-1)[:, :, None, :]
    return m, pin, pout, ar, ai


def _s5_mix(u_ctx, u_lat, s5_d, tables):
    m, pin, pout, ar, ai = tables
    b, lc, width = u_ctx.shape
    l = u_lat.shape[1]
    g = m.shape[1]
    hc = width // g
    tc = S5_CHUNK
    w = tc * hc
    ncc, nlc = lc // tc, l // tc
    nch = ncc + nlc
    u = jnp.concatenate([u_ctx, u_lat], axis=1)
    xg = u.reshape(b, nch, tc, g, hc).transpose(3, 1, 0, 2, 4).reshape(g, nch * b, w)
    dg = jnp.tile(s5_d.reshape(g, 1, hc), (1, tc, 1)).reshape(g, 1, w)
    pgrp = lambda a: pl.BlockSpec((2, 1) + a.shape[2:], lambda i: (0, i, 0, 0))
    y = pl.pallas_call(
        functools.partial(_s5_kernel, nb=b, n_chunks=nch, n_ctx_chunks=ncc),
        grid=(g,),
        in_specs=[pl.BlockSpec((1, nch * b, w), lambda i: (i, 0, 0)),
                  pl.BlockSpec((1, 1, w), lambda i: (i, 0, 0)),
                  pgrp(m), pgrp(pin), pgrp(pout), pgrp(ar), pgrp(ai)],
        out_specs=pl.BlockSpec((1, nlc * b, w), lambda i: (i, 0, 0)),
        out_shape=_sds((g, nlc * b, w)),
        scratch_shapes=[pltpu.VMEM((nch * b, pin.shape[-1]), F32)],
        compiler_params=_cparams(("parallel",), 56),
        name="s5_mix",
    )(xg, dg, m, pin, pout, ar, ai)
    return y.reshape(g, nlc, b, tc, hc).transpose(2, 1, 3, 0, 4).reshape(b, l, width)


def _gelu(x):
    return 0.5 * x * (1.0 + lax.erf(x * (1.0 / math.sqrt(2.0))))


def _mix_out_kernel(s5_ref, rw_ref, bon_ref, g_ref, x_ref, g1_ref, sh2_ref, sc2_ref, n2_ref,
                    wglu_ref, bglu_ref, lnw_ref, lnb_ref, wos_ref, wor_ref, h1_ref, hn_ref, *, head):
    y1 = _gelu(s5_ref[0])
    s5o = y1 * jax.nn.sigmoid(_dot(y1.astype(BF16), wglu_ref[...]) + bglu_ref[...])
    y = rw_ref[0]
    seg_m = _seg_matrix(y.shape[-1], head)
    inv = 1.0 / head
    mu = _seg_sum(y, seg_m) * inv
    yc = y - mu
    var = _seg_sum(yc * yc, seg_m) * inv
    yn = yc * lax.rsqrt(var + GN_EPS) * lnw_ref[...] + lnb_ref[...]
    rwo = (yn + bon_ref[0]) * g_ref[0]
    mix = _dot(s5o.astype(BF16), wos_ref[...]) + _dot(rwo.astype(BF16), wor_ref[...])
    h1 = x_ref[0] + g1_ref[0] * mix
    h1_ref[0] = h1
    ms = jnp.mean(h1 * h1, axis=-1, keepdims=True)
    hn_ref[0] = (h1 * lax.rsqrt(ms + NORM_EPS) * n2_ref[...]) * (1.0 + sc2_ref[0]) + sh2_ref[0]


def _mix_out(s5y, rwy, bonus, gate, x, g1, sh2, sc2, n2g, w_glu, b_glu, ln_w, ln_b, w_out_s, w_out_r, head):
    b, l, d = x.shape
    width = s5y.shape[-1]
    tm = min(512, l)
    tok = lambda c: pl.BlockSpec((1, tm, c), lambda i, j: (i, j, 0))
    mod = pl.BlockSpec((1, 1, d), lambda i, j: (i, 0, 0))
    full = lambda a: pl.BlockSpec(a.shape, lambda i, j: (0,) * a.ndim)
    consts = (n2g, w_glu, b_glu, ln_w, ln_b, w_out_s, w_out_r)
    return pl.pallas_call(
        functools.partial(_mix_out_kernel, head=head),
        grid=(b, l // tm),
        in_specs=[tok(width)] * 4 + [tok(d), mod, mod, mod] + [full(a) for a in consts],
        out_specs=[tok(d), tok(d)],
        out_shape=[_sds((b, l, d)), _sds((b, l, d))],
        compiler_params=_cparams(("parallel", "parallel")),
        name="mix_out",
    )(s5y, rwy, bonus, gate, x, g1, sh2, sc2, *consts)


def _staircase(k):
    return [(a, b) for a in range(k) for b in range(k) if (a + 1) * (b + 1) <= k]


def _topk_rows(s, k):
    rows = s.shape[0]
    iota = lax.broadcasted_iota(jnp.int32, s.shape, 0)
    vals, idxs = [], []
    for _ in range(k):
        m = jnp.max(s, axis=0, keepdims=True)
        idx = jnp.min(jnp.where(s == m, iota, rows), axis=0, keepdims=True)
        vals.append(m)
        idxs.append(idx)
        s = jnp.where(iota == idx, -jnp.inf, s)
    return jnp.concatenate(vals, axis=0), jnp.concatenate(idxs, axis=0).astype(F32)


def _peer_topk_kernel(hn_ref, wq_ref, keys_ref, sel1_ref, sel2_ref, pad_ref, off_ref, gate_ref, *, heads, nkeys, half,
                      rows_per_expert):
    hb = hn_ref[...].astype(BF16)
    nt = (((1,), (1,)), ((), ()))
    sel1 = sel1_ref[...]
    sel2 = sel2_ref[...]
    pick = lambda sel, a: jnp.dot(sel, a, precision=HIGHEST, preferred_element_type=F32)
    ids, gates = [], []
    for h in range(heads):
        top = []
        for c in range(2):
            row0 = (h * 2 + c) * half
            qt = lax.dot_general(wq_ref[row0:row0 + half, :], hb, nt, preferred_element_type=F32)
            st = _dot(keys_ref[h * 2 + c], qt.astype(BF16))
            top.append(_topk_rows(st, PEER_TOPK))
        (s1, i1), (s2, i2) = top
        cand = pick(sel1, s1) + pick(sel2, s2) + pad_ref[...]
        cid = pick(sel1, i1) * nkeys + pick(sel2, i2)
        best, pos = _topk_rows(cand, PEER_TOPK)
        riota = lax.broadcasted_iota(jnp.int32, cid.shape, 0).astype(F32)
        eid = jnp.concatenate(
            [jnp.sum(jnp.where(riota == pos[r:r + 1], cid, 0.0), axis=0, keepdims=True) for r in range(PEER_TOPK)], axis=0)
        e = jnp.exp(best - jnp.max(best, axis=0, keepdims=True))
        gates.append(e / jnp.sum(e, axis=0, keepdims=True))
        ids.append(eid)
    eid = jnp.concatenate(ids, axis=0).T.astype(jnp.int32)
    off_ref[...] = eid * rows_per_expert
    gate_ref[...] = jnp.concatenate(gates, axis=0).T


def _peer_topk(hn, wq_t, keys, rows_per_expert):
    n, d = hn.shape
    heads = keys.shape[0] // 2
    nkeys, half = keys.shape[1], keys.shape[2]
    tm = min(256, n)
    pairs = _staircase(PEER_TOPK)
    rows = -(-len(pairs) // SUBLANES) * SUBLANES
    sel1 = np.zeros((rows, PEER_TOPK), np.float32)
    sel2 = np.zeros((rows, PEER_TOPK), np.float32)
    pad = np.zeros((rows, 1), np.float32)
    for i, (a, b) in enumerate(pairs):
        sel1[i, a] = 1.0
        sel2[i, b] = 1.0
    pad[len(pairs):] = -np.inf
    slots = heads * PEER_TOPK
    full = lambda a: pl.BlockSpec(a.shape, lambda i: (0,) * a.ndim)
    consts = (wq_t, keys, jnp.asarray(sel1), jnp.asarray(sel2), jnp.asarray(pad))
    return pl.pallas_call(
        functools.partial(_peer_topk_kernel, heads=heads, nkeys=nkeys, half=half, rows_per_expert=rows_per_expert),
        grid=(n // tm,),
        in_specs=[pl.BlockSpec((tm, d), lambda i: (i, 0))] + [full(a) for a in consts],
        out_specs=[pl.BlockSpec((tm, slots), lambda i: (i, 0))] * 2,
        out_shape=[_sds((n, slots), jnp.int32), _sds((n, slots))],
        compiler_params=_cparams(("parallel",)),
        name="peer_topk",
    )(hn, *consts)


def _pack_table(tab):
    e, d = tab.shape
    half = d // 2
    bits = lax.bitcast_convert_type(tab.astype(BF16), jnp.uint16).astype(jnp.uint32)
    packed = ((bits[:, :half] << 16) | bits[:, half:]).reshape(e * (half // LANES), LANES)
    return jnp.pad(packed, ((0, SUBLANES), (0, 0)))


def _expert_halves(tab_ref, off):
    bits = tab_ref[pl.ds(off, SUBLANES), :]
    return pltpu.bitcast(bits & jnp.uint32(0xFFFF0000), F32), pltpu.bitcast(bits << 16, F32)


def _peer_u_kernel(off_ref, h_ref, gate_ref, tab_ref, w_ref, q_ref, r_ref, *, tq, slots):
    half_rows = SUBLANES // 2
    sub = lax.broadcasted_iota(jnp.int32, (SUBLANES, LANES), 0)
    low = sub < half_rows
    groups = slots // SUBLANES
    qrows = slots * half_rows

    ones = jnp.ones((SUBLANES, LANES), BF16)
    nt = (((1,), (1,)), ((), ()))

    def lane_sums(t):
        start = t * qrows if isinstance(t, int) else pl.multiple_of(t * qrows, qrows)
        q = q_ref[pl.ds(start, qrows), :]
        r_ref[pl.ds(t, 1), :] = lax.dot_general(ones, q, nt, preferred_element_type=F32)[0:1]

    q_ref[(tq - 1) * qrows:, :] = jnp.zeros((qrows, LANES), BF16)

    def token(t, carry):
        lane_sums(jnp.where(t == 0, tq - 1, t - 1))
        h = h_ref[t]
        h_first = jnp.where(low, h, 0.0)
        h_second = jnp.where(low, pltpu.roll(h, half_rows, 0), 0.0)
        for g8 in range(groups):
            base = t * slots + g8 * SUBLANES
            ps = []
            for r in range(SUBLANES):
                hi, lo = _expert_halves(tab_ref, off_ref[0, 0, base + r])
                ps.append(hi * h_first + lo * h_second)
            pairs = [ps[i] + pltpu.roll(ps[i + 1], half_rows, 0) for i in range(0, SUBLANES, 2)]
            row0 = pl.multiple_of(base * half_rows, SUBLANES * half_rows)
            q_ref[pl.ds(row0, SUBLANES * half_rows), :] = jnp.concatenate(pairs, axis=0).astype(BF16)
        return carry

    lax.fori_loop(0, tq, token, 0)
    lane_sums(tq - 1)

    fold = (lax.broadcasted_iota(jnp.int32, (qrows, slots), 0) // half_rows
            == lax.broadcasted_iota(jnp.int32, (qrows, slots), 1))
    fold = jnp.where(fold, 1.0, 0.0).astype(BF16)
    rs = r_ref[...]
    rh = rs.astype(BF16)
    rl = (rs - rh.astype(F32)).astype(BF16)
    act = _dot(rh, fold) + _dot(rl, fold)
    w_ref[...] = gate_ref[...] * _gelu(act)


def _smem_block(tq, slots):
    return pl.BlockSpec((1, 1, tq * slots), lambda i: (i, 0, 0), memory_space=pltpu.SMEM)


def _peer_u(off_blocks, hn3, gate, tab):
    n, rows, lanes = hn3.shape
    slots = gate.shape[1]
    tq = off_blocks.shape[-1] // slots
    return pl.pallas_call(
        functools.partial(_peer_u_kernel, tq=tq, slots=slots),
        grid=(n // tq,),
        in_specs=[_smem_block(tq, slots),
                  pl.BlockSpec((tq, rows, lanes), lambda i: (i, 0, 0)),
                  pl.BlockSpec((tq, slots), lambda i: (i, 0)),
                  pl.BlockSpec(tab.shape, lambda i: (0, 0), pipeline_mode=pl.Buffered(1))],
        out_specs=pl.BlockSpec((tq, slots), lambda i: (i, 0)),
        out_shape=_sds((n, slots)),
        scratch_shapes=[pltpu.VMEM((tq * slots * (SUBLANES // 2), lanes), BF16),
                        pltpu.VMEM((tq, slots * (SUBLANES // 2)), F32)],
        compiler_params=_cparams(("parallel",), 52),
        name="peer_u",
    )(off_blocks, hn3, gate, tab)


def _peer_v_kernel(off_ref, w_ref, tab_ref, o_ref, ws_ref, dg_ref, *, tq, slots):
    sub = lax.broadcasted_iota(jnp.int32, (SUBLANES, LANES), 0)
    low = sub < SUBLANES // 2
    groups = slots // SUBLANES

    eye = lax.broadcasted_iota(jnp.int32, (slots, slots), 0) == lax.broadcasted_iota(jnp.int32, (slots, slots), 1)

    def diag(t, carry):
        wrow = jnp.broadcast_to(w_ref[pl.ds(t, 1), :], (slots, slots))
        dg_ref[pl.ds(pl.multiple_of(t * slots, slots), slots), :] = jnp.where(eye, wrow, 0.0).astype(BF16)
        return carry

    lax.fori_loop(0, tq, diag, 0, unroll=4)
    ws_ref[...] = _dot(dg_ref[...], jnp.ones((slots, LANES), BF16))

    def token(t, carry):
        def group(g8, accs):
            first, second = list(accs[:2]), list(accs[2:])
            base = t * slots + g8 * SUBLANES
            for r in range(SUBLANES):
                hi, lo = _expert_halves(tab_ref, off_ref[0, 0, base + r])
                w = jnp.broadcast_to(ws_ref[pl.ds(base + r, 1), :], (SUBLANES, LANES))
                first[r % 2] = first[r % 2] + hi * w
                second[r % 2] = second[r % 2] + lo * w
            return tuple(first + second)

        zero = jnp.zeros((SUBLANES, LANES), F32)
        f0, f1, s0, s1 = lax.fori_loop(0, groups, group, (zero,) * 4, unroll=4)
        o_ref[t] = jnp.where(low, f0 + f1, pltpu.roll(s0 + s1, SUBLANES // 2, 0))
        return carry

    lax.fori_loop(0, tq, token, 0)


def _peer_v(off_blocks, w, tab):
    n, slots = w.shape
    tq = off_blocks.shape[-1] // slots
    lanes = tab.shape[-1]
    assert slots == lanes, "the weight splat uses one (slots, lanes) identity tile"
    return pl.pallas_call(
        functools.partial(_peer_v_kernel, tq=tq, slots=slots),
        grid=(n // tq,),
        in_specs=[_smem_block(tq, slots),
                  pl.BlockSpec((tq, slots), lambda i: (i, 0)),
                  pl.BlockSpec(tab.shape, lambda i: (0, 0), pipeline_mode=pl.Buffered(1))],
        out_specs=pl.BlockSpec((tq, SUBLANES, lanes), lambda i: (i, 0, 0)),
        out_shape=_sds((n, SUBLANES, lanes)),
        scratch_shapes=[pltpu.VMEM((tq * slots, lanes), F32), pltpu.VMEM((tq * slots, slots), BF16)],
        compiler_params=_cparams(("parallel",), 52),
        name="peer_v",
    )(off_blocks, w, tab)


def _final_kernel(h1_ref, p_ref, g2_ref, nf_ref, o_ref):
    h = h1_ref[0] + g2_ref[0] * p_ref[0]
    ms = jnp.mean(h * h, axis=-1, keepdims=True)
    o_ref[0] = h * lax.rsqrt(ms + NORM_EPS) * nf_ref[...]


def _final_norm(h1, peer, g2, nfg):
    b, l, d = h1.shape
    tm = min(512, l)
    tok = pl.BlockSpec((1, tm, d), lambda i, j: (i, j, 0))
    return pl.pallas_call(
        _final_kernel,
        grid=(b, l // tm),
        in_specs=[tok, tok, pl.BlockSpec((1, 1, d), lambda i, j: (i, 0, 0)), pl.BlockSpec((1, d), lambda i, j: (0, 0))],
        out_specs=tok,
        out_shape=_sds((b, l, d)),
        compiler_params=_cparams(("parallel", "parallel")),
        name="final_norm",
    )(h1, peer, g2, nfg)


def _pad_cols(a, n):
    return jnp.pad(a, [(0, 0)] * (a.ndim - 1) + [(0, n - a.shape[-1])])


def _to_scan(a, heads):
    *lead, b, t, w = a.shape
    n = w // heads
    a = a.reshape(*lead, b, t, heads, n)
    nl = len(lead)
    perm = tuple(range(nl)) + (nl + 1, nl + 3, nl + 2, nl)
    return a.transpose(perm).reshape(*lead, t, n, heads * b)


def _from_scan(y, heads):
    t, n, c = y.shape
    b = c // heads
    return y.reshape(t, n, heads, b).transpose(3, 0, 2, 1).reshape(b, t, heads * n)


def _layer(h, hc, c, c_ctx, p, norm_f_g):
    b, l, d = h.shape
    lc = hc.shape[1]
    s5w = p['s5_d'].shape[-1]
    rww = p['rw_w0'].shape[-1]
    heads, head = p['rw_r_k'].shape
    n_dec, n_aaa, n_gate = p['rw_w_w2'].shape[1], p['rw_w_a2'].shape[1], p['rw_w_g2'].shape[0]
    n_lora = n_dec + n_aaa + n_gate
    lora_pad = -(-n_lora // LANES) * LANES

    rows = -(-(b + 1) // SUBLANES) * SUBLANES
    cc = jnp.zeros((rows, d), F32).at[:b].set(c).at[b].set(c_ctx)
    mods = _ada_mods(cc, p['w_ada'], p['b_ada'])
    sh1, sc1, g1, sh2, sc2, g2 = [m.reshape(rows, 1, d) for m in jnp.split(mods, 6, axis=-1)]
    lat = lambda m: m[:b]
    ctxm = lambda m: m[b:b + 1]

    w_in = p['w_in']
    w_s5 = w_in[:, :s5w].astype(BF16)
    w_main = w_in[:, s5w:s5w + 3 * rww].astype(BF16)
    w_lora = _pad_cols(w_in[:, s5w + 3 * rww:], lora_pad).astype(BF16)
    n1g = p['norm1_g'].reshape(1, d)
    u_l, zm_l, zl_l = _in_proj(h, lat(sh1), lat(sc1), n1g, w_s5, w_main, w_lora)
    u_c, zm_c, zl_c = _in_proj(hc, ctxm(sh1), ctxm(sc1), n1g, w_s5, w_main, w_lora)

    mu = p['rw_mu']
    mu_main = mu[:3 * rww].reshape(1, -1)
    mu_lora = _pad_cols(mu[3 * rww:], lora_pad).reshape(1, -1)
    place = lambda w, r0: jnp.zeros(w.shape[:-2] + (lora_pad, rww), F32).at[..., r0:r0 + w.shape[-2], :].set(w).astype(BF16)
    rw_params = (place(p['rw_w_g2'], n_dec + n_aaa), place(p['rw_w_w2'], 0), place(p['rw_w_a2'], n_dec),
                 p['rw_w0'].reshape(2, 1, rww), p['rw_a0'].reshape(2, 1, rww),
                 p['rw_k_k'].reshape(1, rww), p['rw_k_a'].reshape(1, rww), p['rw_r_k'].reshape(1, rww))
    r_l, v_l, kk_l, dec_l, kd_l, bb_l, gate_l, bonus_l = _rwkv_prep(zm_l, zl_l, (mu_main, mu_lora) + rw_params, head, True)
    r_c, v_c, kk_c, dec_c, kd_c, bb_c = _rwkv_prep(zm_c, zl_c, (mu_main, mu_lora) + rw_params, head, False)
    cat = lambda a_c, a_l: _to_scan(jnp.concatenate([a_c, a_l], axis=-2), heads)
    ys = _rwkv_scan((cat(r_c, r_l), cat(v_c, v_l), cat(kk_c, kk_l)),
                    (cat(dec_c, dec_l), cat(kd_c, kd_l), cat(bb_c, bb_l)), lc, head)
    rw_y = _from_scan((ys[0] + ys[1])[lc:], heads)

    tables = _s5_tables(p['s5_a_re'], p['s5_a_im'], p['s5_log_dt'], p['s5_b_re'], p['s5_b_im'],
                        p['s5_c_re'], p['s5_c_im'], S5_CHUNK)
    s5_y = _s5_mix(u_c, u_l, p['s5_d'], tables)

    w_out = p['w_out']
    h1, hn2 = _mix_out(s5_y, rw_y, bonus_l, gate_l, h, lat(g1), lat(sh2), lat(sc2), p['norm2_g'].reshape(1, d),
                       p['s5_w_glu'].astype(BF16), p['s5_b_glu'].reshape(1, s5w),
                       p['rw_ln_w'].reshape(1, rww), p['rw_ln_b'].reshape(1, rww),
                       w_out[:s5w].astype(BF16), w_out[s5w:].astype(BF16), head)

    n = b * l
    keys = p['peer_keys']
    pheads, _, nkeys, half = keys.shape
    assert d == VREG_WORDS, "peer_u / peer_v hold one token's features in a single (8, 128) tile"
    off, gate = _peer_topk(hn2.reshape(n, d), p['peer_w_q'].T.astype(BF16),
                           keys.reshape(pheads * 2, nkeys, half).astype(BF16), d // (2 * LANES))
    slots = gate.shape[1]
    tq = min(PEER_TILE, n)
    blocks = lambda a: a.reshape(n // tq, 1, tq * slots)
    w = _peer_u(blocks(off), hn2.reshape(n, SUBLANES, LANES), gate, _pack_table(p['peer_u']))
    peer = _peer_v(blocks(off), w, _pack_table(p['peer_v']))

    return _final_norm(h1, peer.reshape(b, l, d), lat(g2), norm_f_g.reshape(1, d))


def kernel(x, c, ctx, c_ctx, w_ada, b_ada, norm1_g, norm2_g, w_in, s5_a_re, s5_a_im, s5_log_dt, s5_b_re, s5_b_im, s5_c_re, s5_c_im, s5_d, s5_w_glu, s5_b_glu, rw_mu, rw_w0, rw_w_w2, rw_a0, rw_w_a2, rw_w_g2, rw_k_k, rw_k_a, rw_r_k, rw_ln_w, rw_ln_b, w_out, peer_w_q, peer_keys, peer_u, peer_v, norm_f_g):
    assert w_ada.shape[0] == 1, "single-layer block: the context stream is never updated"
    p = dict(w_ada=w_ada[0], b_ada=b_ada[0], norm1_g=norm1_g[0], norm2_g=norm2_g[0], w_in=w_in[0],
             s5_a_re=s5_a_re[0], s5_a_im=s5_a_im[0], s5_log_dt=s5_log_dt[0], s5_b_re=s5_b_re[0], s5_b_im=s5_b_im[0],
             s5_c_re=s5_c_re[0], s5_c_im=s5_c_im[0], s5_d=s5_d[0], s5_w_glu=s5_w_glu[0], s5_b_glu=s5_b_glu[0],
             rw_mu=rw_mu[0], rw_w0=rw_w0[0], rw_w_w2=rw_w_w2[0], rw_a0=rw_a0[0], rw_w_a2=rw_w_a2[0],
             rw_w_g2=rw_w_g2[0], rw_k_k=rw_k_k[0], rw_k_a=rw_k_a[0], rw_r_k=rw_r_k[0], rw_ln_w=rw_ln_w[0],
             rw_ln_b=rw_ln_b[0], w_out=w_out[0], peer_w_q=peer_w_q[0], peer_keys=peer_keys[0],
             peer_u=peer_u[0], peer_v=peer_v[0])
    return _layer(x, ctx, c, c_ctx, p, norm_f_g)
```

```python
import functools
import math

import jax
import jax.numpy as jnp
import numpy as np
from jax import lax
from jax.experimental import pallas as pl
from jax.experimental.pallas import tpu as pltpu

F32 = jnp.float32
BF16 = jnp.bfloat16
HIGHEST = lax.Precision.HIGHEST

SUBLANES = 8
LANES = 128
VREG_WORDS = SUBLANES * LANES

GRID_W = 64
NORM_EPS = 1e-6
GN_EPS = 64e-5
KK_EPS = 1e-12
PEER_TOPK = 16

S5_CHUNK = 32
SCAN_TILE = 64
PEER_TILE = 64


def _dot(a, b):
    return jnp.dot(a, b, preferred_element_type=F32)


def _cparams(sem, vmem_mb=48):
    return pltpu.CompilerParams(dimension_semantics=sem, vmem_limit_bytes=vmem_mb * 1024 * 1024)


def _sds(shape, dtype=F32):
    return jax.ShapeDtypeStruct(shape, dtype)


def _mods_kernel(c_ref, w_ref, b_ref, o_ref):
    c = c_ref[...]
    s = c * jax.nn.sigmoid(c)
    o_ref[...] = jnp.dot(s, w_ref[...], precision=HIGHEST, preferred_element_type=F32) + b_ref[...]


def _ada_mods(cc, w_ada, b_ada):
    rows, d = cc.shape
    n = w_ada.shape[1]
    bn = d if n % d == 0 else n
    return pl.pallas_call(
        _mods_kernel,
        grid=(n // bn,),
        in_specs=[pl.BlockSpec((rows, d), lambda j: (0, 0)),
                  pl.BlockSpec((d, bn), lambda j: (0, j)),
                  pl.BlockSpec((1, bn), lambda j: (0, j))],
        out_specs=pl.BlockSpec((rows, bn), lambda j: (0, j)),
        out_shape=_sds((rows, n)),
        compiler_params=_cparams(("arbitrary",)),
        name="ada_mods",
    )(cc, w_ada, b_ada.reshape(1, n))


def _inproj_kernel(x_ref, sh_ref, sc_ref, g_ref, ws_ref, wz_ref, wl_ref, u_ref, z_ref, l_ref):
    x = x_ref[0]
    ms = jnp.mean(x * x, axis=-1, keepdims=True)
    hn = (x * lax.rsqrt(ms + NORM_EPS) * g_ref[...]) * (1.0 + sc_ref[0]) + sh_ref[0]
    hb = hn.astype(BF16)
    u_ref[0] = _dot(hb, ws_ref[...])
    z_ref[0] = _dot(hb, wz_ref[...])
    l_ref[0] = _dot(hb, wl_ref[...])


def _in_proj(x, shift, scale, gain, w_s5, w_main, w_lora):
    b, l, d = x.shape
    tm = min(512, l)
    bm = shift.shape[0]
    mod_map = (lambda i, j: (i, 0, 0)) if bm > 1 else (lambda i, j: (0, 0, 0))
    full = lambda a: pl.BlockSpec(a.shape, lambda i, j: (0,) * a.ndim)
    return pl.pallas_call(
        _inproj_kernel,
        grid=(b, l // tm),
        in_specs=[pl.BlockSpec((1, tm, d), lambda i, j: (i, j, 0)),
                  pl.BlockSpec((1, 1, d), mod_map), pl.BlockSpec((1, 1, d), mod_map),
                  full(gain), full(w_s5), full(w_main), full(w_lora)],
        out_specs=[pl.BlockSpec((1, tm, w.shape[1]), lambda i, j: (i, j, 0)) for w in (w_s5, w_main, w_lora)],
        out_shape=[_sds((b, l, w.shape[1])) for w in (w_s5, w_main, w_lora)],
        compiler_params=_cparams(("parallel", "parallel")),
        name="in_proj",
    )(x, shift, scale, gain, w_s5, w_main, w_lora)


def _seg_matrix(width, seg):
    r = lax.broadcasted_iota(jnp.int32, (width, width), 0) // seg
    c = lax.broadcasted_iota(jnp.int32, (width, width), 1) // seg
    return jnp.where(r == c, 1.0, 0.0).astype(BF16)


def _seg_sum(x, seg_m):
    hi = x.astype(BF16)
    lo = (x - hi.astype(F32)).astype(BF16)
    return _dot(hi, seg_m) + _dot(lo, seg_m)


def _softplus(x):
    return jnp.maximum(x, 0.0) + jnp.log1p(jnp.exp(-jnp.abs(x)))


def _rwkv_streams(zz, zl, p, head, outs, with_gate):
    (wg_ref, ww_ref, wa_ref, w0_ref, a0_ref, kk_ref, ka_ref, rk_ref) = p
    width = kk_ref.shape[-1]
    r = zz[:, :width]
    k = zz[:, width:2 * width]
    v = zz[:, 2 * width:3 * width]
    seg_m = _seg_matrix(width, head)
    kk = k * kk_ref[...]
    kk = kk * lax.rsqrt(_seg_sum(kk * kk, seg_m) + KK_EPS)
    outs["r"][0] = r
    outs["v"][0] = v
    outs["kk"][0] = kk
    tl = jnp.tanh(zl).astype(BF16)
    zb = zl.astype(BF16)
    bonus = None
    for d in range(2):
        w = -_softplus(-(w0_ref[d] + _dot(tl, ww_ref[d]))) - 0.5
        outs["dec"][d, 0] = jnp.exp(-jnp.exp(w))
        a = jax.nn.sigmoid(a0_ref[d] + _dot(zb, wa_ref[d]))
        kd = k * (1.0 + (a - 1.0) * ka_ref[...])
        outs["kd"][d, 0] = kd
        outs["bb"][d, 0] = kk * a
        if with_gate:
            bd = _seg_sum(r * kd * rk_ref[...], seg_m) * v
            bonus = bd if bonus is None else bonus + bd
    if with_gate:
        outs["bonus"][0] = bonus
        outs["g"][0] = _dot(jax.nn.sigmoid(zl).astype(BF16), wg_ref[...])


def _lerp_shift(z, shifted, mu):
    return z + (shifted - z) * mu


def _grid_shift(zc, zp, zn, has_prev, has_next):
    t, c = zc.shape
    tok = lax.broadcasted_iota(jnp.int32, (t, 1), 0) % GRID_W
    col = lax.broadcasted_iota(jnp.int32, (1, c), 1) % 4
    left = jnp.where(tok == 0, 0.0, pltpu.roll(zc, 1, 0))
    right = jnp.where(tok == GRID_W - 1, 0.0, pltpu.roll(zc, t - 1, 0))
    zp = jnp.where(has_prev, zp, 0.0)
    zn = jnp.where(has_next, zn, 0.0)
    if t > GRID_W:
        up = jnp.concatenate([zp, zc[:t - GRID_W]], axis=0)
        down = jnp.concatenate([zc[GRID_W:], zn], axis=0)
    else:
        up, down = zp, zn
    return jnp.where(col == 0, left, jnp.where(col == 1, right, jnp.where(col == 2, up, down)))


def _seq_shift(zc):
    t, c = zc.shape
    tok = lax.broadcasted_iota(jnp.int32, (t, 1), 0)
    col = lax.broadcasted_iota(jnp.int32, (1, c), 1) % 2
    prev = jnp.where(tok == 0, 0.0, pltpu.roll(zc, 1, 0))
    nxt = jnp.where(tok == t - 1, 0.0, pltpu.roll(zc, t - 1, 0))
    return jnp.where(col == 0, prev, nxt)


_STREAMS = ("r", "v", "kk", "dec", "kd", "bb")


def _rwkv_prep_lat_kernel(zc_ref, zp_ref, zn_ref, lc_ref, lp_ref, ln_ref, mum_ref, mul_ref, *rest, head):
    p, o = rest[:8], rest[8 + len(_STREAMS):]
    outs = dict(zip(_STREAMS + ("g", "bonus"), o))
    j = pl.program_id(1)
    has_prev = j > 0
    has_next = j < pl.num_programs(1) - 1
    zc = zc_ref[0]
    lc = lc_ref[0]
    zz = _lerp_shift(zc, _grid_shift(zc, zp_ref[0], zn_ref[0], has_prev, has_next), mum_ref[...])
    zl = _lerp_shift(lc, _grid_shift(lc, lp_ref[0], ln_ref[0], has_prev, has_next), mul_ref[...])
    _rwkv_streams(zz, zl, p, head, outs, True)


def _rwkv_prep_ctx_kernel(zc_ref, lc_ref, mum_ref, mul_ref, *rest, head):
    p, o = rest[:8], rest[8:]
    outs = dict(zip(_STREAMS, o))
    zc = zc_ref[0]
    lc = lc_ref[0]
    zz = _lerp_shift(zc, _seq_shift(zc), mum_ref[...])
    zl = _lerp_shift(lc, _seq_shift(lc), mul_ref[...])
    _rwkv_streams(zz, zl, p, head, outs, False)


def _rwkv_prep(z_main, z_lora, params, head, l_total, ctx_streams=None):
    b, l, cm = z_main.shape
    cl = z_lora.shape[-1]
    width = cm // 3
    full = lambda a: pl.BlockSpec(a.shape, lambda i, j: (0,) * a.ndim)
    grid_mode = ctx_streams is not None
    if grid_mode:
        l_ctx = l_total - l
        rows_per_tile = next(r for r in (8, 4, 2, 1) if l % (r * GRID_W) == 0 and l_ctx % (r * GRID_W) == 0)
        tt = rows_per_tile * GRID_W
        nt = l // tt
        first = l_ctx // tt
        last_row = l // GRID_W - 1
        cur = lambda c: pl.BlockSpec((1, tt, c), lambda i, j: (i, j, 0))
        prev = lambda c: pl.BlockSpec((1, GRID_W, c), lambda i, j: (i, jnp.maximum(j * rows_per_tile - 1, 0), 0))
        nxt = lambda c: pl.BlockSpec((1, GRID_W, c), lambda i, j: (i, jnp.minimum((j + 1) * rows_per_tile, last_row), 0))
        data = [z_main, z_main, z_main, z_lora, z_lora, z_lora]
        data_specs = [cur(cm), prev(cm), nxt(cm), cur(cl), prev(cl), nxt(cl)]
        kern = functools.partial(_rwkv_prep_lat_kernel, head=head)
        extra = list(ctx_streams)
        extra_specs = [pl.BlockSpec(memory_space=pl.ANY)] * len(extra)
        n_in = len(data) + len(params)
        aliases = {n_in + k: k for k in range(len(extra))}
        lat_only = 2
    else:
        tt, nt, first = l, 1, 0
        data = [z_main, z_lora]
        data_specs = [pl.BlockSpec((1, tt, cm), lambda i, j: (i, j, 0)), pl.BlockSpec((1, tt, cl), lambda i, j: (i, j, 0))]
        kern = functools.partial(_rwkv_prep_ctx_kernel, head=head)
        extra, extra_specs, aliases, lat_only = [], [], {}, 0
    single = pl.BlockSpec((1, tt, width), lambda i, j: (i, first + j, 0))
    double = pl.BlockSpec((2, 1, tt, width), lambda i, j: (0, i, first + j, 0))
    local = pl.BlockSpec((1, tt, width), lambda i, j: (i, j, 0))
    out_specs = [single] * 3 + [double] * 3 + [local] * lat_only
    out_shape = [_sds((b, l_total, width))] * 3 + [_sds((2, b, l_total, width))] * 3 + [_sds((b, l, width))] * lat_only
    return pl.pallas_call(
        kern,
        grid=(b, nt),
        in_specs=data_specs + [full(a) for a in params] + extra_specs,
        out_specs=out_specs,
        out_shape=out_shape,
        input_output_aliases=aliases,
        compiler_params=_cparams(("parallel", "parallel")),
        name="rwkv_prep_lat" if grid_mode else "rwkv_prep_ctx",
    )(*data, *params, *extra)


def _scan_kernel(r_ref, v_ref, kk_ref, w_ref, kd_ref, bb_ref, y_ref, s_ref, *, tt, head):
    d = pl.program_id(0)

    @pl.when(pl.program_id(2) == 0)
    def _():
        s_ref[...] = jnp.zeros_like(s_ref)

    def tok(i):
        return jnp.where(d == 0, i, tt - 1 - i)

    def row(ref, t, k):
        return jnp.broadcast_to(ref[t, pl.ds(k, 1), :], s_ref.shape[1:])

    t0 = tok(0)
    sa0 = jnp.zeros(s_ref.shape[1:], F32)
    for k in range(head):
        sa0 = sa0 - s_ref[k] * row(kk_ref, t0, k)

    def token(i, sa):
        t = tok(i)
        tn = tok(jnp.minimum(i + 1, tt - 1))
        vv = v_ref[t]
        y = jnp.zeros_like(sa)
        sa_next = jnp.zeros_like(sa)
        for k in range(head):
            s = s_ref[k] * row(w_ref, t, k) + sa * row(bb_ref, t, k) + vv * row(kd_ref, t, k)
            s_ref[k] = s
            y = y + s * row(r_ref, t, k)
            sa_next = sa_next - s * row(kk_ref, tn, k)
        y_ref[t] = y
        return sa_next

    lax.fori_loop(0, tt, token, sa0)


def _rwkv_scan(shared, directional, n_ctx_tok, head):
    ltot, _, chains = shared[0].shape
    tt = SCAN_TILE
    lb = min(chains, LANES)
    nlg = chains // lb
    nt = ltot // tt
    nc = n_ctx_tok // tt

    def tile(d, j):
        back = jnp.where(j < nc, nc - 1 - j, nt + nc - 1 - j)
        return jnp.where(d == 0, j, back)

    sh_spec = pl.BlockSpec((tt, head, lb), lambda d, g, j: (tile(d, j), 0, g))
    di_spec = pl.BlockSpec((None, tt, head, lb), lambda d, g, j: (d, tile(d, j), 0, g))
    return pl.pallas_call(
        functools.partial(_scan_kernel, tt=tt, head=head),
        grid=(2, nlg, nt),
        in_specs=[sh_spec] * 3 + [di_spec] * 3,
        out_specs=di_spec,
        out_shape=_sds((2, ltot, head, chains)),
        scratch_shapes=[pltpu.VMEM((head, head, lb), F32)],
        compiler_params=_cparams(("arbitrary", "arbitrary", "arbitrary")),
        name="rwkv_scan",
    )(*shared, *directional)


def _s5_kernel(x_ref, d_ref, m_ref, pin_ref, pout_ref, ar_ref, ai_ref, y_ref, hall_ref, *, nb, n_chunks, n_ctx_chunks):
    x = x_ref[0]
    xb = x.astype(BF16)
    lat0 = n_ctx_chunks * nb
    xl = xb[lat0:]
    y = d_ref[0] * x[lat0:]
    half = ar_ref.shape[-1] // 2
    fwd = list(range(n_chunks))
    bwd = list(range(n_ctx_chunks - 1, -1, -1)) + list(range(n_chunks - 1, n_ctx_chunks - 1, -1))
    for d, order in enumerate((fwd, bwd)):
        v = _dot(xb, pin_ref[d, 0])
        ar = ar_ref[d, 0]
        ai = ai_ref[d, 0]
        h = jnp.zeros((nb, 2 * half), F32)
        for c in order:
            hall_ref[c * nb:(c + 1) * nb, :] = h
            h = h * ar + pltpu.roll(h, half, 1) * ai + v[c * nb:(c + 1) * nb]
        y = y + _dot(xl, m_ref[d, 0]) + _dot(hall_ref[lat0:, :].astype(BF16), pout_ref[d, 0])
    y_ref[0] = y


def _s5_tables(a_re, a_im, log_dt, b_re, b_im, c_re, c_im, tc):
    a_re, a_im, b_re, b_im = (t.astype(F32) for t in (a_re, a_im, b_re, b_im))
    c_re, c_im = c_re.astype(F32), c_im.astype(F32)
    dt = jnp.exp(log_dt.astype(F32))[..., None]
    mag = jnp.exp(dt * a_re)
    ab_re, ab_im = mag * jnp.cos(dt * a_im), mag * jnp.sin(dt * a_im)
    nr, ni = ab_re - 1.0, ab_im
    den = a_re * a_re + a_im * a_im
    cf_re = (nr * a_re + ni * a_im) / den
    cf_im = (ni * a_re - nr * a_im) / den
    bb_re = cf_re[..., None] * b_re - cf_im[..., None] * b_im
    bb_im = cf_re[..., None] * b_im + cf_im[..., None] * b_re

    k = jnp.arange(tc + 1, dtype=F32)[:, None, None, None]
    pmag = jnp.exp(k * (dt * a_re)[None])
    pang = k * (dt * a_im)[None]
    pw_re, pw_im = pmag * jnp.cos(pang), pmag * jnp.sin(pang)

    ab_k_re = pw_re[..., None] * bb_re[None] - pw_im[..., None] * bb_im[None]
    ab_k_im = pw_re[..., None] * bb_im[None] + pw_im[..., None] * bb_re[None]
    ca_k_re = c_re[None] * pw_re[:, :, :, None, :] - c_im[None] * pw_im[:, :, :, None, :]
    ca_k_im = c_re[None] * pw_im[:, :, :, None, :] + c_im[None] * pw_re[:, :, :, None, :]

    taps = (jnp.einsum('dghp,kdgpj->kdghj', c_re, ab_k_re[:tc], precision=HIGHEST)
            - jnp.einsum('dghp,kdgpj->kdghj', c_im, ab_k_im[:tc], precision=HIGHEST))
    s_idx = jnp.arange(tc)[:, None]
    t_idx = jnp.arange(tc)[None, :]
    hc = b_re.shape[-1]
    g = a_re.shape[1]
    p = a_re.shape[2]

    def toeplitz(tp, lag, valid):
        m = jnp.where(valid[:, :, None, None, None], tp[jnp.clip(lag, 0, tc - 1)], 0.0)
        return m.transpose(2, 0, 4, 1, 3).reshape(g, tc * hc, tc * hc)

    m_f = toeplitz(taps[:, 0], t_idx - s_idx, t_idx >= s_idx)
    m_b = toeplitz(taps[:, 1], s_idx - t_idx, s_idx >= t_idx)
    m = jnp.stack([m_f, m_b]).astype(BF16)

    def pin_of(re, im):
        both = jnp.concatenate([re, im], axis=2)
        return both.transpose(1, 0, 3, 2).reshape(g, tc * hc, 2 * p)

    pin = jnp.stack([pin_of(ab_k_re[:tc, 0][::-1], ab_k_im[:tc, 0][::-1]),
                     pin_of(ab_k_re[:tc, 1], ab_k_im[:tc, 1])]).astype(BF16)

    def pout_of(re, im):
        both = jnp.concatenate([re, -im], axis=3)
        return both.transpose(1, 3, 0, 2).reshape(g, 2 * p, tc * hc)

    pout = jnp.stack([pout_of(ca_k_re[1:tc + 1, 0], ca_k_im[1:tc + 1, 0]),
                      pout_of(ca_k_re[1:tc + 1, 1][::-1], ca_k_im[1:tc + 1, 1][::-1])]).astype(BF16)

    at_re, at_im = pw_re[tc], pw_im[tc]
    ar = jnp.concatenate([at_re, at_re], axis=-1)[:, :, None, :]
    ai = jnp.concatenate([-at_im, at_im], axis=-1)[:, :, None, :]
    return m, pin, pout, ar, ai


def _s5_mix(u_ctx, u_lat, s5_d, tables):
    m, pin, pout, ar, ai = tables
    b, lc, width = u_ctx.shape
    l = u_lat.shape[1]
    g = m.shape[1]
    hc = width // g
    tc = S5_CHUNK
    w = tc * hc
    ncc, nlc = lc // tc, l // tc
    nch = ncc + nlc
    u = jnp.concatenate([u_ctx, u_lat], axis=1)
    xg = u.reshape(b, nch, tc, g, hc).transpose(3, 1, 0, 2, 4).reshape(g, nch * b, w)
    dg = jnp.tile(s5_d.reshape(g, 1, hc), (1, tc, 1)).reshape(g, 1, w)
    pgrp = lambda a: pl.BlockSpec((2, 1) + a.shape[2:], lambda i: (0, i, 0, 0))
    y = pl.pallas_call(
        functools.partial(_s5_kernel, nb=b, n_chunks=nch, n_ctx_chunks=ncc),
        grid=(g,),
        in_specs=[pl.BlockSpec((1, nch * b, w), lambda i: (i, 0, 0)),
                  pl.BlockSpec((1, 1, w), lambda i: (i, 0, 0)),
                  pgrp(m), pgrp(pin), pgrp(pout), pgrp(ar), pgrp(ai)],
        out_specs=pl.BlockSpec((1, nlc * b, w), lambda i: (i, 0, 0)),
        out_shape=_sds((g, nlc * b, w)),
        scratch_shapes=[pltpu.VMEM((nch * b, pin.shape[-1]), F32)],
        compiler_params=_cparams(("parallel",), 56),
        name="s5_mix",
    )(xg, dg, m, pin, pout, ar, ai)
    return y.reshape(g, nlc, b, tc, hc).transpose(2, 1, 3, 0, 4).reshape(b, l, width)


def _gelu(x):
    return 0.5 * x * (1.0 + lax.erf(x * (1.0 / math.sqrt(2.0))))


def _mix_out_kernel(s5_ref, rw_ref, bon_ref, g_ref, x_ref, g1_ref, sh2_ref, sc2_ref, n2_ref,
                    wglu_ref, bglu_ref, lnw_ref, lnb_ref, wos_ref, wor_ref, h1_ref, hn_ref, *, head):
    y1 = _gelu(s5_ref[0])
    s5o = y1 * jax.nn.sigmoid(_dot(y1.astype(BF16), wglu_ref[...]) + bglu_ref[...])
    y = rw_ref[0]
    seg_m = _seg_matrix(y.shape[-1], head)
    inv = 1.0 / head
    mu = _seg_sum(y, seg_m) * inv
    yc = y - mu
    var = _seg_sum(yc * yc, seg_m) * inv
    yn = yc * lax.rsqrt(var + GN_EPS) * lnw_ref[...] + lnb_ref[...]
    rwo = (yn + bon_ref[0]) * g_ref[0]
    mix = _dot(s5o.astype(BF16), wos_ref[...]) + _dot(rwo.astype(BF16), wor_ref[...])
    h1 = x_ref[0] + g1_ref[0] * mix
    h1_ref[0] = h1
    ms = jnp.mean(h1 * h1, axis=-1, keepdims=True)
    hn_ref[0] = (h1 * lax.rsqrt(ms + NORM_EPS) * n2_ref[...]) * (1.0 + sc2_ref[0]) + sh2_ref[0]


def _mix_out(s5y, rwy, bonus, gate, x, g1, sh2, sc2, n2g, w_glu, b_glu, ln_w, ln_b, w_out_s, w_out_r, head):
    b, l, d = x.shape
    width = s5y.shape[-1]
    tm = min(512, l)
    tok = lambda c: pl.BlockSpec((1, tm, c), lambda i, j: (i, j, 0))
    mod = pl.BlockSpec((1, 1, d), lambda i, j: (i, 0, 0))
    full = lambda a: pl.BlockSpec(a.shape, lambda i, j: (0,) * a.ndim)
    consts = (n2g, w_glu, b_glu, ln_w, ln_b, w_out_s, w_out_r)
    return pl.pallas_call(
        functools.partial(_mix_out_kernel, head=head),
        grid=(b, l // tm),
        in_specs=[tok(width)] * 4 + [tok(d), mod, mod, mod] + [full(a) for a in consts],
        out_specs=[tok(d), tok(d)],
        out_shape=[_sds((b, l, d)), _sds((b, l, d))],
        compiler_params=_cparams(("parallel", "parallel")),
        name="mix_out",
    )(s5y, rwy, bonus, gate, x, g1, sh2, sc2, *consts)


def _staircase(k):
    return [(a, b) for a in range(k) for b in range(k) if (a + 1) * (b + 1) <= k]


def _topk_rows(s, k):
    rows = s.shape[0]
    iota = lax.broadcasted_iota(jnp.int32, s.shape, 0)
    vals, idxs = [], []
    for _ in range(k):
        m = jnp.max(s, axis=0, keepdims=True)
        idx = jnp.min(jnp.where(s == m, iota, rows), axis=0, keepdims=True)
        vals.append(m)
        idxs.append(idx)
        s = jnp.where(iota == idx, -jnp.inf, s)
    return jnp.concatenate(vals, axis=0), jnp.concatenate(idxs, axis=0).astype(F32)


def _peer_topk_kernel(hn_ref, wq_ref, keys_ref, sel1_ref, sel2_ref, pad_ref, off_ref, gate_ref, *, heads, nkeys, half,
                      rows_per_expert):
    hb = hn_ref[...].astype(BF16)
    nt = (((1,), (1,)), ((), ()))
    sel1 = sel1_ref[...]
    sel2 = sel2_ref[...]
    pick = lambda sel, a: jnp.dot(sel, a, precision=HIGHEST, preferred_element_type=F32)
    ids, gates = [], []
    for h in range(heads):
        top = []
        for c in range(2):
            row0 = (h * 2 + c) * half
            qt = lax.dot_general(wq_ref[row0:row0 + half, :], hb, nt, preferred_element_type=F32)
            st = _dot(keys_ref[h * 2 + c], qt.astype(BF16))
            top.append(_topk_rows(st, PEER_TOPK))
        (s1, i1), (s2, i2) = top
        cand = pick(sel1, s1) + pick(sel2, s2) + pad_ref[...]
        cid = pick(sel1, i1) * nkeys + pick(sel2, i2)
        best, pos = _topk_rows(cand, PEER_TOPK)
        riota = lax.broadcasted_iota(jnp.int32, cid.shape, 0).astype(F32)
        eid = jnp.concatenate(
            [jnp.sum(jnp.where(riota == pos[r:r + 1], cid, 0.0), axis=0, keepdims=True) for r in range(PEER_TOPK)], axis=0)
        e = jnp.exp(best - jnp.max(best, axis=0, keepdims=True))
        gates.append(e / jnp.sum(e, axis=0, keepdims=True))
        ids.append(eid)
    eid = jnp.concatenate(ids, axis=0).T.astype(jnp.int32)
    off_ref[...] = eid * rows_per_expert
    gate_ref[...] = jnp.concatenate(gates, axis=0).T


def _peer_topk(hn, wq_t, keys, rows_per_expert):
    n, d = hn.shape
    heads = keys.shape[0] // 2
    nkeys, half = keys.shape[1], keys.shape[2]
    tm = min(256, n)
    pairs = _staircase(PEER_TOPK)
    rows = -(-len(pairs) // SUBLANES) * SUBLANES
    sel1 = np.zeros((rows, PEER_TOPK), np.float32)
    sel2 = np.zeros((rows, PEER_TOPK), np.float32)
    pad = np.zeros((rows, 1), np.float32)
    for i, (a, b) in enumerate(pairs):
        sel1[i, a] = 1.0
        sel2[i, b] = 1.0
    pad[len(pairs):] = -np.inf
    slots = heads * PEER_TOPK
    full = lambda a: pl.BlockSpec(a.shape, lambda i: (0,) * a.ndim)
    consts = (wq_t, keys, jnp.asarray(sel1), jnp.asarray(sel2), jnp.asarray(pad))
    return pl.pallas_call(
        functools.partial(_peer_topk_kernel, heads=heads, nkeys=nkeys, half=half, rows_per_expert=rows_per_expert),
        grid=(n // tm,),
        in_specs=[pl.BlockSpec((tm, d), lambda i: (i, 0))] + [full(a) for a in consts],
        out_specs=[pl.BlockSpec((tm, slots), lambda i: (i, 0))] * 2,
        out_shape=[_sds((n, slots), jnp.int32), _sds((n, slots))],
        compiler_params=_cparams(("parallel",)),
        name="peer_topk",
    )(hn, *consts)


def _pack_table(tab):
    e, d = tab.shape
    half = d // 2
    bits = lax.bitcast_convert_type(tab.astype(BF16), jnp.uint16).astype(jnp.uint32)
    packed = ((bits[:, :half] << 16) | bits[:, half:]).reshape(e * (half // LANES), LANES)
    return jnp.pad(packed, ((0, SUBLANES), (0, 0)))


def _expert_halves(tab_ref, off):
    bits = tab_ref[pl.ds(off, SUBLANES), :]
    return pltpu.bitcast(bits & jnp.uint32(0xFFFF0000), F32), pltpu.bitcast(bits << 16, F32)


def _peer_u_kernel(off_ref, h_ref, gate_ref, tab_ref, w_ref, q_ref, r_ref, *, tq, slots):
    half_rows = SUBLANES // 2
    sub = lax.broadcasted_iota(jnp.int32, (SUBLANES, LANES), 0)
    low = sub < half_rows
    groups = slots // SUBLANES
    qrows = slots * half_rows

    ones = jnp.ones((SUBLANES, LANES), BF16)
    nt = (((1,), (1,)), ((), ()))

    def lane_sums(t):
        start = t * qrows if isinstance(t, int) else pl.multiple_of(t * qrows, qrows)
        q = q_ref[pl.ds(start, qrows), :]
        r_ref[pl.ds(t, 1), :] = lax.dot_general(ones, q, nt, preferred_element_type=F32)[0:1]

    q_ref[(tq - 1) * qrows:, :] = jnp.zeros((qrows, LANES), BF16)

    def token(t, carry):
        lane_sums(jnp.where(t == 0, tq - 1, t - 1))
        h = h_ref[t]
        h_first = jnp.where(low, h, 0.0)
        h_second = jnp.where(low, pltpu.roll(h, half_rows, 0), 0.0)
        for g8 in range(groups):
            base = t * slots + g8 * SUBLANES
            ps = []
            for r in range(SUBLANES):
                hi, lo = _expert_halves(tab_ref, off_ref[0, 0, base + r])
                ps.append(hi * h_first + lo * h_second)
            pairs = [ps[i] + pltpu.roll(ps[i + 1], half_rows, 0) for i in range(0, SUBLANES, 2)]
            row0 = pl.multiple_of(base * half_rows, SUBLANES * half_rows)
            q_ref[pl.ds(row0, SUBLANES * half_rows), :] = jnp.concatenate(pairs, axis=0).astype(BF16)
        return carry

    lax.fori_loop(0, tq, token, 0)
    lane_sums(tq - 1)

    fold = (lax.broadcasted_iota(jnp.int32, (qrows, slots), 0) // half_rows
            == lax.broadcasted_iota(jnp.int32, (qrows, slots), 1))
    fold = jnp.where(fold, 1.0, 0.0).astype(BF16)
    rs = r_ref[...]
    rh = rs.astype(BF16)
    rl = (rs - rh.astype(F32)).astype(BF16)
    act = _dot(rh, fold) + _dot(rl, fold)
    w_ref[...] = gate_ref[...] * _gelu(act)


def _smem_block(tq, slots):
    return pl.BlockSpec((1, 1, tq * slots), lambda i: (i, 0, 0), memory_space=pltpu.SMEM)


def _peer_u(off_blocks, hn3, gate, tab):
    n, rows, lanes = hn3.shape
    slots = gate.shape[1]
    tq = off_blocks.shape[-1] // slots
    return pl.pallas_call(
        functools.partial(_peer_u_kernel, tq=tq, slots=slots),
        grid=(n // tq,),
        in_specs=[_smem_block(tq, slots),
                  pl.BlockSpec((tq, rows, lanes), lambda i: (i, 0, 0)),
                  pl.BlockSpec((tq, slots), lambda i: (i, 0)),
                  pl.BlockSpec(tab.shape, lambda i: (0, 0), pipeline_mode=pl.Buffered(1))],
        out_specs=pl.BlockSpec((tq, slots), lambda i: (i, 0)),
        out_shape=_sds((n, slots)),
        scratch_shapes=[pltpu.VMEM((tq * slots * (SUBLANES // 2), lanes), BF16),
                        pltpu.VMEM((tq, slots * (SUBLANES // 2)), F32)],
        compiler_params=_cparams(("parallel",), 52),
        name="peer_u",
    )(off_blocks, hn3, gate, tab)


def _peer_v_kernel(off_ref, w_ref, tab_ref, o_ref, ws_ref, dg_ref, ot_ref, *, tq, slots):
    sub = lax.broadcasted_iota(jnp.int32, (SUBLANES, LANES), 0)
    low = sub < SUBLANES // 2

    eye = lax.broadcasted_iota(jnp.int32, (slots, slots), 0) == lax.broadcasted_iota(jnp.int32, (slots, slots), 1)

    def diag(t, carry):
        wrow = jnp.broadcast_to(w_ref[pl.ds(t, 1), :], (slots, slots))
        dg_ref[pl.ds(pl.multiple_of(t * slots, slots), slots), :] = jnp.where(eye, wrow, 0.0).astype(BF16)
        return carry

    lax.fori_loop(0, tq, diag, 0, unroll=4)
    ws_ref[...] = _dot(dg_ref[...], jnp.ones((slots, LANES), BF16))

    def token(t, carry):
        zero = jnp.zeros((SUBLANES, LANES), F32)
        first, second = [zero, zero], [zero, zero]
        for s in range(slots):
            hi, lo = _expert_halves(tab_ref, off_ref[0, 0, t * slots + s])
            w = jnp.broadcast_to(ws_ref[pl.ds(t * slots + s, 1), :], (SUBLANES, LANES))
            first[s % 2] = first[s % 2] + hi * w
            second[s % 2] = second[s % 2] + lo * w
        tile = jnp.where(low, first[0] + first[1], pltpu.roll(second[0] + second[1], SUBLANES // 2, 0))
        ot_ref[pl.ds(pl.multiple_of(t * SUBLANES, SUBLANES), SUBLANES), :] = tile
        return carry

    lax.fori_loop(0, tq, token, 0)
    for j in range(SUBLANES):
        o_ref[:, j * LANES:(j + 1) * LANES] = ot_ref[pl.ds(j, tq, stride=SUBLANES), :]


def _peer_v(off_blocks, w, tab):
    n, slots = w.shape
    tq = off_blocks.shape[-1] // slots
    lanes = tab.shape[-1]
    assert slots == lanes, "the weight splat uses one (slots, lanes) identity tile"
    return pl.pallas_call(
        functools.partial(_peer_v_kernel, tq=tq, slots=slots),
        grid=(n // tq,),
        in_specs=[_smem_block(tq, slots),
                  pl.BlockSpec((tq, slots), lambda i: (i, 0)),
                  pl.BlockSpec(tab.shape, lambda i: (0, 0), pipeline_mode=pl.Buffered(1))],
        out_specs=pl.BlockSpec((tq, VREG_WORDS), lambda i: (i, 0)),
        out_shape=_sds((n, VREG_WORDS)),
        scratch_shapes=[pltpu.VMEM((tq * slots, lanes), F32), pltpu.VMEM((tq * slots, slots), BF16),
                        pltpu.VMEM((tq * SUBLANES, lanes), F32)],
        compiler_params=_cparams(("parallel",), 52),
        name="peer_v",
    )(off_blocks, w, tab)


def _final_kernel(h1_ref, p_ref, g2_ref, nf_ref, o_ref):
    h = h1_ref[0] + g2_ref[0] * p_ref[0]
    ms = jnp.mean(h * h, axis=-1, keepdims=True)
    o_ref[0] = h * lax.rsqrt(ms + NORM_EPS) * nf_ref[...]


def _final_norm(h1, peer, g2, nfg):
    b, l, d = h1.shape
    tm = min(512, l)
    tok = pl.BlockSpec((1, tm, d), lambda i, j: (i, j, 0))
    return pl.pallas_call(
        _final_kernel,
        grid=(b, l // tm),
        in_specs=[tok, tok, pl.BlockSpec((1, 1, d), lambda i, j: (i, 0, 0)), pl.BlockSpec((1, d), lambda i, j: (0, 0))],
        out_specs=tok,
        out_shape=_sds((b, l, d)),
        compiler_params=_cparams(("parallel", "parallel")),
        name="final_norm",
    )(h1, peer, g2, nfg)


def _pad_cols(a, n):
    return jnp.pad(a, [(0, 0)] * (a.ndim - 1) + [(0, n - a.shape[-1])])


def _to_scan(a, heads):
    *lead, b, t, w = a.shape
    n = w // heads
    a = a.reshape(*lead, b, t, heads, n)
    nl = len(lead)
    perm = tuple(range(nl)) + (nl + 1, nl + 3, nl + 2, nl)
    return a.transpose(perm).reshape(*lead, t, n, heads * b)


def _from_scan(y, heads):
    t, n, c = y.shape
    b = c // heads
    return y.reshape(t, n, heads, b).transpose(3, 0, 2, 1).reshape(b, t, heads * n)


def _layer(h, hc, c, c_ctx, p, norm_f_g):
    b, l, d = h.shape
    lc = hc.shape[1]
    s5w = p['s5_d'].shape[-1]
    rww = p['rw_w0'].shape[-1]
    heads, head = p['rw_r_k'].shape
    n_dec, n_aaa, n_gate = p['rw_w_w2'].shape[1], p['rw_w_a2'].shape[1], p['rw_w_g2'].shape[0]
    n_lora = n_dec + n_aaa + n_gate
    lora_pad = -(-n_lora // LANES) * LANES

    rows = -(-(b + 1) // SUBLANES) * SUBLANES
    cc = jnp.zeros((rows, d), F32).at[:b].set(c).at[b].set(c_ctx)
    mods = _ada_mods(cc, p['w_ada'], p['b_ada'])
    sh1, sc1, g1, sh2, sc2, g2 = [m.reshape(rows, 1, d) for m in jnp.split(mods, 6, axis=-1)]
    lat = lambda m: m[:b]
    ctxm = lambda m: m[b:b + 1]

    w_in = p['w_in']
    w_s5 = w_in[:, :s5w].astype(BF16)
    w_main = w_in[:, s5w:s5w + 3 * rww].astype(BF16)
    w_lora = _pad_cols(w_in[:, s5w + 3 * rww:], lora_pad).astype(BF16)
    n1g = p['norm1_g'].reshape(1, d)
    u_l, zm_l, zl_l = _in_proj(h, lat(sh1), lat(sc1), n1g, w_s5, w_main, w_lora)
    u_c, zm_c, zl_c = _in_proj(hc, ctxm(sh1), ctxm(sc1), n1g, w_s5, w_main, w_lora)

    mu = p['rw_mu']
    mu_main = mu[:3 * rww].reshape(1, -1)
    mu_lora = _pad_cols(mu[3 * rww:], lora_pad).reshape(1, -1)
    place = lambda w, r0: jnp.zeros(w.shape[:-2] + (lora_pad, rww), F32).at[..., r0:r0 + w.shape[-2], :].set(w).astype(BF16)
    rw_params = (place(p['rw_w_g2'], n_dec + n_aaa), place(p['rw_w_w2'], 0), place(p['rw_w_a2'], n_dec),
                 p['rw_w0'].reshape(2, 1, rww), p['rw_a0'].reshape(2, 1, rww),
                 p['rw_k_k'].reshape(1, rww), p['rw_k_a'].reshape(1, rww), p['rw_r_k'].reshape(1, rww))
    prep_params = (mu_main, mu_lora) + rw_params
    ctx_streams = _rwkv_prep(zm_c, zl_c, prep_params, head, lc + l)
    *streams, gate_l, bonus_l = _rwkv_prep(zm_l, zl_l, prep_params, head, lc + l, ctx_streams)
    rr, vv, kk, dec, kd, bb = [_to_scan(a, heads) for a in streams]
    ys = _rwkv_scan((rr, vv, kk), (dec, kd, bb), lc, head)
    rw_y = _from_scan((ys[0] + ys[1])[lc:], heads)

    tables = _s5_tables(p['s5_a_re'], p['s5_a_im'], p['s5_log_dt'], p['s5_b_re'], p['s5_b_im'],
                        p['s5_c_re'], p['s5_c_im'], S5_CHUNK)
    s5_y = _s5_mix(u_c, u_l, p['s5_d'], tables)

    w_out = p['w_out']
    h1, hn2 = _mix_out(s5_y, rw_y, bonus_l, gate_l, h, lat(g1), lat(sh2), lat(sc2), p['norm2_g'].reshape(1, d),
                       p['s5_w_glu'].astype(BF16), p['s5_b_glu'].reshape(1, s5w),
                       p['rw_ln_w'].reshape(1, rww), p['rw_ln_b'].reshape(1, rww),
                       w_out[:s5w].astype(BF16), w_out[s5w:].astype(BF16), head)

    n = b * l
    keys = p['peer_keys']
    pheads, _, nkeys, half = keys.shape
    assert d == VREG_WORDS, "peer_u / peer_v hold one token's features in a single (8, 128) tile"
    off, gate = _peer_topk(hn2.reshape(n, d), p['peer_w_q'].T.astype(BF16),
                           keys.reshape(pheads * 2, nkeys, half).astype(BF16), d // (2 * LANES))
    slots = gate.shape[1]
    tq = min(PEER_TILE, n)
    blocks = lambda a: a.reshape(n // tq, 1, tq * slots)
    w = _peer_u(blocks(off), hn2.reshape(n, SUBLANES, LANES), gate, _pack_table(p['peer_u']))
    peer = _peer_v(blocks(off), w, _pack_table(p['peer_v']))

    return _final_norm(h1, peer.reshape(b, l, d), lat(g2), norm_f_g.reshape(1, d))


def kernel(x, c, ctx, c_ctx, w_ada, b_ada, norm1_g, norm2_g, w_in, s5_a_re, s5_a_im, s5_log_dt, s5_b_re, s5_b_im, s5_c_re, s5_c_im, s5_d, s5_w_glu, s5_b_glu, rw_mu, rw_w0, rw_w_w2, rw_a0, rw_w_a2, rw_w_g2, rw_k_k, rw_k_a, rw_r_k, rw_ln_w, rw_ln_b, w_out, peer_w_q, peer_keys, peer_u, peer_v, norm_f_g):
    assert w_ada.shape[0] == 1, "single-layer block: the context stream is never updated"
    p = dict(w_ada=w_ada[0], b_ada=b_ada[0], norm1_g=norm1_g[0], norm2_g=norm2_g[0], w_in=w_in[0],
             s5_a_re=s5_a_re[0], s5_a_im=s5_a_im[0], s5_log_dt=s5_log_dt[0], s5_b_re=s5_b_re[0], s5_b_im=s5_b_im[0],
             s5_c_re=s5_c_re[0], s5_c_im=s5_c_im[0], s5_d=s5_d[0], s5_w_glu=s5_w_glu[0], s5_b_glu=s5_b_glu[0],
             rw_mu=rw_mu[0], rw_w0=rw_w0[0], rw_w_w2=rw_w_w2[0], rw_a0=rw_a0[0], rw_w_a2=rw_w_a2[0],
             rw_w_g2=rw_w_g2[0], rw_k_k=rw_k_k[0], rw_k_a=rw_k_a[0], rw_r_k=rw_r_k[0], rw_ln_w=rw_ln_w[0],
             rw_ln_b=rw_ln_b[0], w_out=w_out[0], peer_w_q=peer_w_q[0], peer_keys=peer_keys[0],
             peer_u=peer_u[0], peer_v=peer_v[0])
    return _layer(x, ctx, c, c_ctx, p, norm_f_g)
```

```python
import functools
import math

import jax
import jax.numpy as jnp
import numpy as np
from jax import lax
from jax.experimental import pallas as pl
from jax.experimental.pallas import tpu as pltpu

F32 = jnp.float32
BF16 = jnp.bfloat16
HIGHEST = lax.Precision.HIGHEST

SUBLANES = 8
LANES = 128
VREG_WORDS = SUBLANES * LANES

GRID_W = 64
NORM_EPS = 1e-6
GN_EPS = 64e-5
KK_EPS = 1e-12
PEER_TOPK = 16

S5_CHUNK = 32
RWKV_CHUNK = 64
RWKV_ROWS = 2
PEER_TILE = 64


def _dot(a, b):
    return jnp.dot(a, b, preferred_element_type=F32)


def _cparams(sem, vmem_mb=48):
    return pltpu.CompilerParams(dimension_semantics=sem, vmem_limit_bytes=vmem_mb * 1024 * 1024)


def _sds(shape, dtype=F32):
    return jax.ShapeDtypeStruct(shape, dtype)


def _mods_kernel(c_ref, w_ref, b_ref, o_ref):
    c = c_ref[...]
    s = c * jax.nn.sigmoid(c)
    o_ref[...] = jnp.dot(s, w_ref[...], precision=HIGHEST, preferred_element_type=F32) + b_ref[...]


def _ada_mods(cc, w_ada, b_ada):
    rows, d = cc.shape
    n = w_ada.shape[1]
    bn = d if n % d == 0 else n
    return pl.pallas_call(
        _mods_kernel,
        grid=(n // bn,),
        in_specs=[pl.BlockSpec((rows, d), lambda j: (0, 0)),
                  pl.BlockSpec((d, bn), lambda j: (0, j)),
                  pl.BlockSpec((1, bn), lambda j: (0, j))],
        out_specs=pl.BlockSpec((rows, bn), lambda j: (0, j)),
        out_shape=_sds((rows, n)),
        compiler_params=_cparams(("arbitrary",)),
        name="ada_mods",
    )(cc, w_ada, b_ada.reshape(1, n))


def _inproj_kernel(x_ref, sh_ref, sc_ref, g_ref, ws_ref, wz_ref, wl_ref, u_ref, z_ref, l_ref):
    x = x_ref[0]
    ms = jnp.mean(x * x, axis=-1, keepdims=True)
    hn = (x * lax.rsqrt(ms + NORM_EPS) * g_ref[...]) * (1.0 + sc_ref[0]) + sh_ref[0]
    hb = hn.astype(BF16)
    u_ref[0] = _dot(hb, ws_ref[...])
    z_ref[0] = _dot(hb, wz_ref[...])
    l_ref[0] = _dot(hb, wl_ref[...])


def _in_proj(x, shift, scale, gain, w_s5, w_main, w_lora):
    b, l, d = x.shape
    tm = min(512, l)
    bm = shift.shape[0]
    mod_map = (lambda i, j: (i, 0, 0)) if bm > 1 else (lambda i, j: (0, 0, 0))
    full = lambda a: pl.BlockSpec(a.shape, lambda i, j: (0,) * a.ndim)
    return pl.pallas_call(
        _inproj_kernel,
        grid=(b, l // tm),
        in_specs=[pl.BlockSpec((1, tm, d), lambda i, j: (i, j, 0)),
                  pl.BlockSpec((1, 1, d), mod_map), pl.BlockSpec((1, 1, d), mod_map),
                  full(gain), full(w_s5), full(w_main), full(w_lora)],
        out_specs=[pl.BlockSpec((1, tm, w.shape[1]), lambda i, j: (i, j, 0)) for w in (w_s5, w_main, w_lora)],
        out_shape=[_sds((b, l, w.shape[1])) for w in (w_s5, w_main, w_lora)],
        compiler_params=_cparams(("parallel", "parallel")),
        name="in_proj",
    )(x, shift, scale, gain, w_s5, w_main, w_lora)


def _seg_matrix(width, seg):
    r = lax.broadcasted_iota(jnp.int32, (width, width), 0) // seg
    c = lax.broadcasted_iota(jnp.int32, (width, width), 1) // seg
    return jnp.where(r == c, 1.0, 0.0).astype(BF16)


def _seg_sum(x, seg_m):
    hi = x.astype(BF16)
    lo = (x - hi.astype(F32)).astype(BF16)
    return _dot(hi, seg_m) + _dot(lo, seg_m)


def _softplus(x):
    return jnp.maximum(x, 0.0) + jnp.log1p(jnp.exp(-jnp.abs(x)))


def _rwkv_streams(zz, zl, p, head, outs, with_gate):
    (wg_ref, ww_ref, wa_ref, w0_ref, a0_ref, kk_ref, ka_ref, rk_ref) = p
    width = kk_ref.shape[-1]
    r = zz[:, :width]
    k = zz[:, width:2 * width]
    v = zz[:, 2 * width:3 * width]
    seg_m = _seg_matrix(width, head)
    kk = k * kk_ref[...]
    kk = kk * lax.rsqrt(_seg_sum(kk * kk, seg_m) + KK_EPS)
    outs["r"][0] = r
    outs["v"][0] = v
    outs["kk"][0] = kk
    tl = jnp.tanh(zl).astype(BF16)
    zb = zl.astype(BF16)
    bonus = None
    for d in range(2):
        w = -_softplus(-(w0_ref[d] + _dot(tl, ww_ref[d]))) - 0.5
        outs["dec"][d, 0] = jnp.exp(-jnp.exp(w))
        a = jax.nn.sigmoid(a0_ref[d] + _dot(zb, wa_ref[d]))
        kd = k * (1.0 + (a - 1.0) * ka_ref[...])
        outs["kd"][d, 0] = kd
        outs["bb"][d, 0] = kk * a
        if with_gate:
            bd = _seg_sum(r * kd * rk_ref[...], seg_m) * v
            bonus = bd if bonus is None else bonus + bd
    if with_gate:
        outs["bonus"][0] = bonus
        outs["g"][0] = _dot(jax.nn.sigmoid(zl).astype(BF16), wg_ref[...])


def _lerp_shift(z, shifted, mu):
    return z + (shifted - z) * mu


def _grid_shift(zc, zp, zn, has_prev, has_next):
    t, c = zc.shape
    tok = lax.broadcasted_iota(jnp.int32, (t, 1), 0) % GRID_W
    col = lax.broadcasted_iota(jnp.int32, (1, c), 1) % 4
    left = jnp.where(tok == 0, 0.0, pltpu.roll(zc, 1, 0))
    right = jnp.where(tok == GRID_W - 1, 0.0, pltpu.roll(zc, t - 1, 0))
    zp = jnp.where(has_prev, zp, 0.0)
    zn = jnp.where(has_next, zn, 0.0)
    if t > GRID_W:
        up = jnp.concatenate([zp, zc[:t - GRID_W]], axis=0)
        down = jnp.concatenate([zc[GRID_W:], zn], axis=0)
    else:
        up, down = zp, zn
    return jnp.where(col == 0, left, jnp.where(col == 1, right, jnp.where(col == 2, up, down)))


def _seq_shift(zc):
    t, c = zc.shape
    tok = lax.broadcasted_iota(jnp.int32, (t, 1), 0)
    col = lax.broadcasted_iota(jnp.int32, (1, c), 1) % 2
    prev = jnp.where(tok == 0, 0.0, pltpu.roll(zc, 1, 0))
    nxt = jnp.where(tok == t - 1, 0.0, pltpu.roll(zc, t - 1, 0))
    return jnp.where(col == 0, prev, nxt)


_STREAMS = ("r", "v", "kk", "dec", "kd", "bb")


def _rwkv_prep_lat_kernel(zc_ref, zp_ref, zn_ref, lc_ref, lp_ref, ln_ref, mum_ref, mul_ref, *rest, head):
    p, o = rest[:8], rest[8 + len(_STREAMS):]
    outs = dict(zip(_STREAMS + ("g", "bonus"), o))
    j = pl.program_id(1)
    has_prev = j > 0
    has_next = j < pl.num_programs(1) - 1
    zc = zc_ref[0]
    lc = lc_ref[0]
    zz = _lerp_shift(zc, _grid_shift(zc, zp_ref[0], zn_ref[0], has_prev, has_next), mum_ref[...])
    zl = _lerp_shift(lc, _grid_shift(lc, lp_ref[0], ln_ref[0], has_prev, has_next), mul_ref[...])
    _rwkv_streams(zz, zl, p, head, outs, True)


def _rwkv_prep_ctx_kernel(zc_ref, lc_ref, mum_ref, mul_ref, *rest, head):
    p, o = rest[:8], rest[8:]
    outs = dict(zip(_STREAMS, o))
    zc = zc_ref[0]
    lc = lc_ref[0]
    zz = _lerp_shift(zc, _seq_shift(zc), mum_ref[...])
    zl = _lerp_shift(lc, _seq_shift(lc), mul_ref[...])
    _rwkv_streams(zz, zl, p, head, outs, False)


def _rwkv_prep(z_main, z_lora, params, head, l_total, ctx_streams=None):
    b, l, cm = z_main.shape
    cl = z_lora.shape[-1]
    width = cm // 3
    full = lambda a: pl.BlockSpec(a.shape, lambda i, j: (0,) * a.ndim)
    grid_mode = ctx_streams is not None
    if grid_mode:
        l_ctx = l_total - l
        rows_per_tile = next(r for r in (8, 4, 2, 1) if l % (r * GRID_W) == 0 and l_ctx % (r * GRID_W) == 0)
        tt = rows_per_tile * GRID_W
        nt = l // tt
        first = l_ctx // tt
        last_row = l // GRID_W - 1
        cur = lambda c: pl.BlockSpec((1, tt, c), lambda i, j: (i, j, 0))
        prev = lambda c: pl.BlockSpec((1, GRID_W, c), lambda i, j: (i, jnp.maximum(j * rows_per_tile - 1, 0), 0))
        nxt = lambda c: pl.BlockSpec((1, GRID_W, c), lambda i, j: (i, jnp.minimum((j + 1) * rows_per_tile, last_row), 0))
        data = [z_main, z_main, z_main, z_lora, z_lora, z_lora]
        data_specs = [cur(cm), prev(cm), nxt(cm), cur(cl), prev(cl), nxt(cl)]
        kern = functools.partial(_rwkv_prep_lat_kernel, head=head)
        extra = list(ctx_streams)
        extra_specs = [pl.BlockSpec(memory_space=pl.ANY)] * len(extra)
        n_in = len(data) + len(params)
        aliases = {n_in + k: k for k in range(len(extra))}
        lat_only = 2
    else:
        tt, nt, first = l, 1, 0
        data = [z_main, z_lora]
        data_specs = [pl.BlockSpec((1, tt, cm), lambda i, j: (i, j, 0)), pl.BlockSpec((1, tt, cl), lambda i, j: (i, j, 0))]
        kern = functools.partial(_rwkv_prep_ctx_kernel, head=head)
        extra, extra_specs, aliases, lat_only = [], [], {}, 0
    single = pl.BlockSpec((1, tt, width), lambda i, j: (i, first + j, 0))
    double = pl.BlockSpec((2, 1, tt, width), lambda i, j: (0, i, first + j, 0))
    local = pl.BlockSpec((1, tt, width), lambda i, j: (i, j, 0))
    out_specs = [single] * 3 + [double] * 3 + [local] * lat_only
    out_shape = [_sds((b, l_total, width))] * 3 + [_sds((2, b, l_total, width))] * 3 + [_sds((b, l, width))] * lat_only
    return pl.pallas_call(
        kern,
        grid=(b, nt),
        in_specs=data_specs + [full(a) for a in params] + extra_specs,
        out_specs=out_specs,
        out_shape=out_shape,
        input_output_aliases=aliases,
        compiler_params=_cparams(("parallel", "parallel")),
        name="rwkv_prep_lat" if grid_mode else "rwkv_prep_ctx",
    )(*data, *params, *extra)


def _split_bf16(x):
    hi = x.astype(BF16)
    return hi, (x - hi.astype(F32)).astype(BF16)


def _rwkv_chunk_kernel(r_ref, v_ref, kk_ref, w_ref, kd_ref, bb_ref, y_ref, s_ref, *, c, head):
    d = pl.program_id(0)
    nb, _, width = r_ref.shape
    pair = 2 * head
    n_pairs = width // pair

    @pl.when(pl.program_id(2) == 0)
    def _():
        s_ref[...] = jnp.zeros_like(s_ref)

    ri = lax.broadcasted_iota(jnp.int32, (c, c), 0)
    ci = lax.broadcasted_iota(jnp.int32, (c, c), 1)
    before = jnp.where(d == 0, ci - ri, ri - ci) < 0
    incl = before | (ri == ci)
    eye = jnp.where(ri == ci, 1.0, 0.0)
    lmat = jnp.broadcast_to(jnp.where(incl, 1.0, 0.0).astype(BF16), (nb, c, c))

    def bmm(a, b, dims=((2,), (1,))):
        return lax.dot_general(a.astype(BF16), b.astype(BF16), (dims, ((0,), (0,))), preferred_element_type=F32)

    nt = ((2,), (2,))

    ld = jnp.log(w_ref[...])
    ld_hi, ld_lo = _split_bf16(ld)
    cum = bmm(lmat, ld_hi) + bmm(lmat, ld_lo)
    gn = jnp.exp(-cum)
    at = -(jnp.exp(cum - ld) * kk_ref[...])
    rt = jnp.exp(cum) * r_ref[...]
    bt = gn * bb_ref[...]
    kt = gn * kd_ref[...]
    vv = v_ref[...]

    def tiles(x):
        return jnp.stack([x[i, :, p * pair:(p + 1) * pair] for i in range(nb) for p in range(n_pairs)], axis=0)

    def per_head(x, select=None):
        xs = [x, x] if select is None else [jnp.where(select == hh, x, 0.0) for hh in range(2)]
        return jnp.stack(xs, axis=1).reshape((2 * x.shape[0],) + x.shape[1:])

    lane_head = lax.broadcasted_iota(jnp.int32, (1, 1, pair), 2) // head
    at_q, rt_q, bt_q, kt_q, v_q = tiles(at), tiles(rt), tiles(bt), tiles(kt), tiles(vv)
    s0 = s_ref[...]
    as0 = per_head(bmm(at_q, s0))
    rs0 = per_head(bmm(rt_q, s0))
    at_h, rt_h, v_h = per_head(at_q), per_head(rt_q), per_head(v_q)
    bt_h, kt_h = per_head(bt_q, lane_head), per_head(kt_q, lane_head)
    a_ab = jnp.where(before, bmm(at_h, bt_h, nt), 0.0)
    a_ak = jnp.where(before, bmm(at_h, kt_h, nt), 0.0)
    a_rb = jnp.where(incl, bmm(rt_h, bt_h, nt), 0.0)
    a_rk = jnp.where(incl, bmm(rt_h, kt_h, nt), 0.0)
    t = eye + a_ab
    pw = a_ab
    for _ in range(c.bit_length() - 2):
        pw = bmm(pw, pw)
        t = t + bmm(t, pw)
    u = bmm(t, as0 + bmm(a_ak, v_h))
    y = rs0 + bmm(a_rb, u) + bmm(a_rk, v_h)

    def merge(x):
        x = x.reshape((x.shape[0] // 2, 2) + x.shape[1:])
        return jnp.where(lane_head == 0, x[:, 0], x[:, 1])

    u_q, y_q = merge(u), merge(y)
    for i in range(nb):
        for p in range(n_pairs):
            y_ref[i, :, p * pair:(p + 1) * pair] = y_q[i * n_pairs + p]
    tn = ((1,), (1,))
    ones = jnp.ones((nb * n_pairs, c, pair), BF16)
    ds = bmm(bt_q, u_q, tn) + bmm(kt_q, v_q, tn)
    tot = bmm(tiles(ld_hi), ones, tn) + bmm(tiles(ld_lo), ones, tn)
    blk = (lax.broadcasted_iota(jnp.int32, (pair, pair), 0) // head
           == lax.broadcasted_iota(jnp.int32, (pair, pair), 1) // head)
    s_ref[...] = jnp.where(blk, jnp.exp(tot) * (s0 + ds), 0.0)


def _rwkv_chunked(shared, directional, l_ctx, head):
    b, ltot, width = shared[0].shape
    c = RWKV_CHUNK
    nb = RWKV_ROWS if b % RWKV_ROWS == 0 else 1
    nch, ncc = ltot // c, l_ctx // c
    nlc = nch - ncc

    def chunk(d, j):
        back = jnp.where(j < ncc, ncc - 1 - j, nch + ncc - 1 - j)
        return jnp.where(d == 0, j, back)

    def out_chunk(d, j):
        o = chunk(d, j)
        return jnp.where(o >= ncc, o - ncc, jnp.where(d == 0, 0, nlc - 1))

    sh_spec = pl.BlockSpec((nb, c, width), lambda d, i, j: (i, chunk(d, j), 0))
    di_spec = pl.BlockSpec((None, nb, c, width), lambda d, i, j: (d, i, chunk(d, j), 0))
    return pl.pallas_call(
        functools.partial(_rwkv_chunk_kernel, c=c, head=head),
        grid=(2, b // nb, nch),
        in_specs=[sh_spec] * 3 + [di_spec] * 3,
        out_specs=pl.BlockSpec((None, nb, c, width), lambda d, i, j: (d, i, out_chunk(d, j), 0)),
        out_shape=_sds((2, b, (nch - ncc) * c, width)),
        scratch_shapes=[pltpu.VMEM((nb * width // (2 * head), 2 * head, 2 * head), F32)],
        compiler_params=_cparams(("arbitrary", "arbitrary", "arbitrary")),
        name="rwkv_chunk",
    )(*shared, *directional)


def _s5_kernel(x_ref, d_ref, m_ref, pin_ref, pout_ref, ar_ref, ai_ref, y_ref, hall_ref, *, nb, n_chunks, n_ctx_chunks):
    x = x_ref[0]
    xb = x.astype(BF16)
    lat0 = n_ctx_chunks * nb
    xl = xb[lat0:]
    y = d_ref[0] * x[lat0:]
    half = ar_ref.shape[-1] // 2
    fwd = list(range(n_chunks))
    bwd = list(range(n_ctx_chunks - 1, -1, -1)) + list(range(n_chunks - 1, n_ctx_chunks - 1, -1))
    for d, order in enumerate((fwd, bwd)):
        v = _dot(xb, pin_ref[d, 0])
        ar = ar_ref[d, 0]
        ai = ai_ref[d, 0]
        h = jnp.zeros((nb, 2 * half), F32)
        for c in order:
            hall_ref[c * nb:(c + 1) * nb, :] = h
            h = h * ar + pltpu.roll(h, half, 1) * ai + v[c * nb:(c + 1) * nb]
        y = y + _dot(xl, m_ref[d, 0]) + _dot(hall_ref[lat0:, :].astype(BF16), pout_ref[d, 0])
    y_ref[0] = y


def _s5_tables(a_re, a_im, log_dt, b_re, b_im, c_re, c_im, tc):
    a_re, a_im, b_re, b_im = (t.astype(F32) for t in (a_re, a_im, b_re, b_im))
    c_re, c_im = c_re.astype(F32), c_im.astype(F32)
    dt = jnp.exp(log_dt.astype(F32))[..., None]
    mag = jnp.exp(dt * a_re)
    ab_re, ab_im = mag * jnp.cos(dt * a_im), mag * jnp.sin(dt * a_im)
    nr, ni = ab_re - 1.0, ab_im
    den = a_re * a_re + a_im * a_im
    cf_re = (nr * a_re + ni * a_im) / den
    cf_im = (ni * a_re - nr * a_im) / den
    bb_re = cf_re[..., None] * b_re - cf_im[..., None] * b_im
    bb_im = cf_re[..., None] * b_im + cf_im[..., None] * b_re

    k = jnp.arange(tc + 1, dtype=F32)[:, None, None, None]
    pmag = jnp.exp(k * (dt * a_re)[None])
    pang = k * (dt * a_im)[None]
    pw_re, pw_im = pmag * jnp.cos(pang), pmag * jnp.sin(pang)

    ab_k_re = pw_re[..., None] * bb_re[None] - pw_im[..., None] * bb_im[None]
    ab_k_im = pw_re[..., None] * bb_im[None] + pw_im[..., None] * bb_re[None]
    ca_k_re = c_re[None] * pw_re[:, :, :, None, :] - c_im[None] * pw_im[:, :, :, None, :]
    ca_k_im = c_re[None] * pw_im[:, :, :, None, :] + c_im[None] * pw_re[:, :, :, None, :]

    taps = (jnp.einsum('dghp,kdgpj->kdghj', c_re, ab_k_re[:tc], precision=HIGHEST)
            - jnp.einsum('dghp,kdgpj->kdghj', c_im, ab_k_im[:tc], precision=HIGHEST))
    s_idx = jnp.arange(tc)[:, None]
    t_idx = jnp.arange(tc)[None, :]
    hc = b_re.shape[-1]
    g = a_re.shape[1]
    p = a_re.shape[2]

    def toeplitz(tp, lag, valid):
        m = jnp.where(valid[:, :, None, None, None], tp[jnp.clip(lag, 0, tc - 1)], 0.0)
        return m.transpose(2, 0, 4, 1, 3).reshape(g, tc * hc, tc * hc)

    m_f = toeplitz(taps[:, 0], t_idx - s_idx, t_idx >= s_idx)
    m_b = toeplitz(taps[:, 1], s_idx - t_idx, s_idx >= t_idx)
    m = jnp.stack([m_f, m_b]).astype(BF16)

    def pin_of(re, im):
        both = jnp.concatenate([re, im], axis=2)
        return both.transpose(1, 0, 3, 2).reshape(g, tc * hc, 2 * p)

    pin = jnp.stack([pin_of(ab_k_re[:tc, 0][::-1], ab_k_im[:tc, 0][::-1]),
                     pin_of(ab_k_re[:tc, 1], ab_k_im[:tc, 1])]).astype(BF16)

    def pout_of(re, im):
        both = jnp.concatenate([re, -im], axis=3)
        return both.transpose(1, 3, 0, 2).reshape(g, 2 * p, tc * hc)

    pout = jnp.stack([pout_of(ca_k_re[1:tc + 1, 0], ca_k_im[1:tc + 1, 0]),
                      pout_of(ca_k_re[1:tc + 1, 1][::-1], ca_k_im[1:tc + 1, 1][::-1])]).astype(BF16)

    at_re, at_im = pw_re[tc], pw_im[tc]
    ar = jnp.concatenate([at_re, at_re], axis=-1)[:, :, None, :]
    ai = jnp.concatenate([-at_im, at_im], axis=-1)[:, :, None, :]
    return m, pin, pout, ar, ai


def _s5_mix(u_ctx, u_lat, s5_d, tables):
    m, pin, pout, ar, ai = tables
    b, lc, width = u_ctx.shape
    l = u_lat.shape[1]
    g = m.shape[1]
    hc = width // g
    tc = S5_CHUNK
    w = tc * hc
    ncc, nlc = lc // tc, l // tc
    nch = ncc + nlc
    u = jnp.concatenate([u_ctx, u_lat], axis=1)
    xg = u.reshape(b, nch, tc, g, hc).transpose(3, 1, 0, 2, 4).reshape(g, nch * b, w)
    dg = jnp.tile(s5_d.reshape(g, 1, hc), (1, tc, 1)).reshape(g, 1, w)
    pgrp = lambda a: pl.BlockSpec((2, 1) + a.shape[2:], lambda i: (0, i, 0, 0))
    y = pl.pallas_call(
        functools.partial(_s5_kernel, nb=b, n_chunks=nch, n_ctx_chunks=ncc),
        grid=(g,),
        in_specs=[pl.BlockSpec((1, nch * b, w), lambda i: (i, 0, 0)),
                  pl.BlockSpec((1, 1, w), lambda i: (i, 0, 0)),
                  pgrp(m), pgrp(pin), pgrp(pout), pgrp(ar), pgrp(ai)],
        out_specs=pl.BlockSpec((1, nlc * b, w), lambda i: (i, 0, 0)),
        out_shape=_sds((g, nlc * b, w)),
        scratch_shapes=[pltpu.VMEM((nch * b, pin.shape[-1]), F32)],
        compiler_params=_cparams(("parallel",), 56),
        name="s5_mix",
    )(xg, dg, m, pin, pout, ar, ai)
    return y.reshape(g, nlc, b, tc, hc).transpose(2, 1, 3, 0, 4).reshape(b, l, width)


def _gelu(x):
    return 0.5 * x * (1.0 + lax.erf(x * (1.0 / math.sqrt(2.0))))


def _mix_out_kernel(s5_ref, rw_ref, bon_ref, g_ref, x_ref, g1_ref, sh2_ref, sc2_ref, n2_ref,
                    wglu_ref, bglu_ref, lnw_ref, lnb_ref, wos_ref, wor_ref, h1_ref, hn_ref, *, head):
    y1 = _gelu(s5_ref[0])
    s5o = y1 * jax.nn.sigmoid(_dot(y1.astype(BF16), wglu_ref[...]) + bglu_ref[...])
    y = rw_ref[0, 0] + rw_ref[1, 0]
    seg_m = _seg_matrix(y.shape[-1], head)
    inv = 1.0 / head
    mu = _seg_sum(y, seg_m) * inv
    yc = y - mu
    var = _seg_sum(yc * yc, seg_m) * inv
    yn = yc * lax.rsqrt(var + GN_EPS) * lnw_ref[...] + lnb_ref[...]
    rwo = (yn + bon_ref[0]) * g_ref[0]
    mix = _dot(s5o.astype(BF16), wos_ref[...]) + _dot(rwo.astype(BF16), wor_ref[...])
    h1 = x_ref[0] + g1_ref[0] * mix
    h1_ref[0] = h1
    ms = jnp.mean(h1 * h1, axis=-1, keepdims=True)
    hn_ref[0] = (h1 * lax.rsqrt(ms + NORM_EPS) * n2_ref[...]) * (1.0 + sc2_ref[0]) + sh2_ref[0]


def _mix_out(s5y, rwy, bonus, gate, x, g1, sh2, sc2, n2g, w_glu, b_glu, ln_w, ln_b, w_out_s, w_out_r, head):
    b, l, d = x.shape
    width = s5y.shape[-1]
    tm = min(512, l)
    tok = lambda c: pl.BlockSpec((1, tm, c), lambda i, j: (i, j, 0))
    mod = pl.BlockSpec((1, 1, d), lambda i, j: (i, 0, 0))
    full = lambda a: pl.BlockSpec(a.shape, lambda i, j: (0,) * a.ndim)
    consts = (n2g, w_glu, b_glu, ln_w, ln_b, w_out_s, w_out_r)
    return pl.pallas_call(
        functools.partial(_mix_out_kernel, head=head),
        grid=(b, l // tm),
        in_specs=[tok(width), pl.BlockSpec((2, 1, tm, width), lambda i, j: (0, i, j, 0)), tok(width), tok(width)]
        + [tok(d), mod, mod, mod] + [full(a) for a in consts],
        out_specs=[tok(d), tok(d)],
        out_shape=[_sds((b, l, d)), _sds((b, l, d))],
        compiler_params=_cparams(("parallel", "parallel")),
        name="mix_out",
    )(s5y, rwy, bonus, gate, x, g1, sh2, sc2, *consts)


def _staircase(k):
    return [(a, b) for a in range(k) for b in range(k) if (a + 1) * (b + 1) <= k]


def _topk_rows(s, k):
    rows = s.shape[0]
    iota = lax.broadcasted_iota(jnp.int32, s.shape, 0)
    vals, idxs = [], []
    for _ in range(k):
        m = jnp.max(s, axis=0, keepdims=True)
        idx = jnp.min(jnp.where(s == m, iota, rows), axis=0, keepdims=True)
        vals.append(m)
        idxs.append(idx)
        s = jnp.where(iota == idx, -jnp.inf, s)
    return jnp.concatenate(vals, axis=0), jnp.concatenate(idxs, axis=0).astype(F32)


def _peer_topk_kernel(hn_ref, wq_ref, keys_ref, sel1_ref, sel2_ref, pad_ref, off_ref, gate_ref, *, heads, nkeys, half,
                      rows_per_expert):
    hb = hn_ref[...].astype(BF16)
    nt = (((1,), (1,)), ((), ()))
    sel1 = sel1_ref[...]
    sel2 = sel2_ref[...]
    pick = lambda sel, a: jnp.dot(sel, a, precision=HIGHEST, preferred_element_type=F32)
    ids, gates = [], []
    for h in range(heads):
        top = []
        for c in range(2):
            row0 = (h * 2 + c) * half
            qt = lax.dot_general(wq_ref[row0:row0 + half, :], hb, nt, preferred_element_type=F32)
            st = _dot(keys_ref[h * 2 + c], qt.astype(BF16))
            top.append(_topk_rows(st, PEER_TOPK))
        (s1, i1), (s2, i2) = top
        cand = pick(sel1, s1) + pick(sel2, s2) + pad_ref[...]
        cid = pick(sel1, i1) * nkeys + pick(sel2, i2)
        best, pos = _topk_rows(cand, PEER_TOPK)
        riota = lax.broadcasted_iota(jnp.int32, cid.shape, 0).astype(F32)
        eid = jnp.concatenate(
            [jnp.sum(jnp.where(riota == pos[r:r + 1], cid, 0.0), axis=0, keepdims=True) for r in range(PEER_TOPK)], axis=0)
        e = jnp.exp(best - jnp.max(best, axis=0, keepdims=True))
        gates.append(e / jnp.sum(e, axis=0, keepdims=True))
        ids.append(eid)
    eid = jnp.concatenate(ids, axis=0).T.astype(jnp.int32)
    off_ref[...] = eid * rows_per_expert
    gate_ref[...] = jnp.concatenate(gates, axis=0).T


def _peer_topk(hn, wq_t, keys, rows_per_expert):
    n, d = hn.shape
    heads = keys.shape[0] // 2
    nkeys, half = keys.shape[1], keys.shape[2]
    tm = min(256, n)
    pairs = _staircase(PEER_TOPK)
    rows = -(-len(pairs) // SUBLANES) * SUBLANES
    sel1 = np.zeros((rows, PEER_TOPK), np.float32)
    sel2 = np.zeros((rows, PEER_TOPK), np.float32)
    pad = np.zeros((rows, 1), np.float32)
    for i, (a, b) in enumerate(pairs):
        sel1[i, a] = 1.0
        sel2[i, b] = 1.0
    pad[len(pairs):] = -np.inf
    slots = heads * PEER_TOPK
    full = lambda a: pl.BlockSpec(a.shape, lambda i: (0,) * a.ndim)
    consts = (wq_t, keys, jnp.asarray(sel1), jnp.asarray(sel2), jnp.asarray(pad))
    return pl.pallas_call(
        functools.partial(_peer_topk_kernel, heads=heads, nkeys=nkeys, half=half, rows_per_expert=rows_per_expert),
        grid=(n // tm,),
        in_specs=[pl.BlockSpec((tm, d), lambda i: (i, 0))] + [full(a) for a in consts],
        out_specs=[pl.BlockSpec((tm, slots), lambda i: (i, 0))] * 2,
        out_shape=[_sds((n, slots), jnp.int32), _sds((n, slots))],
        compiler_params=_cparams(("parallel",)),
        name="peer_topk",
    )(hn, *consts)


def _pack_table(tab):
    e, d = tab.shape
    half = d // 2
    bits = lax.bitcast_convert_type(tab.astype(BF16), jnp.uint16).astype(jnp.uint32)
    packed = ((bits[:, :half] << 16) | bits[:, half:]).reshape(e * (half // LANES), LANES)
    return jnp.pad(packed, ((0, SUBLANES), (0, 0)))


def _expert_halves(tab_ref, off):
    bits = tab_ref[pl.ds(off, SUBLANES), :]
    return pltpu.bitcast(bits & jnp.uint32(0xFFFF0000), F32), pltpu.bitcast(bits << 16, F32)


def _peer_u_kernel(off_ref, h_ref, gate_ref, tab_ref, w_ref, q_ref, r_ref, *, tq, slots):
    half_rows = SUBLANES // 2
    sub = lax.broadcasted_iota(jnp.int32, (SUBLANES, LANES), 0)
    low = sub < half_rows
    groups = slots // SUBLANES
    qrows = slots * half_rows

    ones = jnp.ones((SUBLANES, LANES), BF16)
    nt = (((1,), (1,)), ((), ()))

    def lane_sums(t):
        start = t * qrows if isinstance(t, int) else pl.multiple_of(t * qrows, qrows)
        q = q_ref[pl.ds(start, qrows), :]
        r_ref[pl.ds(t, 1), :] = lax.dot_general(ones, q, nt, preferred_element_type=F32)[0:1]

    q_ref[(tq - 1) * qrows:, :] = jnp.zeros((qrows, LANES), BF16)

    def token(t, carry):
        lane_sums(jnp.where(t == 0, tq - 1, t - 1))
        h = h_ref[t]
        h_first = jnp.where(low, h, 0.0)
        h_second = jnp.where(low, pltpu.roll(h, half_rows, 0), 0.0)
        q_tok = q_ref.at[pl.ds(pl.multiple_of(t * qrows, qrows), qrows)]
        grows = SUBLANES * half_rows
        for g8 in range(groups):
            base = t * slots + g8 * SUBLANES
            ps = []
            for r in range(SUBLANES):
                hi, lo = _expert_halves(tab_ref, off_ref[0, 0, base + r])
                ps.append(hi * h_first + lo * h_second)
            pairs = [ps[i] + pltpu.roll(ps[i + 1], half_rows, 0) for i in range(0, SUBLANES, 2)]
            q_tok[g8 * grows:(g8 + 1) * grows, :] = jnp.concatenate(pairs, axis=0).astype(BF16)
        return carry

    lax.fori_loop(0, tq, token, 0)
    lane_sums(tq - 1)

    fold = (lax.broadcasted_iota(jnp.int32, (qrows, slots), 0) // half_rows
            == lax.broadcasted_iota(jnp.int32, (qrows, slots), 1))
    fold = jnp.where(fold, 1.0, 0.0).astype(BF16)
    rs = r_ref[...]
    rh = rs.astype(BF16)
    rl = (rs - rh.astype(F32)).astype(BF16)
    act = _dot(rh, fold) + _dot(rl, fold)
    w_ref[...] = gate_ref[...] * _gelu(act)


def _smem_block(tq, slots):
    return pl.BlockSpec((1, 1, tq * slots), lambda i: (i, 0, 0), memory_space=pltpu.SMEM)


def _peer_u(off_blocks, hn3, gate, tab):
    n, rows, lanes = hn3.shape
    slots = gate.shape[1]
    tq = off_blocks.shape[-1] // slots
    return pl.pallas_call(
        functools.partial(_peer_u_kernel, tq=tq, slots=slots),
        grid=(n // tq,),
        in_specs=[_smem_block(tq, slots),
                  pl.BlockSpec((tq, rows, lanes), lambda i: (i, 0, 0)),
                  pl.BlockSpec((tq, slots), lambda i: (i, 0)),
                  pl.BlockSpec(tab.shape, lambda i: (0, 0), pipeline_mode=pl.Buffered(1))],
        out_specs=pl.BlockSpec((tq, slots), lambda i: (i, 0)),
        out_shape=_sds((n, slots)),
        scratch_shapes=[pltpu.VMEM((tq * slots * (SUBLANES // 2), lanes), BF16),
                        pltpu.VMEM((tq, slots * (SUBLANES // 2)), F32)],
        compiler_params=_cparams(("parallel",), 52),
        name="peer_u",
    )(off_blocks, hn3, gate, tab)


def _peer_v_kernel(off_ref, w_ref, tab_ref, o_ref, ws_ref, dg_ref, ot_ref, *, tq, slots):
    sub = lax.broadcasted_iota(jnp.int32, (SUBLANES, LANES), 0)
    low = sub < SUBLANES // 2

    eye = lax.broadcasted_iota(jnp.int32, (slots, slots), 0) == lax.broadcasted_iota(jnp.int32, (slots, slots), 1)

    def diag(t, carry):
        wrow = jnp.broadcast_to(w_ref[pl.ds(t, 1), :], (slots, slots))
        dg_ref[pl.ds(pl.multiple_of(t * slots, slots), slots), :] = jnp.where(eye, wrow, 0.0).astype(BF16)
        return carry

    lax.fori_loop(0, tq, diag, 0, unroll=4)
    ws_ref[...] = _dot(dg_ref[...], jnp.ones((slots, LANES), BF16))

    def token(t, carry):
        zero = jnp.zeros((SUBLANES, LANES), F32)
        first, second = [zero, zero], [zero, zero]
        for s in range(slots):
            hi, lo = _expert_halves(tab_ref, off_ref[0, 0, t * slots + s])
            w = jnp.broadcast_to(ws_ref[pl.ds(t * slots + s, 1), :], (SUBLANES, LANES))
            first[s % 2] = first[s % 2] + hi * w
            second[s % 2] = second[s % 2] + lo * w
        tile = jnp.where(low, first[0] + first[1], pltpu.roll(second[0] + second[1], SUBLANES // 2, 0))
        ot_ref[pl.ds(pl.multiple_of(t * SUBLANES, SUBLANES), SUBLANES), :] = tile
        return carry

    lax.fori_loop(0, tq, token, 0)
    for j in range(SUBLANES):
        o_ref[:, j * LANES:(j + 1) * LANES] = ot_ref[pl.ds(j, tq, stride=SUBLANES), :]


def _peer_v(off_blocks, w, tab):
    n, slots = w.shape
    tq = off_blocks.shape[-1] // slots
    lanes = tab.shape[-1]
    assert slots == lanes, "the weight splat uses one (slots, lanes) identity tile"
    return pl.pallas_call(
        functools.partial(_peer_v_kernel, tq=tq, slots=slots),
        grid=(n // tq,),
        in_specs=[_smem_block(tq, slots),
                  pl.BlockSpec((tq, slots), lambda i: (i, 0)),
                  pl.BlockSpec(tab.shape, lambda i: (0, 0), pipeline_mode=pl.Buffered(1))],
        out_specs=pl.BlockSpec((tq, VREG_WORDS), lambda i: (i, 0)),
        out_shape=_sds((n, VREG_WORDS)),
        scratch_shapes=[pltpu.VMEM((tq * slots, lanes), F32), pltpu.VMEM((tq * slots, slots), BF16),
                        pltpu.VMEM((tq * SUBLANES, lanes), F32)],
        compiler_params=_cparams(("parallel",), 52),
        name="peer_v",
    )(off_blocks, w, tab)


def _final_kernel(h1_ref, p_ref, g2_ref, nf_ref, o_ref):
    h = h1_ref[0] + g2_ref[0] * p_ref[0]
    ms = jnp.mean(h * h, axis=-1, keepdims=True)
    o_ref[0] = h * lax.rsqrt(ms + NORM_EPS) * nf_ref[...]


def _final_norm(h1, peer, g2, nfg):
    b, l, d = h1.shape
    tm = min(512, l)
    tok = pl.BlockSpec((1, tm, d), lambda i, j: (i, j, 0))
    return pl.pallas_call(
        _final_kernel,
        grid=(b, l // tm),
        in_specs=[tok, tok, pl.BlockSpec((1, 1, d), lambda i, j: (i, 0, 0)), pl.BlockSpec((1, d), lambda i, j: (0, 0))],
        out_specs=tok,
        out_shape=_sds((b, l, d)),
        compiler_params=_cparams(("parallel", "parallel")),
        name="final_norm",
    )(h1, peer, g2, nfg)


def _pad_cols(a, n):
    return jnp.pad(a, [(0, 0)] * (a.ndim - 1) + [(0, n - a.shape[-1])])


def _layer(h, hc, c, c_ctx, p, norm_f_g):
    b, l, d = h.shape
    lc = hc.shape[1]
    s5w = p['s5_d'].shape[-1]
    rww = p['rw_w0'].shape[-1]
    heads, head = p['rw_r_k'].shape
    n_dec, n_aaa, n_gate = p['rw_w_w2'].shape[1], p['rw_w_a2'].shape[1], p['rw_w_g2'].shape[0]
    n_lora = n_dec + n_aaa + n_gate
    lora_pad = -(-n_lora // LANES) * LANES

    rows = -(-(b + 1) // SUBLANES) * SUBLANES
    cc = jnp.zeros((rows, d), F32).at[:b].set(c).at[b].set(c_ctx)
    mods = _ada_mods(cc, p['w_ada'], p['b_ada'])
    sh1, sc1, g1, sh2, sc2, g2 = [m.reshape(rows, 1, d) for m in jnp.split(mods, 6, axis=-1)]
    lat = lambda m: m[:b]
    ctxm = lambda m: m[b:b + 1]

    w_in = p['w_in']
    w_s5 = w_in[:, :s5w].astype(BF16)
    w_main = w_in[:, s5w:s5w + 3 * rww].astype(BF16)
    w_lora = _pad_cols(w_in[:, s5w + 3 * rww:], lora_pad).astype(BF16)
    n1g = p['norm1_g'].reshape(1, d)
    u_l, zm_l, zl_l = _in_proj(h, lat(sh1), lat(sc1), n1g, w_s5, w_main, w_lora)
    u_c, zm_c, zl_c = _in_proj(hc, ctxm(sh1), ctxm(sc1), n1g, w_s5, w_main, w_lora)

    mu = p['rw_mu']
    mu_main = mu[:3 * rww].reshape(1, -1)
    mu_lora = _pad_cols(mu[3 * rww:], lora_pad).reshape(1, -1)
    place = lambda w, r0: jnp.zeros(w.shape[:-2] + (lora_pad, rww), F32).at[..., r0:r0 + w.shape[-2], :].set(w).astype(BF16)
    rw_params = (place(p['rw_w_g2'], n_dec + n_aaa), place(p['rw_w_w2'], 0), place(p['rw_w_a2'], n_dec),
                 p['rw_w0'].reshape(2, 1, rww), p['rw_a0'].reshape(2, 1, rww),
                 p['rw_k_k'].reshape(1, rww), p['rw_k_a'].reshape(1, rww), p['rw_r_k'].reshape(1, rww))
    prep_params = (mu_main, mu_lora) + rw_params
    ctx_streams = _rwkv_prep(zm_c, zl_c, prep_params, head, lc + l)
    *streams, gate_l, bonus_l = _rwkv_prep(zm_l, zl_l, prep_params, head, lc + l, ctx_streams)
    rw_y = _rwkv_chunked(tuple(streams[:3]), tuple(streams[3:]), lc, head)

    tables = _s5_tables(p['s5_a_re'], p['s5_a_im'], p['s5_log_dt'], p['s5_b_re'], p['s5_b_im'],
                        p['s5_c_re'], p['s5_c_im'], S5_CHUNK)
    s5_y = _s5_mix(u_c, u_l, p['s5_d'], tables)

    w_out = p['w_out']
    h1, hn2 = _mix_out(s5_y, rw_y, bonus_l, gate_l, h, lat(g1), lat(sh2), lat(sc2), p['norm2_g'].reshape(1, d),
                       p['s5_w_glu'].astype(BF16), p['s5_b_glu'].reshape(1, s5w),
                       p['rw_ln_w'].reshape(1, rww), p['rw_ln_b'].reshape(1, rww),
                       w_out[:s5w].astype(BF16), w_out[s5w:].astype(BF16), head)

    n = b * l
    keys = p['peer_keys']
    pheads, _, nkeys, half = keys.shape
    assert d == VREG_WORDS, "peer_u / peer_v hold one token's features in a single (8, 128) tile"
    off, gate = _peer_topk(hn2.reshape(n, d), p['peer_w_q'].T.astype(BF16),
                           keys.reshape(pheads * 2, nkeys, half).astype(BF16), d // (2 * LANES))
    slots = gate.shape[1]
    tq = min(PEER_TILE, n)
    blocks = lambda a: a.reshape(n // tq, 1, tq * slots)
    w = _peer_u(blocks(off), hn2.reshape(n, SUBLANES, LANES), gate, _pack_table(p['peer_u']))
    peer = _peer_v(blocks(off), w, _pack_table(p['peer_v']))

    return _final_norm(h1, peer.reshape(b, l, d), lat(g2), norm_f_g.reshape(1, d))


def kernel(x, c, ctx, c_ctx, w_ada, b_ada, norm1_g, norm2_g, w_in, s5_a_re, s5_a_im, s5_log_dt, s5_b_re, s5_b_im, s5_c_re, s5_c_im, s5_d, s5_w_glu, s5_b_glu, rw_mu, rw_w0, rw_w_w2, rw_a0, rw_w_a2, rw_w_g2, rw_k_k, rw_k_a, rw_r_k, rw_ln_w, rw_ln_b, w_out, peer_w_q, peer_keys, peer_u, peer_v, norm_f_g):
    assert w_ada.shape[0] == 1, "single-layer block: the context stream is never updated"
    p = dict(w_ada=w_ada[0], b_ada=b_ada[0], norm1_g=norm1_g[0], norm2_g=norm2_g[0], w_in=w_in[0],
             s5_a_re=s5_a_re[0], s5_a_im=s5_a_im[0], s5_log_dt=s5_log_dt[0], s5_b_re=s5_b_re[0], s5_b_im=s5_b_im[0],
             s5_c_re=s5_c_re[0], s5_c_im=s5_c_im[0], s5_d=s5_d[0], s5_w_glu=s5_w_glu[0], s5_b_glu=s5_b_glu[0],
             rw_mu=rw_mu[0], rw_w0=rw_w0[0], rw_w_w2=rw_w_w2[0], rw_a0=rw_a0[0], rw_w_a2=rw_w_a2[0],
             rw_w_g2=rw_w_g2[0], rw_k_k=rw_k_k[0], rw_k_a=rw_k_a[0], rw_r_k=rw_r_k[0], rw_ln_w=rw_ln_w[0],
             rw_ln_b=rw_ln_b[0], w_out=w_out[0], peer_w_q=peer_w_q[0], peer_keys=peer_keys[0],
             peer_u=peer_u[0], peer_v=peer_v[0])
    return _layer(x, ctx, c, c_ctx, p, norm_f_g)
```

```python
import functools
import math

import jax
import jax.numpy as jnp
import numpy as np
from jax import lax
from jax.experimental import pallas as pl
from jax.experimental.pallas import tpu as pltpu

F32 = jnp.float32
BF16 = jnp.bfloat16
HIGHEST = lax.Precision.HIGHEST

SUBLANES = 8
LANES = 128
VREG_WORDS = SUBLANES * LANES

GRID_W = 64
NORM_EPS = 1e-6
GN_EPS = 64e-5
KK_EPS = 1e-12
PEER_TOPK = 16

S5_CHUNK = 32
RWKV_CHUNK = 64
RWKV_ROWS = 4
PEER_TILE = 64


def _dot(a, b):
    return jnp.dot(a, b, preferred_element_type=F32)


def _cparams(sem, vmem_mb=48):
    return pltpu.CompilerParams(dimension_semantics=sem, vmem_limit_bytes=vmem_mb * 1024 * 1024)


def _sds(shape, dtype=F32):
    return jax.ShapeDtypeStruct(shape, dtype)


def _mods_kernel(c_ref, w_ref, b_ref, o_ref):
    c = c_ref[...]
    s = c * jax.nn.sigmoid(c)
    o_ref[...] = jnp.dot(s, w_ref[...], precision=HIGHEST, preferred_element_type=F32) + b_ref[...]


def _ada_mods(cc, w_ada, b_ada):
    rows, d = cc.shape
    n = w_ada.shape[1]
    bn = d if n % d == 0 else n
    return pl.pallas_call(
        _mods_kernel,
        grid=(n // bn,),
        in_specs=[pl.BlockSpec((rows, d), lambda j: (0, 0)),
                  pl.BlockSpec((d, bn), lambda j: (0, j)),
                  pl.BlockSpec((1, bn), lambda j: (0, j))],
        out_specs=pl.BlockSpec((rows, bn), lambda j: (0, j)),
        out_shape=_sds((rows, n)),
        compiler_params=_cparams(("arbitrary",)),
        name="ada_mods",
    )(cc, w_ada, b_ada.reshape(1, n))


def _inproj_kernel(x_ref, sh_ref, sc_ref, g_ref, ws_ref, wz_ref, wl_ref, u_ref, z_ref, l_ref):
    x = x_ref[0]
    ms = jnp.mean(x * x, axis=-1, keepdims=True)
    hn = (x * lax.rsqrt(ms + NORM_EPS) * g_ref[...]) * (1.0 + sc_ref[0]) + sh_ref[0]
    hb = hn.astype(BF16)
    u_ref[0] = _dot(hb, ws_ref[...])
    z_ref[0] = _dot(hb, wz_ref[...])
    l_ref[0] = _dot(hb, wl_ref[...])


def _in_proj(x, shift, scale, gain, w_s5, w_main, w_lora):
    b, l, d = x.shape
    tm = min(512, l)
    bm = shift.shape[0]
    mod_map = (lambda i, j: (i, 0, 0)) if bm > 1 else (lambda i, j: (0, 0, 0))
    full = lambda a: pl.BlockSpec(a.shape, lambda i, j: (0,) * a.ndim)
    return pl.pallas_call(
        _inproj_kernel,
        grid=(b, l // tm),
        in_specs=[pl.BlockSpec((1, tm, d), lambda i, j: (i, j, 0)),
                  pl.BlockSpec((1, 1, d), mod_map), pl.BlockSpec((1, 1, d), mod_map),
                  full(gain), full(w_s5), full(w_main), full(w_lora)],
        out_specs=[pl.BlockSpec((1, tm, w.shape[1]), lambda i, j: (i, j, 0)) for w in (w_s5, w_main, w_lora)],
        out_shape=[_sds((b, l, w.shape[1])) for w in (w_s5, w_main, w_lora)],
        compiler_params=_cparams(("parallel", "parallel")),
        name="in_proj",
    )(x, shift, scale, gain, w_s5, w_main, w_lora)


def _seg_matrix(width, seg):
    r = lax.broadcasted_iota(jnp.int32, (width, width), 0) // seg
    c = lax.broadcasted_iota(jnp.int32, (width, width), 1) // seg
    return jnp.where(r == c, 1.0, 0.0).astype(BF16)


def _seg_sum(x, seg_m):
    hi = x.astype(BF16)
    lo = (x - hi.astype(F32)).astype(BF16)
    return _dot(hi, seg_m) + _dot(lo, seg_m)


def _softplus(x):
    return jnp.maximum(x, 0.0) + jnp.log1p(jnp.exp(-jnp.abs(x)))


def _rwkv_streams(zz, zl, p, head, outs, with_gate):
    (wg_ref, ww_ref, wa_ref, w0_ref, a0_ref, kk_ref, ka_ref, rk_ref) = p
    width = kk_ref.shape[-1]
    r = zz[:, :width]
    k = zz[:, width:2 * width]
    v = zz[:, 2 * width:3 * width]
    seg_m = _seg_matrix(width, head)
    kk = k * kk_ref[...]
    kk = kk * lax.rsqrt(_seg_sum(kk * kk, seg_m) + KK_EPS)
    outs["r"][0] = r
    outs["v"][0] = v
    outs["kk"][0] = kk
    tl = jnp.tanh(zl).astype(BF16)
    zb = zl.astype(BF16)
    bonus = None
    for d in range(2):
        w = -_softplus(-(w0_ref[d] + _dot(tl, ww_ref[d]))) - 0.5
        outs["dec"][d, 0] = jnp.exp(-jnp.exp(w))
        a = jax.nn.sigmoid(a0_ref[d] + _dot(zb, wa_ref[d]))
        kd = k * (1.0 + (a - 1.0) * ka_ref[...])
        outs["kd"][d, 0] = kd
        outs["bb"][d, 0] = kk * a
        if with_gate:
            bd = _seg_sum(r * kd * rk_ref[...], seg_m) * v
            bonus = bd if bonus is None else bonus + bd
    if with_gate:
        outs["bonus"][0] = bonus
        outs["g"][0] = _dot(jax.nn.sigmoid(zl).astype(BF16), wg_ref[...])


def _lerp_shift(z, shifted, mu):
    return z + (shifted - z) * mu


def _grid_shift(zc, zp, zn, has_prev, has_next):
    t, c = zc.shape
    tok = lax.broadcasted_iota(jnp.int32, (t, 1), 0) % GRID_W
    col = lax.broadcasted_iota(jnp.int32, (1, c), 1) % 4
    left = jnp.where(tok == 0, 0.0, pltpu.roll(zc, 1, 0))
    right = jnp.where(tok == GRID_W - 1, 0.0, pltpu.roll(zc, t - 1, 0))
    zp = jnp.where(has_prev, zp, 0.0)
    zn = jnp.where(has_next, zn, 0.0)
    if t > GRID_W:
        up = jnp.concatenate([zp, zc[:t - GRID_W]], axis=0)
        down = jnp.concatenate([zc[GRID_W:], zn], axis=0)
    else:
        up, down = zp, zn
    return jnp.where(col == 0, left, jnp.where(col == 1, right, jnp.where(col == 2, up, down)))


def _seq_shift(zc):
    t, c = zc.shape
    tok = lax.broadcasted_iota(jnp.int32, (t, 1), 0)
    col = lax.broadcasted_iota(jnp.int32, (1, c), 1) % 2
    prev = jnp.where(tok == 0, 0.0, pltpu.roll(zc, 1, 0))
    nxt = jnp.where(tok == t - 1, 0.0, pltpu.roll(zc, t - 1, 0))
    return jnp.where(col == 0, prev, nxt)


_STREAMS = ("r", "v", "kk", "dec", "kd", "bb")


def _rwkv_prep_lat_kernel(zc_ref, zp_ref, zn_ref, lc_ref, lp_ref, ln_ref, mum_ref, mul_ref, *rest, head):
    p, o = rest[:8], rest[8 + len(_STREAMS):]
    outs = dict(zip(_STREAMS + ("g", "bonus"), o))
    j = pl.program_id(1)
    has_prev = j > 0
    has_next = j < pl.num_programs(1) - 1
    zc = zc_ref[0]
    lc = lc_ref[0]
    zz = _lerp_shift(zc, _grid_shift(zc, zp_ref[0], zn_ref[0], has_prev, has_next), mum_ref[...])
    zl = _lerp_shift(lc, _grid_shift(lc, lp_ref[0], ln_ref[0], has_prev, has_next), mul_ref[...])
    _rwkv_streams(zz, zl, p, head, outs, True)


def _rwkv_prep_ctx_kernel(zc_ref, lc_ref, mum_ref, mul_ref, *rest, head):
    p, o = rest[:8], rest[8:]
    outs = dict(zip(_STREAMS, o))
    zc = zc_ref[0]
    lc = lc_ref[0]
    zz = _lerp_shift(zc, _seq_shift(zc), mum_ref[...])
    zl = _lerp_shift(lc, _seq_shift(lc), mul_ref[...])
    _rwkv_streams(zz, zl, p, head, outs, False)


def _rwkv_prep(z_main, z_lora, params, head, l_total, ctx_streams=None):
    b, l, cm = z_main.shape
    cl = z_lora.shape[-1]
    width = cm // 3
    full = lambda a: pl.BlockSpec(a.shape, lambda i, j: (0,) * a.ndim)
    grid_mode = ctx_streams is not None
    if grid_mode:
        l_ctx = l_total - l
        rows_per_tile = next(r for r in (8, 4, 2, 1) if l % (r * GRID_W) == 0 and l_ctx % (r * GRID_W) == 0)
        tt = rows_per_tile * GRID_W
        nt = l // tt
        first = l_ctx // tt
        last_row = l // GRID_W - 1
        cur = lambda c: pl.BlockSpec((1, tt, c), lambda i, j: (i, j, 0))
        prev = lambda c: pl.BlockSpec((1, GRID_W, c), lambda i, j: (i, jnp.maximum(j * rows_per_tile - 1, 0), 0))
        nxt = lambda c: pl.BlockSpec((1, GRID_W, c), lambda i, j: (i, jnp.minimum((j + 1) * rows_per_tile, last_row), 0))
        data = [z_main, z_main, z_main, z_lora, z_lora, z_lora]
        data_specs = [cur(cm), prev(cm), nxt(cm), cur(cl), prev(cl), nxt(cl)]
        kern = functools.partial(_rwkv_prep_lat_kernel, head=head)
        extra = list(ctx_streams)
        extra_specs = [pl.BlockSpec(memory_space=pl.ANY)] * len(extra)
        n_in = len(data) + len(params)
        aliases = {n_in + k: k for k in range(len(extra))}
        lat_only = 2
    else:
        tt, nt, first = l, 1, 0
        data = [z_main, z_lora]
        data_specs = [pl.BlockSpec((1, tt, cm), lambda i, j: (i, j, 0)), pl.BlockSpec((1, tt, cl), lambda i, j: (i, j, 0))]
        kern = functools.partial(_rwkv_prep_ctx_kernel, head=head)
        extra, extra_specs, aliases, lat_only = [], [], {}, 0
    single = pl.BlockSpec((1, tt, width), lambda i, j: (i, first + j, 0))
    double = pl.BlockSpec((2, 1, tt, width), lambda i, j: (0, i, first + j, 0))
    local = pl.BlockSpec((1, tt, width), lambda i, j: (i, j, 0))
    out_specs = [single] * 3 + [double] * 3 + [local] * lat_only
    out_shape = [_sds((b, l_total, width))] * 3 + [_sds((2, b, l_total, width))] * 3 + [_sds((b, l, width))] * lat_only
    return pl.pallas_call(
        kern,
        grid=(b, nt),
        in_specs=data_specs + [full(a) for a in params] + extra_specs,
        out_specs=out_specs,
        out_shape=out_shape,
        input_output_aliases=aliases,
        compiler_params=_cparams(("parallel", "parallel")),
        name="rwkv_prep_lat" if grid_mode else "rwkv_prep_ctx",
    )(*data, *params, *extra)


def _split_bf16(x):
    hi = x.astype(BF16)
    return hi, (x - hi.astype(F32)).astype(BF16)


def _rwkv_chunk_kernel(r_ref, v_ref, kk_ref, w_ref, kd_ref, bb_ref, y_ref, s_ref, *, c, head):
    d = pl.program_id(0)
    nb, _, width = r_ref.shape
    pair = 2 * head
    n_pairs = width // pair

    @pl.when(pl.program_id(2) == 0)
    def _():
        s_ref[...] = jnp.zeros_like(s_ref)

    ri = lax.broadcasted_iota(jnp.int32, (c, c), 0)
    ci = lax.broadcasted_iota(jnp.int32, (c, c), 1)
    before = jnp.where(d == 0, ci - ri, ri - ci) < 0
    incl = before | (ri == ci)
    eye = jnp.where(ri == ci, 1.0, 0.0)
    lmat = jnp.broadcast_to(jnp.where(incl, 1.0, 0.0).astype(BF16), (nb, c, c))

    def bmm(a, b, dims=((2,), (1,))):
        return lax.dot_general(a.astype(BF16), b.astype(BF16), (dims, ((0,), (0,))), preferred_element_type=F32)

    nt = ((2,), (2,))

    ld = jnp.log(w_ref[...])
    ld_hi, ld_lo = _split_bf16(ld)
    cum = bmm(lmat, ld_hi) + bmm(lmat, ld_lo)
    gn = jnp.exp(-cum)
    at = -(jnp.exp(cum - ld) * kk_ref[...])
    rt = jnp.exp(cum) * r_ref[...]
    bt = gn * bb_ref[...]
    kt = gn * kd_ref[...]
    vv = v_ref[...]

    def tiles(x):
        return jnp.stack([x[i, :, p * pair:(p + 1) * pair] for i in range(nb) for p in range(n_pairs)], axis=0)

    def per_head(x, select=None):
        xs = [x, x] if select is None else [jnp.where(select == hh, x, 0.0) for hh in range(2)]
        return jnp.stack(xs, axis=1).reshape((2 * x.shape[0],) + x.shape[1:])

    lane_head = lax.broadcasted_iota(jnp.int32, (1, 1, pair), 2) // head
    at_q, rt_q, bt_q, kt_q, v_q = tiles(at), tiles(rt), tiles(bt), tiles(kt), tiles(vv)
    s0 = s_ref[...]
    as0 = per_head(bmm(at_q, s0))
    rs0 = per_head(bmm(rt_q, s0))
    at_h, rt_h, v_h = per_head(at_q), per_head(rt_q), per_head(v_q)
    bt_h, kt_h = per_head(bt_q, lane_head), per_head(kt_q, lane_head)
    a_ab = jnp.where(before, bmm(at_h, bt_h, nt), 0.0)
    a_ak = jnp.where(before, bmm(at_h, kt_h, nt), 0.0)
    a_rb = jnp.where(incl, bmm(rt_h, bt_h, nt), 0.0)
    a_rk = jnp.where(incl, bmm(rt_h, kt_h, nt), 0.0)
    t = eye + a_ab
    pw = a_ab
    for _ in range(c.bit_length() - 2):
        pw = bmm(pw, pw)
        t = t + bmm(t, pw)
    u = bmm(t, as0 + bmm(a_ak, v_h))
    y = rs0 + bmm(a_rb, u) + bmm(a_rk, v_h)

    def merge(x):
        x = x.reshape((x.shape[0] // 2, 2) + x.shape[1:])
        return jnp.where(lane_head == 0, x[:, 0], x[:, 1])

    u_q, y_q = merge(u), merge(y)
    for i in range(nb):
        for p in range(n_pairs):
            y_ref[i, :, p * pair:(p + 1) * pair] = y_q[i * n_pairs + p]
    tn = ((1,), (1,))
    ones = jnp.ones((nb * n_pairs, c, pair), BF16)
    ds = bmm(bt_q, u_q, tn) + bmm(kt_q, v_q, tn)
    tot = bmm(tiles(ld_hi), ones, tn) + bmm(tiles(ld_lo), ones, tn)
    blk = (lax.broadcasted_iota(jnp.int32, (pair, pair), 0) // head
           == lax.broadcasted_iota(jnp.int32, (pair, pair), 1) // head)
    s_ref[...] = jnp.where(blk, jnp.exp(tot) * (s0 + ds), 0.0)


def _rwkv_chunked(shared, directional, l_ctx, head):
    b, ltot, width = shared[0].shape
    c = RWKV_CHUNK
    nb = RWKV_ROWS if b % RWKV_ROWS == 0 else 1
    nch, ncc = ltot // c, l_ctx // c
    nlc = nch - ncc

    def chunk(d, j):
        back = jnp.where(j < ncc, ncc - 1 - j, nch + ncc - 1 - j)
        return jnp.where(d == 0, j, back)

    def out_chunk(d, j):
        o = chunk(d, j)
        return jnp.where(o >= ncc, o - ncc, jnp.where(d == 0, 0, nlc - 1))

    sh_spec = pl.BlockSpec((nb, c, width), lambda d, i, j: (i, chunk(d, j), 0))
    di_spec = pl.BlockSpec((None, nb, c, width), lambda d, i, j: (d, i, chunk(d, j), 0))
    return pl.pallas_call(
        functools.partial(_rwkv_chunk_kernel, c=c, head=head),
        grid=(2, b // nb, nch),
        in_specs=[sh_spec] * 3 + [di_spec] * 3,
        out_specs=pl.BlockSpec((None, nb, c, width), lambda d, i, j: (d, i, out_chunk(d, j), 0)),
        out_shape=_sds((2, b, (nch - ncc) * c, width)),
        scratch_shapes=[pltpu.VMEM((nb * width // (2 * head), 2 * head, 2 * head), F32)],
        compiler_params=_cparams(("arbitrary", "arbitrary", "arbitrary")),
        name="rwkv_chunk",
    )(*shared, *directional)


def _s5_kernel(x_ref, d_ref, m_ref, pin_ref, pout_ref, ar_ref, ai_ref, y_ref, hall_ref, *, nb, n_chunks, n_ctx_chunks):
    x = x_ref[0]
    xb = x.astype(BF16)
    lat0 = n_ctx_chunks * nb
    xl = xb[lat0:]
    y = d_ref[0] * x[lat0:]
    half = ar_ref.shape[-1] // 2
    fwd = list(range(n_chunks))
    bwd = list(range(n_ctx_chunks - 1, -1, -1)) + list(range(n_chunks - 1, n_ctx_chunks - 1, -1))
    for d, order in enumerate((fwd, bwd)):
        v = _dot(xb, pin_ref[d, 0])
        ar = ar_ref[d, 0]
        ai = ai_ref[d, 0]
        h = jnp.zeros((nb, 2 * half), F32)
        for c in order:
            hall_ref[c * nb:(c + 1) * nb, :] = h
            h = h * ar + pltpu.roll(h, half, 1) * ai + v[c * nb:(c + 1) * nb]
        y = y + _dot(xl, m_ref[d, 0]) + _dot(hall_ref[lat0:, :].astype(BF16), pout_ref[d, 0])
    y_ref[0] = y


def _s5_tables(a_re, a_im, log_dt, b_re, b_im, c_re, c_im, tc):
    a_re, a_im, b_re, b_im = (t.astype(F32) for t in (a_re, a_im, b_re, b_im))
    c_re, c_im = c_re.astype(F32), c_im.astype(F32)
    dt = jnp.exp(log_dt.astype(F32))[..., None]
    mag = jnp.exp(dt * a_re)
    ab_re, ab_im = mag * jnp.cos(dt * a_im), mag * jnp.sin(dt * a_im)
    nr, ni = ab_re - 1.0, ab_im
    den = a_re * a_re + a_im * a_im
    cf_re = (nr * a_re + ni * a_im) / den
    cf_im = (ni * a_re - nr * a_im) / den
    bb_re = cf_re[..., None] * b_re - cf_im[..., None] * b_im
    bb_im = cf_re[..., None] * b_im + cf_im[..., None] * b_re

    k = jnp.arange(tc + 1, dtype=F32)[:, None, None, None]
    pmag = jnp.exp(k * (dt * a_re)[None])
    pang = k * (dt * a_im)[None]
    pw_re, pw_im = pmag * jnp.cos(pang), pmag * jnp.sin(pang)

    ab_k_re = pw_re[..., None] * bb_re[None] - pw_im[..., None] * bb_im[None]
    ab_k_im = pw_re[..., None] * bb_im[None] + pw_im[..., None] * bb_re[None]
    ca_k_re = c_re[None] * pw_re[:, :, :, None, :] - c_im[None] * pw_im[:, :, :, None, :]
    ca_k_im = c_re[None] * pw_im[:, :, :, None, :] + c_im[None] * pw_re[:, :, :, None, :]

    taps = (jnp.einsum('dghp,kdgpj->kdghj', c_re, ab_k_re[:tc], precision=HIGHEST)
            - jnp.einsum('dghp,kdgpj->kdghj', c_im, ab_k_im[:tc], precision=HIGHEST))
    s_idx = jnp.arange(tc)[:, None]
    t_idx = jnp.arange(tc)[None, :]
    hc = b_re.shape[-1]
    g = a_re.shape[1]
    p = a_re.shape[2]

    def toeplitz(tp, lag, valid):
        m = jnp.where(valid[:, :, None, None, None], tp[jnp.clip(lag, 0, tc - 1)], 0.0)
        return m.transpose(2, 0, 4, 1, 3).reshape(g, tc * hc, tc * hc)

    m_f = toeplitz(taps[:, 0], t_idx - s_idx, t_idx >= s_idx)
    m_b = toeplitz(taps[:, 1], s_idx - t_idx, s_idx >= t_idx)
    m = jnp.stack([m_f, m_b]).astype(BF16)

    def pin_of(re, im):
        both = jnp.concatenate([re, im], axis=2)
        return both.transpose(1, 0, 3, 2).reshape(g, tc * hc, 2 * p)

    pin = jnp.stack([pin_of(ab_k_re[:tc, 0][::-1], ab_k_im[:tc, 0][::-1]),
                     pin_of(ab_k_re[:tc, 1], ab_k_im[:tc, 1])]).astype(BF16)

    def pout_of(re, im):
        both = jnp.concatenate([re, -im], axis=3)
        return both.transpose(1, 3, 0, 2).reshape(g, 2 * p, tc * hc)

    pout = jnp.stack([pout_of(ca_k_re[1:tc + 1, 0], ca_k_im[1:tc + 1, 0]),
                      pout_of(ca_k_re[1:tc + 1, 1][::-1], ca_k_im[1:tc + 1, 1][::-1])]).astype(BF16)

    at_re, at_im = pw_re[tc], pw_im[tc]
    ar = jnp.concatenate([at_re, at_re], axis=-1)[:, :, None, :]
    ai = jnp.concatenate([-at_im, at_im], axis=-1)[:, :, None, :]
    return m, pin, pout, ar, ai


def _s5_mix(u_ctx, u_lat, s5_d, tables):
    m, pin, pout, ar, ai = tables
    b, lc, width = u_ctx.shape
    l = u_lat.shape[1]
    g = m.shape[1]
    hc = width // g
    tc = S5_CHUNK
    w = tc * hc
    ncc, nlc = lc // tc, l // tc
    nch = ncc + nlc
    u = jnp.concatenate([u_ctx, u_lat], axis=1)
    xg = u.reshape(b, nch, tc, g, hc).transpose(3, 1, 0, 2, 4).reshape(g, nch * b, w)
    dg = jnp.tile(s5_d.reshape(g, 1, hc), (1, tc, 1)).reshape(g, 1, w)
    pgrp = lambda a: pl.BlockSpec((2, 1) + a.shape[2:], lambda i: (0, i, 0, 0))
    y = pl.pallas_call(
        functools.partial(_s5_kernel, nb=b, n_chunks=nch, n_ctx_chunks=ncc),
        grid=(g,),
        in_specs=[pl.BlockSpec((1, nch * b, w), lambda i: (i, 0, 0)),
                  pl.BlockSpec((1, 1, w), lambda i: (i, 0, 0)),
                  pgrp(m), pgrp(pin), pgrp(pout), pgrp(ar), pgrp(ai)],
        out_specs=pl.BlockSpec((1, nlc * b, w), lambda i: (i, 0, 0)),
        out_shape=_sds((g, nlc * b, w)),
        scratch_shapes=[pltpu.VMEM((nch * b, pin.shape[-1]), F32)],
        compiler_params=_cparams(("parallel",), 56),
        name="s5_mix",
    )(xg, dg, m, pin, pout, ar, ai)
    return y.reshape(g, nlc, b, tc, hc).transpose(2, 1, 3, 0, 4).reshape(b, l, width)


def _gelu(x):
    return 0.5 * x * (1.0 + lax.erf(x * (1.0 / math.sqrt(2.0))))


def _mix_out_kernel(s5_ref, rw_ref, bon_ref, g_ref, x_ref, g1_ref, sh2_ref, sc2_ref, n2_ref,
                    wglu_ref, bglu_ref, lnw_ref, lnb_ref, wos_ref, wor_ref, h1_ref, hn_ref, *, head):
    y1 = _gelu(s5_ref[0])
    s5o = y1 * jax.nn.sigmoid(_dot(y1.astype(BF16), wglu_ref[...]) + bglu_ref[...])
    y = rw_ref[0, 0] + rw_ref[1, 0]
    seg_m = _seg_matrix(y.shape[-1], head)
    inv = 1.0 / head
    mu = _seg_sum(y, seg_m) * inv
    yc = y - mu
    var = _seg_sum(yc * yc, seg_m) * inv
    yn = yc * lax.rsqrt(var + GN_EPS) * lnw_ref[...] + lnb_ref[...]
    rwo = (yn + bon_ref[0]) * g_ref[0]
    mix = _dot(s5o.astype(BF16), wos_ref[...]) + _dot(rwo.astype(BF16), wor_ref[...])
    h1 = x_ref[0] + g1_ref[0] * mix
    h1_ref[0] = h1
    ms = jnp.mean(h1 * h1, axis=-1, keepdims=True)
    hn_ref[0] = (h1 * lax.rsqrt(ms + NORM_EPS) * n2_ref[...]) * (1.0 + sc2_ref[0]) + sh2_ref[0]


def _mix_out(s5y, rwy, bonus, gate, x, g1, sh2, sc2, n2g, w_glu, b_glu, ln_w, ln_b, w_out_s, w_out_r, head):
    b, l, d = x.shape
    width = s5y.shape[-1]
    tm = min(512, l)
    tok = lambda c: pl.BlockSpec((1, tm, c), lambda i, j: (i, j, 0))
    mod = pl.BlockSpec((1, 1, d), lambda i, j: (i, 0, 0))
    full = lambda a: pl.BlockSpec(a.shape, lambda i, j: (0,) * a.ndim)
    consts = (n2g, w_glu, b_glu, ln_w, ln_b, w_out_s, w_out_r)
    return pl.pallas_call(
        functools.partial(_mix_out_kernel, head=head),
        grid=(b, l // tm),
        in_specs=[tok(width), pl.BlockSpec((2, 1, tm, width), lambda i, j: (0, i, j, 0)), tok(width), tok(width)]
        + [tok(d), mod, mod, mod] + [full(a) for a in consts],
        out_specs=[tok(d), tok(d)],
        out_shape=[_sds((b, l, d)), _sds((b, l, d))],
        compiler_params=_cparams(("parallel", "parallel")),
        name="mix_out",
    )(s5y, rwy, bonus, gate, x, g1, sh2, sc2, *consts)


def _staircase(k):
    return [(a, b) for a in range(k) for b in range(k) if (a + 1) * (b + 1) <= k]


def _topk_rows(s, k):
    rows = s.shape[0]
    iota = lax.broadcasted_iota(jnp.int32, s.shape, 0)
    vals, idxs = [], []
    for _ in range(k):
        m = jnp.max(s, axis=0, keepdims=True)
        idx = jnp.min(jnp.where(s == m, iota, rows), axis=0, keepdims=True)
        vals.append(m)
        idxs.append(idx)
        s = jnp.where(iota == idx, -jnp.inf, s)
    return jnp.concatenate(vals, axis=0), jnp.concatenate(idxs, axis=0).astype(F32)


def _peer_topk_kernel(hn_ref, wq_ref, keys_ref, sel1_ref, sel2_ref, pad_ref, off_ref, gate_ref, *, heads, nkeys, half,
                      rows_per_expert):
    hb = hn_ref[...].astype(BF16)
    nt = (((1,), (1,)), ((), ()))
    sel1 = sel1_ref[...]
    sel2 = sel2_ref[...]
    pick = lambda sel, a: jnp.dot(sel, a, precision=HIGHEST, preferred_element_type=F32)
    ids, gates = [], []
    for h in range(heads):
        top = []
        for c in range(2):
            row0 = (h * 2 + c) * half
            qt = lax.dot_general(wq_ref[row0:row0 + half, :], hb, nt, preferred_element_type=F32)
            st = _dot(keys_ref[h * 2 + c], qt.astype(BF16))
            top.append(_topk_rows(st, PEER_TOPK))
        (s1, i1), (s2, i2) = top
        cand = pick(sel1, s1) + pick(sel2, s2) + pad_ref[...]
        cid = pick(sel1, i1) * nkeys + pick(sel2, i2)
        best, pos = _topk_rows(cand, PEER_TOPK)
        riota = lax.broadcasted_iota(jnp.int32, cid.shape, 0).astype(F32)
        eid = jnp.concatenate(
            [jnp.sum(jnp.where(riota == pos[r:r + 1], cid, 0.0), axis=0, keepdims=True) for r in range(PEER_TOPK)], axis=0)
        e = jnp.exp(best - jnp.max(best, axis=0, keepdims=True))
        gates.append(e / jnp.sum(e, axis=0, keepdims=True))
        ids.append(eid)
    eid = jnp.concatenate(ids, axis=0).T.astype(jnp.int32)
    off_ref[...] = eid * rows_per_expert
    gate_ref[...] = jnp.concatenate(gates, axis=0).T


def _peer_topk(hn, wq_t, keys, rows_per_expert):
    n, d = hn.shape
    heads = keys.shape[0] // 2
    nkeys, half = keys.shape[1], keys.shape[2]
    tm = min(256, n)
    pairs = _staircase(PEER_TOPK)
    rows = -(-len(pairs) // SUBLANES) * SUBLANES
    sel1 = np.zeros((rows, PEER_TOPK), np.float32)
    sel2 = np.zeros((rows, PEER_TOPK), np.float32)
    pad = np.zeros((rows, 1), np.float32)
    for i, (a, b) in enumerate(pairs):
        sel1[i, a] = 1.0
        sel2[i, b] = 1.0
    pad[len(pairs):] = -np.inf
    slots = heads * PEER_TOPK
    full = lambda a: pl.BlockSpec(a.shape, lambda i: (0,) * a.ndim)
    consts = (wq_t, keys, jnp.asarray(sel1), jnp.asarray(sel2), jnp.asarray(pad))
    return pl.pallas_call(
        functools.partial(_peer_topk_kernel, heads=heads, nkeys=nkeys, half=half, rows_per_expert=rows_per_expert),
        grid=(n // tm,),
        in_specs=[pl.BlockSpec((tm, d), lambda i: (i, 0))] + [full(a) for a in consts],
        out_specs=[pl.BlockSpec((tm, slots), lambda i: (i, 0))] * 2,
        out_shape=[_sds((n, slots), jnp.int32), _sds((n, slots))],
        compiler_params=_cparams(("parallel",)),
        name="peer_topk",
    )(hn, *consts)


def _pack_table(tab):
    e, d = tab.shape
    half = d // 2
    bits = lax.bitcast_convert_type(tab.astype(BF16), jnp.uint16).astype(jnp.uint32)
    packed = ((bits[:, :half] << 16) | bits[:, half:]).reshape(e * (half // LANES), LANES)
    return jnp.pad(packed, ((0, SUBLANES), (0, 0)))


def _expert_halves(tab_ref, off):
    bits = tab_ref[pl.ds(off, SUBLANES), :]
    return pltpu.bitcast(bits & jnp.uint32(0xFFFF0000), F32), pltpu.bitcast(bits << 16, F32)


def _peer_u_kernel(off_ref, h_ref, gate_ref, tab_ref, w_ref, q_ref, r_ref, *, tq, slots):
    half_rows = SUBLANES // 2
    sub = lax.broadcasted_iota(jnp.int32, (SUBLANES, LANES), 0)
    low = sub < half_rows
    groups = slots // SUBLANES
    qrows = slots * half_rows

    ones = jnp.ones((SUBLANES, LANES), BF16)
    nt = (((1,), (1,)), ((), ()))

    def lane_sums(t):
        start = t * qrows if isinstance(t, int) else pl.multiple_of(t * qrows, qrows)
        q = q_ref[pl.ds(start, qrows), :]
        r_ref[pl.ds(t, 1), :] = lax.dot_general(ones, q, nt, preferred_element_type=F32)[0:1]

    q_ref[(tq - 1) * qrows:, :] = jnp.zeros((qrows, LANES), BF16)

    def token(t, carry):
        lane_sums(jnp.where(t == 0, tq - 1, t - 1))
        h = h_ref[t]
        h_first = jnp.where(low, h, 0.0)
        h_second = jnp.where(low, pltpu.roll(h, half_rows, 0), 0.0)
        q_tok = q_ref.at[pl.ds(pl.multiple_of(t * qrows, qrows), qrows)]
        grows = SUBLANES * half_rows
        for g8 in range(groups):
            base = t * slots + g8 * SUBLANES
            ps = []
            for r in range(SUBLANES):
                hi, lo = _expert_halves(tab_ref, off_ref[0, 0, base + r])
                ps.append(hi * h_first + lo * h_second)
            pairs = [ps[i] + pltpu.roll(ps[i + 1], half_rows, 0) for i in range(0, SUBLANES, 2)]
            q_tok[g8 * grows:(g8 + 1) * grows, :] = jnp.concatenate(pairs, axis=0).astype(BF16)
        return carry

    lax.fori_loop(0, tq, token, 0)
    lane_sums(tq - 1)

    fold = (lax.broadcasted_iota(jnp.int32, (qrows, slots), 0) // half_rows
            == lax.broadcasted_iota(jnp.int32, (qrows, slots), 1))
    fold = jnp.where(fold, 1.0, 0.0).astype(BF16)
    rs = r_ref[...]
    rh = rs.astype(BF16)
    rl = (rs - rh.astype(F32)).astype(BF16)
    act = _dot(rh, fold) + _dot(rl, fold)
    w_ref[...] = gate_ref[...] * _gelu(act)


def _smem_block(tq, slots):
    return pl.BlockSpec((1, 1, tq * slots), lambda i: (i, 0, 0), memory_space=pltpu.SMEM)


def _peer_u(off_blocks, hn3, gate, tab):
    n, rows, lanes = hn3.shape
    slots = gate.shape[1]
    tq = off_blocks.shape[-1] // slots
    return pl.pallas_call(
        functools.partial(_peer_u_kernel, tq=tq, slots=slots),
        grid=(n // tq,),
        in_specs=[_smem_block(tq, slots),
                  pl.BlockSpec((tq, rows, lanes), lambda i: (i, 0, 0)),
                  pl.BlockSpec((tq, slots), lambda i: (i, 0)),
                  pl.BlockSpec(tab.shape, lambda i: (0, 0), pipeline_mode=pl.Buffered(1))],
        out_specs=pl.BlockSpec((tq, slots), lambda i: (i, 0)),
        out_shape=_sds((n, slots)),
        scratch_shapes=[pltpu.VMEM((tq * slots * (SUBLANES // 2), lanes), BF16),
                        pltpu.VMEM((tq, slots * (SUBLANES // 2)), F32)],
        compiler_params=_cparams(("parallel",), 52),
        name="peer_u",
    )(off_blocks, hn3, gate, tab)


def _peer_v_kernel(off_ref, w_ref, tab_ref, o_ref, ws_ref, dg_ref, ot_ref, *, tq, slots):
    sub = lax.broadcasted_iota(jnp.int32, (SUBLANES, LANES), 0)
    low = sub < SUBLANES // 2

    eye = lax.broadcasted_iota(jnp.int32, (slots, slots), 0) == lax.broadcasted_iota(jnp.int32, (slots, slots), 1)

    def diag(t, carry):
        wrow = jnp.broadcast_to(w_ref[pl.ds(t, 1), :], (slots, slots))
        dg_ref[pl.ds(pl.multiple_of(t * slots, slots), slots), :] = jnp.where(eye, wrow, 0.0).astype(BF16)
        return carry

    lax.fori_loop(0, tq, diag, 0, unroll=4)
    ws_ref[...] = _dot(dg_ref[...], jnp.ones((slots, LANES), BF16))

    def token(t, carry):
        zero = jnp.zeros((SUBLANES, LANES), F32)
        first, second = [zero, zero], [zero, zero]
        for s in range(slots):
            hi, lo = _expert_halves(tab_ref, off_ref[0, 0, t * slots + s])
            w = jnp.broadcast_to(ws_ref[pl.ds(t * slots + s, 1), :], (SUBLANES, LANES))
            first[s % 2] = first[s % 2] + hi * w
            second[s % 2] = second[s % 2] + lo * w
        tile = jnp.where(low, first[0] + first[1], pltpu.roll(second[0] + second[1], SUBLANES // 2, 0))
        ot_ref[pl.ds(pl.multiple_of(t * SUBLANES, SUBLANES), SUBLANES), :] = tile
        return carry

    lax.fori_loop(0, tq, token, 0)
    for j in range(SUBLANES):
        o_ref[:, j * LANES:(j + 1) * LANES] = ot_ref[pl.ds(j, tq, stride=SUBLANES), :]


def _peer_v(off_blocks, w, tab):
    n, slots = w.shape
    tq = off_blocks.shape[-1] // slots
    lanes = tab.shape[-1]
    assert slots == lanes, "the weight splat uses one (slots, lanes) identity tile"
    return pl.pallas_call(
        functools.partial(_peer_v_kernel, tq=tq, slots=slots),
        grid=(n // tq,),
        in_specs=[_smem_block(tq, slots),
                  pl.BlockSpec((tq, slots), lambda i: (i, 0)),
                  pl.BlockSpec(tab.shape, lambda i: (0, 0), pipeline_mode=pl.Buffered(1))],
        out_specs=pl.BlockSpec((tq, VREG_WORDS), lambda i: (i, 0)),
        out_shape=_sds((n, VREG_WORDS)),
        scratch_shapes=[pltpu.VMEM((tq * slots, lanes), F32), pltpu.VMEM((tq * slots, slots), BF16),
                        pltpu.VMEM((tq * SUBLANES, lanes), F32)],
        compiler_params=_cparams(("parallel",), 52),
        name="peer_v",
    )(off_blocks, w, tab)


def _final_kernel(h1_ref, p_ref, g2_ref, nf_ref, o_ref):
    h = h1_ref[0] + g2_ref[0] * p_ref[0]
    ms = jnp.mean(h * h, axis=-1, keepdims=True)
    o_ref[0] = h * lax.rsqrt(ms + NORM_EPS) * nf_ref[...]


def _final_norm(h1, peer, g2, nfg):
    b, l, d = h1.shape
    tm = min(512, l)
    tok = pl.BlockSpec((1, tm, d), lambda i, j: (i, j, 0))
    return pl.pallas_call(
        _final_kernel,
        grid=(b, l // tm),
        in_specs=[tok, tok, pl.BlockSpec((1, 1, d), lambda i, j: (i, 0, 0)), pl.BlockSpec((1, d), lambda i, j: (0, 0))],
        out_specs=tok,
        out_shape=_sds((b, l, d)),
        compiler_params=_cparams(("parallel", "parallel")),
        name="final_norm",
    )(h1, peer, g2, nfg)


def _pad_cols(a, n):
    return jnp.pad(a, [(0, 0)] * (a.ndim - 1) + [(0, n - a.shape[-1])])


def _layer(h, hc, c, c_ctx, p, norm_f_g):
    b, l, d = h.shape
    lc = hc.shape[1]
    s5w = p['s5_d'].shape[-1]
    rww = p['rw_w0'].shape[-1]
    heads, head = p['rw_r_k'].shape
    n_dec, n_aaa, n_gate = p['rw_w_w2'].shape[1], p['rw_w_a2'].shape[1], p['rw_w_g2'].shape[0]
    n_lora = n_dec + n_aaa + n_gate
    lora_pad = -(-n_lora // LANES) * LANES

    rows = -(-(b + 1) // SUBLANES) * SUBLANES
    cc = jnp.zeros((rows, d), F32).at[:b].set(c).at[b].set(c_ctx)
    mods = _ada_mods(cc, p['w_ada'], p['b_ada'])
    sh1, sc1, g1, sh2, sc2, g2 = [m.reshape(rows, 1, d) for m in jnp.split(mods, 6, axis=-1)]
    lat = lambda m: m[:b]
    ctxm = lambda m: m[b:b + 1]

    w_in = p['w_in']
    w_s5 = w_in[:, :s5w].astype(BF16)
    w_main = w_in[:, s5w:s5w + 3 * rww].astype(BF16)
    w_lora = _pad_cols(w_in[:, s5w + 3 * rww:], lora_pad).astype(BF16)
    n1g = p['norm1_g'].reshape(1, d)
    u_l, zm_l, zl_l = _in_proj(h, lat(sh1), lat(sc1), n1g, w_s5, w_main, w_lora)
    u_c, zm_c, zl_c = _in_proj(hc, ctxm(sh1), ctxm(sc1), n1g, w_s5, w_main, w_lora)

    mu = p['rw_mu']
    mu_main = mu[:3 * rww].reshape(1, -1)
    mu_lora = _pad_cols(mu[3 * rww:], lora_pad).reshape(1, -1)
    place = lambda w, r0: jnp.zeros(w.shape[:-2] + (lora_pad, rww), F32).at[..., r0:r0 + w.shape[-2], :].set(w).astype(BF16)
    rw_params = (place(p['rw_w_g2'], n_dec + n_aaa), place(p['rw_w_w2'], 0), place(p['rw_w_a2'], n_dec),
                 p['rw_w0'].reshape(2, 1, rww), p['rw_a0'].reshape(2, 1, rww),
                 p['rw_k_k'].reshape(1, rww), p['rw_k_a'].reshape(1, rww), p['rw_r_k'].reshape(1, rww))
    prep_params = (mu_main, mu_lora) + rw_params
    ctx_streams = _rwkv_prep(zm_c, zl_c, prep_params, head, lc + l)
    *streams, gate_l, bonus_l = _rwkv_prep(zm_l, zl_l, prep_params, head, lc + l, ctx_streams)
    rw_y = _rwkv_chunked(tuple(streams[:3]), tuple(streams[3:]), lc, head)

    tables = _s5_tables(p['s5_a_re'], p['s5_a_im'], p['s5_log_dt'], p['s5_b_re'], p['s5_b_im'],
                        p['s5_c_re'], p['s5_c_im'], S5_CHUNK)
    s5_y = _s5_mix(u_c, u_l, p['s5_d'], tables)

    w_out = p['w_out']
    h1, hn2 = _mix_out(s5_y, rw_y, bonus_l, gate_l, h, lat(g1), lat(sh2), lat(sc2), p['norm2_g'].reshape(1, d),
                       p['s5_w_glu'].astype(BF16), p['s5_b_glu'].reshape(1, s5w),
                       p['rw_ln_w'].reshape(1, rww), p['rw_ln_b'].reshape(1, rww),
                       w_out[:s5w].astype(BF16), w_out[s5w:].astype(BF16), head)

    n = b * l
    keys = p['peer_keys']
    pheads, _, nkeys, half = keys.shape
    assert d == VREG_WORDS, "peer_u / peer_v hold one token's features in a single (8, 128) tile"
    off, gate = _peer_topk(hn2.reshape(n, d), p['peer_w_q'].T.astype(BF16),
                           keys.reshape(pheads * 2, nkeys, half).astype(BF16), d // (2 * LANES))
    slots = gate.shape[1]
    tq = min(PEER_TILE, n)
    blocks = lambda a: a.reshape(n // tq, 1, tq * slots)
    w = _peer_u(blocks(off), hn2.reshape(n, SUBLANES, LANES), gate, _pack_table(p['peer_u']))
    peer = _peer_v(blocks(off), w, _pack_table(p['peer_v']))

    return _final_norm(h1, peer.reshape(b, l, d), lat(g2), norm_f_g.reshape(1, d))


def kernel(x, c, ctx, c_ctx, w_ada, b_ada, norm1_g, norm2_g, w_in, s5_a_re, s5_a_im, s5_log_dt, s5_b_re, s5_b_im, s5_c_re, s5_c_im, s5_d, s5_w_glu, s5_b_glu, rw_mu, rw_w0, rw_w_w2, rw_a0, rw_w_a2, rw_w_g2, rw_k_k, rw_k_a, rw_r_k, rw_ln_w, rw_ln_b, w_out, peer_w_q, peer_keys, peer_u, peer_v, norm_f_g):
    assert w_ada.shape[0] == 1, "single-layer block: the context stream is never updated"
    p = dict(w_ada=w_ada[0], b_ada=b_ada[0], norm1_g=norm1_g[0], norm2_g=norm2_g[0], w_in=w_in[0],
             s5_a_re=s5_a_re[0], s5_a_im=s5_a_im[0], s5_log_dt=s5_log_dt[0], s5_b_re=s5_b_re[0], s5_b_im=s5_b_im[0],
             s5_c_re=s5_c_re[0], s5_c_im=s5_c_im[0], s5_d=s5_d[0], s5_w_glu=s5_w_glu[0], s5_b_glu=s5_b_glu[0],
             rw_mu=rw_mu[0], rw_w0=rw_w0[0], rw_w_w2=rw_w_w2[0], rw_a0=rw_a0[0], rw_w_a2=rw_w_a2[0],
             rw_w_g2=rw_w_g2[0], rw_k_k=rw_k_k[0], rw_k_a=rw_k_a[0], rw_r_k=rw_r_k[0], rw_ln_w=rw_ln_w[0],
             rw_ln_b=rw_ln_b[0], w_out=w_out[0], peer_w_q=peer_w_q[0], peer_keys=peer_keys[0],
             peer_u=peer_u[0], peer_v=peer_v[0])
    return _layer(x, ctx, c, c_ctx, p, norm_f_g)
```

```python
import functools
import math

import jax
import jax.numpy as jnp
import numpy as np
from jax import lax
from jax.experimental import pallas as pl
from jax.experimental.pallas import tpu as pltpu

F32 = jnp.float32
BF16 = jnp.bfloat16
HIGHEST = lax.Precision.HIGHEST

SUBLANES = 8
LANES = 128
VREG_WORDS = SUBLANES * LANES

GRID_W = 64
NORM_EPS = 1e-6
GN_EPS = 64e-5
KK_EPS = 1e-12
PEER_TOPK = 16

S5_CHUNK = 32
RWKV_CHUNK = 64
RWKV_ROWS = 4
PEER_TILE = 64
PEER_U_GROUP = 4


def _dot(a, b):
    return jnp.dot(a, b, preferred_element_type=F32)


def _cparams(sem, vmem_mb=48):
    return pltpu.CompilerParams(dimension_semantics=sem, vmem_limit_bytes=vmem_mb * 1024 * 1024)


def _sds(shape, dtype=F32):
    return jax.ShapeDtypeStruct(shape, dtype)


def _mods_kernel(c_ref, w_ref, b_ref, o_ref):
    c = c_ref[...]
    s = c * jax.nn.sigmoid(c)
    o_ref[...] = jnp.dot(s, w_ref[...], precision=HIGHEST, preferred_element_type=F32) + b_ref[...]


def _ada_mods(cc, w_ada, b_ada):
    rows, d = cc.shape
    n = w_ada.shape[1]
    bn = d if n % d == 0 else n
    return pl.pallas_call(
        _mods_kernel,
        grid=(n // bn,),
        in_specs=[pl.BlockSpec((rows, d), lambda j: (0, 0)),
                  pl.BlockSpec((d, bn), lambda j: (0, j)),
                  pl.BlockSpec((1, bn), lambda j: (0, j))],
        out_specs=pl.BlockSpec((rows, bn), lambda j: (0, j)),
        out_shape=_sds((rows, n)),
        compiler_params=_cparams(("arbitrary",)),
        name="ada_mods",
    )(cc, w_ada, b_ada.reshape(1, n))


def _inproj_kernel(x_ref, sh_ref, sc_ref, g_ref, ws_ref, wz_ref, wl_ref, u_ref, z_ref, l_ref):
    x = x_ref[0]
    ms = jnp.mean(x * x, axis=-1, keepdims=True)
    hn = (x * lax.rsqrt(ms + NORM_EPS) * g_ref[...]) * (1.0 + sc_ref[0]) + sh_ref[0]
    hb = hn.astype(BF16)
    u_ref[0] = _dot(hb, ws_ref[...])
    z_ref[0] = _dot(hb, wz_ref[...])
    l_ref[0] = _dot(hb, wl_ref[...])


def _in_proj(x, shift, scale, gain, w_s5, w_main, w_lora):
    b, l, d = x.shape
    tm = min(512, l)
    bm = shift.shape[0]
    mod_map = (lambda i, j: (i, 0, 0)) if bm > 1 else (lambda i, j: (0, 0, 0))
    full = lambda a: pl.BlockSpec(a.shape, lambda i, j: (0,) * a.ndim)
    return pl.pallas_call(
        _inproj_kernel,
        grid=(b, l // tm),
        in_specs=[pl.BlockSpec((1, tm, d), lambda i, j: (i, j, 0)),
                  pl.BlockSpec((1, 1, d), mod_map), pl.BlockSpec((1, 1, d), mod_map),
                  full(gain), full(w_s5), full(w_main), full(w_lora)],
        out_specs=[pl.BlockSpec((1, tm, w.shape[1]), lambda i, j: (i, j, 0)) for w in (w_s5, w_main, w_lora)],
        out_shape=[_sds((b, l, w.shape[1])) for w in (w_s5, w_main, w_lora)],
        compiler_params=_cparams(("parallel", "parallel")),
        name="in_proj",
    )(x, shift, scale, gain, w_s5, w_main, w_lora)


def _seg_matrix(width, seg):
    r = lax.broadcasted_iota(jnp.int32, (width, width), 0) // seg
    c = lax.broadcasted_iota(jnp.int32, (width, width), 1) // seg
    return jnp.where(r == c, 1.0, 0.0).astype(BF16)


def _seg_sum(x, seg_m):
    hi = x.astype(BF16)
    lo = (x - hi.astype(F32)).astype(BF16)
    return _dot(hi, seg_m) + _dot(lo, seg_m)


def _softplus(x):
    return jnp.maximum(x, 0.0) + jnp.log1p(jnp.exp(-jnp.abs(x)))


def _rwkv_streams(zz, zl, p, head, outs, with_gate):
    (wg_ref, ww_ref, wa_ref, w0_ref, a0_ref, kk_ref, ka_ref, rk_ref) = p
    width = kk_ref.shape[-1]
    r = zz[:, :width]
    k = zz[:, width:2 * width]
    v = zz[:, 2 * width:3 * width]
    seg_m = _seg_matrix(width, head)
    kk = k * kk_ref[...]
    kk = kk * lax.rsqrt(_seg_sum(kk * kk, seg_m) + KK_EPS)
    outs["r"][0] = r
    outs["v"][0] = v
    outs["kk"][0] = kk
    tl = jnp.tanh(zl).astype(BF16)
    zb = zl.astype(BF16)
    bonus = None
    for d in range(2):
        w = -_softplus(-(w0_ref[d] + _dot(tl, ww_ref[d]))) - 0.5
        outs["dec"][d, 0] = jnp.exp(-jnp.exp(w))
        a = jax.nn.sigmoid(a0_ref[d] + _dot(zb, wa_ref[d]))
        kd = k * (1.0 + (a - 1.0) * ka_ref[...])
        outs["kd"][d, 0] = kd
        outs["bb"][d, 0] = kk * a
        if with_gate:
            bd = _seg_sum(r * kd * rk_ref[...], seg_m) * v
            bonus = bd if bonus is None else bonus + bd
    if with_gate:
        outs["bonus"][0] = bonus
        outs["g"][0] = _dot(jax.nn.sigmoid(zl).astype(BF16), wg_ref[...])


def _lerp_shift(z, shifted, mu):
    return z + (shifted - z) * mu


def _grid_shift(zc, zp, zn, has_prev, has_next):
    t, c = zc.shape
    tok = lax.broadcasted_iota(jnp.int32, (t, 1), 0) % GRID_W
    col = lax.broadcasted_iota(jnp.int32, (1, c), 1) % 4
    left = jnp.where(tok == 0, 0.0, pltpu.roll(zc, 1, 0))
    right = jnp.where(tok == GRID_W - 1, 0.0, pltpu.roll(zc, t - 1, 0))
    zp = jnp.where(has_prev, zp, 0.0)
    zn = jnp.where(has_next, zn, 0.0)
    if t > GRID_W:
        up = jnp.concatenate([zp, zc[:t - GRID_W]], axis=0)
        down = jnp.concatenate([zc[GRID_W:], zn], axis=0)
    else:
        up, down = zp, zn
    return jnp.where(col == 0, left, jnp.where(col == 1, right, jnp.where(col == 2, up, down)))


def _seq_shift(zc):
    t, c = zc.shape
    tok = lax.broadcasted_iota(jnp.int32, (t, 1), 0)
    col = lax.broadcasted_iota(jnp.int32, (1, c), 1) % 2
    prev = jnp.where(tok == 0, 0.0, pltpu.roll(zc, 1, 0))
    nxt = jnp.where(tok == t - 1, 0.0, pltpu.roll(zc, t - 1, 0))
    return jnp.where(col == 0, prev, nxt)


_STREAMS = ("r", "v", "kk", "dec", "kd", "bb")


def _rwkv_prep_lat_kernel(zc_ref, zp_ref, zn_ref, lc_ref, lp_ref, ln_ref, mum_ref, mul_ref, *rest, head):
    p, o = rest[:8], rest[8 + len(_STREAMS):]
    outs = dict(zip(_STREAMS + ("g", "bonus"), o))
    j = pl.program_id(1)
    has_prev = j > 0
    has_next = j < pl.num_programs(1) - 1
    zc = zc_ref[0]
    lc = lc_ref[0]
    zz = _lerp_shift(zc, _grid_shift(zc, zp_ref[0], zn_ref[0], has_prev, has_next), mum_ref[...])
    zl = _lerp_shift(lc, _grid_shift(lc, lp_ref[0], ln_ref[0], has_prev, has_next), mul_ref[...])
    _rwkv_streams(zz, zl, p, head, outs, True)


def _rwkv_prep_ctx_kernel(zc_ref, lc_ref, mum_ref, mul_ref, *rest, head):
    p, o = rest[:8], rest[8:]
    outs = dict(zip(_STREAMS, o))
    zc = zc_ref[0]
    lc = lc_ref[0]
    zz = _lerp_shift(zc, _seq_shift(zc), mum_ref[...])
    zl = _lerp_shift(lc, _seq_shift(lc), mul_ref[...])
    _rwkv_streams(zz, zl, p, head, outs, False)


def _rwkv_prep(z_main, z_lora, params, head, l_total, ctx_streams=None):
    b, l, cm = z_main.shape
    cl = z_lora.shape[-1]
    width = cm // 3
    full = lambda a: pl.BlockSpec(a.shape, lambda i, j: (0,) * a.ndim)
    grid_mode = ctx_streams is not None
    if grid_mode:
        l_ctx = l_total - l
        rows_per_tile = next(r for r in (8, 4, 2, 1) if l % (r * GRID_W) == 0 and l_ctx % (r * GRID_W) == 0)
        tt = rows_per_tile * GRID_W
        nt = l // tt
        first = l_ctx // tt
        last_row = l // GRID_W - 1
        cur = lambda c: pl.BlockSpec((1, tt, c), lambda i, j: (i, j, 0))
        prev = lambda c: pl.BlockSpec((1, GRID_W, c), lambda i, j: (i, jnp.maximum(j * rows_per_tile - 1, 0), 0))
        nxt = lambda c: pl.BlockSpec((1, GRID_W, c), lambda i, j: (i, jnp.minimum((j + 1) * rows_per_tile, last_row), 0))
        data = [z_main, z_main, z_main, z_lora, z_lora, z_lora]
        data_specs = [cur(cm), prev(cm), nxt(cm), cur(cl), prev(cl), nxt(cl)]
        kern = functools.partial(_rwkv_prep_lat_kernel, head=head)
        extra = list(ctx_streams)
        extra_specs = [pl.BlockSpec(memory_space=pl.ANY)] * len(extra)
        n_in = len(data) + len(params)
        aliases = {n_in + k: k for k in range(len(extra))}
        lat_only = 2
    else:
        tt, nt, first = l, 1, 0
        data = [z_main, z_lora]
        data_specs = [pl.BlockSpec((1, tt, cm), lambda i, j: (i, j, 0)), pl.BlockSpec((1, tt, cl), lambda i, j: (i, j, 0))]
        kern = functools.partial(_rwkv_prep_ctx_kernel, head=head)
        extra, extra_specs, aliases, lat_only = [], [], {}, 0
    single = pl.BlockSpec((1, tt, width), lambda i, j: (i, first + j, 0))
    double = pl.BlockSpec((2, 1, tt, width), lambda i, j: (0, i, first + j, 0))
    local = pl.BlockSpec((1, tt, width), lambda i, j: (i, j, 0))
    out_specs = [single] * 3 + [double] * 3 + [local] * lat_only
    out_shape = [_sds((b, l_total, width))] * 3 + [_sds((2, b, l_total, width))] * 3 + [_sds((b, l, width))] * lat_only
    return pl.pallas_call(
        kern,
        grid=(b, nt),
        in_specs=data_specs + [full(a) for a in params] + extra_specs,
        out_specs=out_specs,
        out_shape=out_shape,
        input_output_aliases=aliases,
        compiler_params=_cparams(("parallel", "parallel")),
        name="rwkv_prep_lat" if grid_mode else "rwkv_prep_ctx",
    )(*data, *params, *extra)


def _split_bf16(x):
    hi = x.astype(BF16)
    return hi, (x - hi.astype(F32)).astype(BF16)


def _rwkv_chunk_kernel(r_ref, v_ref, kk_ref, w_ref, kd_ref, bb_ref, y_ref, s_ref, *, c, head):
    d = pl.program_id(0)
    nb, _, width = r_ref.shape
    pair = 2 * head
    n_pairs = width // pair

    @pl.when(pl.program_id(2) == 0)
    def _():
        s_ref[...] = jnp.zeros_like(s_ref)

    ri = lax.broadcasted_iota(jnp.int32, (c, c), 0)
    ci = lax.broadcasted_iota(jnp.int32, (c, c), 1)
    before = jnp.where(d == 0, ci - ri, ri - ci) < 0
    incl = before | (ri == ci)
    eye = jnp.where(ri == ci, 1.0, 0.0)
    lmat = jnp.broadcast_to(jnp.where(incl, 1.0, 0.0).astype(BF16), (nb, c, c))

    def bmm(a, b, dims=((2,), (1,))):
        return lax.dot_general(a.astype(BF16), b.astype(BF16), (dims, ((0,), (0,))), preferred_element_type=F32)

    nt = ((2,), (2,))

    ld = jnp.log(w_ref[...])
    ld_hi, ld_lo = _split_bf16(ld)
    cum = bmm(lmat, ld_hi) + bmm(lmat, ld_lo)
    gn = jnp.exp(-cum)
    at = -(jnp.exp(cum - ld) * kk_ref[...])
    rt = jnp.exp(cum) * r_ref[...]
    bt = gn * bb_ref[...]
    kt = gn * kd_ref[...]
    vv = v_ref[...]

    def tiles(x):
        return jnp.stack([x[i, :, p * pair:(p + 1) * pair] for i in range(nb) for p in range(n_pairs)], axis=0)

    def per_head(x, select=None):
        xs = [x, x] if select is None else [jnp.where(select == hh, x, 0.0) for hh in range(2)]
        return jnp.stack(xs, axis=1).reshape((2 * x.shape[0],) + x.shape[1:])

    lane_head = lax.broadcasted_iota(jnp.int32, (1, 1, pair), 2) // head
    at_q, rt_q, bt_q, kt_q, v_q = tiles(at), tiles(rt), tiles(bt), tiles(kt), tiles(vv)
    s0 = s_ref[...]
    as0 = per_head(bmm(at_q, s0))
    rs0 = per_head(bmm(rt_q, s0))
    at_h, rt_h, v_h = per_head(at_q), per_head(rt_q), per_head(v_q)
    bt_h, kt_h = per_head(bt_q, lane_head), per_head(kt_q, lane_head)
    a_ab = jnp.where(before, bmm(at_h, bt_h, nt), 0.0)
    a_ak = jnp.where(before, bmm(at_h, kt_h, nt), 0.0)
    a_rb = jnp.where(incl, bmm(rt_h, bt_h, nt), 0.0)
    a_rk = jnp.where(incl, bmm(rt_h, kt_h, nt), 0.0)
    t = eye + a_ab
    pw = a_ab
    for _ in range(c.bit_length() - 2):
        pw = bmm(pw, pw)
        t = t + bmm(t, pw)
    u = bmm(t, as0 + bmm(a_ak, v_h))
    y = rs0 + bmm(a_rb, u) + bmm(a_rk, v_h)

    def merge(x):
        x = x.reshape((x.shape[0] // 2, 2) + x.shape[1:])
        return jnp.where(lane_head == 0, x[:, 0], x[:, 1])

    u_q, y_q = merge(u), merge(y)
    for i in range(nb):
        for p in range(n_pairs):
            y_ref[i, :, p * pair:(p + 1) * pair] = y_q[i * n_pairs + p]
    tn = ((1,), (1,))
    ones = jnp.ones((nb * n_pairs, c, pair), BF16)
    ds = bmm(bt_q, u_q, tn) + bmm(kt_q, v_q, tn)
    tot = bmm(tiles(ld_hi), ones, tn) + bmm(tiles(ld_lo), ones, tn)
    blk = (lax.broadcasted_iota(jnp.int32, (pair, pair), 0) // head
           == lax.broadcasted_iota(jnp.int32, (pair, pair), 1) // head)
    s_ref[...] = jnp.where(blk, jnp.exp(tot) * (s0 + ds), 0.0)


def _rwkv_chunked(shared, directional, l_ctx, head):
    b, ltot, width = shared[0].shape
    c = RWKV_CHUNK
    nb = RWKV_ROWS if b % RWKV_ROWS == 0 else 1
    nch, ncc = ltot // c, l_ctx // c
    nlc = nch - ncc

    def chunk(d, j):
        back = jnp.where(j < ncc, ncc - 1 - j, nch + ncc - 1 - j)
        return jnp.where(d == 0, j, back)

    def out_chunk(d, j):
        o = chunk(d, j)
        return jnp.where(o >= ncc, o - ncc, jnp.where(d == 0, 0, nlc - 1))

    sh_spec = pl.BlockSpec((nb, c, width), lambda d, i, j: (i, chunk(d, j), 0))
    di_spec = pl.BlockSpec((None, nb, c, width), lambda d, i, j: (d, i, chunk(d, j), 0))
    return pl.pallas_call(
        functools.partial(_rwkv_chunk_kernel, c=c, head=head),
        grid=(2, b // nb, nch),
        in_specs=[sh_spec] * 3 + [di_spec] * 3,
        out_specs=pl.BlockSpec((None, nb, c, width), lambda d, i, j: (d, i, out_chunk(d, j), 0)),
        out_shape=_sds((2, b, (nch - ncc) * c, width)),
        scratch_shapes=[pltpu.VMEM((nb * width // (2 * head), 2 * head, 2 * head), F32)],
        compiler_params=_cparams(("arbitrary", "arbitrary", "arbitrary")),
        name="rwkv_chunk",
    )(*shared, *directional)


def _s5_kernel(x_ref, d_ref, m_ref, pin_ref, pout_ref, ar_ref, ai_ref, y_ref, hall_ref, *, nb, n_chunks, n_ctx_chunks):
    x = x_ref[0]
    xb = x.astype(BF16)
    lat0 = n_ctx_chunks * nb
    xl = xb[lat0:]
    y = d_ref[0] * x[lat0:]
    half = ar_ref.shape[-1] // 2
    fwd = list(range(n_chunks))
    bwd = list(range(n_ctx_chunks - 1, -1, -1)) + list(range(n_chunks - 1, n_ctx_chunks - 1, -1))
    for d, order in enumerate((fwd, bwd)):
        v = _dot(xb, pin_ref[d, 0])
        ar = ar_ref[d, 0]
        ai = ai_ref[d, 0]
        h = jnp.zeros((nb, 2 * half), F32)
        for c in order:
            hall_ref[c * nb:(c + 1) * nb, :] = h
            h = h * ar + pltpu.roll(h, half, 1) * ai + v[c * nb:(c + 1) * nb]
        y = y + _dot(xl, m_ref[d, 0]) + _dot(hall_ref[lat0:, :].astype(BF16), pout_ref[d, 0])
    y_ref[0] = y


def _s5_tables(a_re, a_im, log_dt, b_re, b_im, c_re, c_im, tc):
    a_re, a_im, b_re, b_im = (t.astype(F32) for t in (a_re, a_im, b_re, b_im))
    c_re, c_im = c_re.astype(F32), c_im.astype(F32)
    dt = jnp.exp(log_dt.astype(F32))[..., None]
    mag = jnp.exp(dt * a_re)
    ab_re, ab_im = mag * jnp.cos(dt * a_im), mag * jnp.sin(dt * a_im)
    nr, ni = ab_re - 1.0, ab_im
    den = a_re * a_re + a_im * a_im
    cf_re = (nr * a_re + ni * a_im) / den
    cf_im = (ni * a_re - nr * a_im) / den
    bb_re = cf_re[..., None] * b_re - cf_im[..., None] * b_im
    bb_im = cf_re[..., None] * b_im + cf_im[..., None] * b_re

    k = jnp.arange(tc + 1, dtype=F32)[:, None, None, None]
    pmag = jnp.exp(k * (dt * a_re)[None])
    pang = k * (dt * a_im)[None]
    pw_re, pw_im = pmag * jnp.cos(pang), pmag * jnp.sin(pang)

    ab_k_re = pw_re[..., None] * bb_re[None] - pw_im[..., None] * bb_im[None]
    ab_k_im = pw_re[..., None] * bb_im[None] + pw_im[..., None] * bb_re[None]
    ca_k_re = c_re[None] * pw_re[:, :, :, None, :] - c_im[None] * pw_im[:, :, :, None, :]
    ca_k_im = c_re[None] * pw_im[:, :, :, None, :] + c_im[None] * pw_re[:, :, :, None, :]

    taps = (jnp.einsum('dghp,kdgpj->kdghj', c_re, ab_k_re[:tc], precision=HIGHEST)
            - jnp.einsum('dghp,kdgpj->kdghj', c_im, ab_k_im[:tc], precision=HIGHEST))
    s_idx = jnp.arange(tc)[:, None]
    t_idx = jnp.arange(tc)[None, :]
    hc = b_re.shape[-1]
    g = a_re.shape[1]
    p = a_re.shape[2]

    def toeplitz(tp, lag, valid):
        m = jnp.where(valid[:, :, None, None, None], tp[jnp.clip(lag, 0, tc - 1)], 0.0)
        return m.transpose(2, 0, 4, 1, 3).reshape(g, tc * hc, tc * hc)

    m_f = toeplitz(taps[:, 0], t_idx - s_idx, t_idx >= s_idx)
    m_b = toeplitz(taps[:, 1], s_idx - t_idx, s_idx >= t_idx)
    m = jnp.stack([m_f, m_b]).astype(BF16)

    def pin_of(re, im):
        both = jnp.concatenate([re, im], axis=2)
        return both.transpose(1, 0, 3, 2).reshape(g, tc * hc, 2 * p)

    pin = jnp.stack([pin_of(ab_k_re[:tc, 0][::-1], ab_k_im[:tc, 0][::-1]),
                     pin_of(ab_k_re[:tc, 1], ab_k_im[:tc, 1])]).astype(BF16)

    def pout_of(re, im):
        both = jnp.concatenate([re, -im], axis=3)
        return both.transpose(1, 3, 0, 2).reshape(g, 2 * p, tc * hc)

    pout = jnp.stack([pout_of(ca_k_re[1:tc + 1, 0], ca_k_im[1:tc + 1, 0]),
                      pout_of(ca_k_re[1:tc + 1, 1][::-1], ca_k_im[1:tc + 1, 1][::-1])]).astype(BF16)

    at_re, at_im = pw_re[tc], pw_im[tc]
    ar = jnp.concatenate([at_re, at_re], axis=-1)[:, :, None, :]
    ai = jnp.concatenate([-at_im, at_im], axis=-1)[:, :, None, :]
    return m, pin, pout, ar, ai


def _s5_mix(u_ctx, u_lat, s5_d, tables):
    m, pin, pout, ar, ai = tables
    b, lc, width = u_ctx.shape
    l = u_lat.shape[1]
    g = m.shape[1]
    hc = width // g
    tc = S5_CHUNK
    w = tc * hc
    ncc, nlc = lc // tc, l // tc
    nch = ncc + nlc
    u = jnp.concatenate([u_ctx, u_lat], axis=1)
    xg = u.reshape(b, nch, tc, g, hc).transpose(3, 1, 0, 2, 4).reshape(g, nch * b, w)
    dg = jnp.tile(s5_d.reshape(g, 1, hc), (1, tc, 1)).reshape(g, 1, w)
    pgrp = lambda a: pl.BlockSpec((2, 1) + a.shape[2:], lambda i: (0, i, 0, 0))
    y = pl.pallas_call(
        functools.partial(_s5_kernel, nb=b, n_chunks=nch, n_ctx_chunks=ncc),
        grid=(g,),
        in_specs=[pl.BlockSpec((1, nch * b, w), lambda i: (i, 0, 0)),
                  pl.BlockSpec((1, 1, w), lambda i: (i, 0, 0)),
                  pgrp(m), pgrp(pin), pgrp(pout), pgrp(ar), pgrp(ai)],
        out_specs=pl.BlockSpec((1, nlc * b, w), lambda i: (i, 0, 0)),
        out_shape=_sds((g, nlc * b, w)),
        scratch_shapes=[pltpu.VMEM((nch * b, pin.shape[-1]), F32)],
        compiler_params=_cparams(("parallel",), 56),
        name="s5_mix",
    )(xg, dg, m, pin, pout, ar, ai)
    return y.reshape(g, nlc, b, tc, hc).transpose(2, 1, 3, 0, 4).reshape(b, l, width)


def _gelu(x):
    return 0.5 * x * (1.0 + lax.erf(x * (1.0 / math.sqrt(2.0))))


def _mix_out_kernel(s5_ref, rw_ref, bon_ref, g_ref, x_ref, g1_ref, sh2_ref, sc2_ref, n2_ref,
                    wglu_ref, bglu_ref, lnw_ref, lnb_ref, wos_ref, wor_ref, h1_ref, hn_ref, *, head):
    y1 = _gelu(s5_ref[0])
    s5o = y1 * jax.nn.sigmoid(_dot(y1.astype(BF16), wglu_ref[...]) + bglu_ref[...])
    y = rw_ref[0, 0] + rw_ref[1, 0]
    seg_m = _seg_matrix(y.shape[-1], head)
    inv = 1.0 / head
    mu = _seg_sum(y, seg_m) * inv
    yc = y - mu
    var = _seg_sum(yc * yc, seg_m) * inv
    yn = yc * lax.rsqrt(var + GN_EPS) * lnw_ref[...] + lnb_ref[...]
    rwo = (yn + bon_ref[0]) * g_ref[0]
    mix = _dot(s5o.astype(BF16), wos_ref[...]) + _dot(rwo.astype(BF16), wor_ref[...])
    h1 = x_ref[0] + g1_ref[0] * mix
    h1_ref[0] = h1
    ms = jnp.mean(h1 * h1, axis=-1, keepdims=True)
    hn_ref[0] = (h1 * lax.rsqrt(ms + NORM_EPS) * n2_ref[...]) * (1.0 + sc2_ref[0]) + sh2_ref[0]


def _mix_out(s5y, rwy, bonus, gate, x, g1, sh2, sc2, n2g, w_glu, b_glu, ln_w, ln_b, w_out_s, w_out_r, head):
    b, l, d = x.shape
    width = s5y.shape[-1]
    tm = min(512, l)
    tok = lambda c: pl.BlockSpec((1, tm, c), lambda i, j: (i, j, 0))
    mod = pl.BlockSpec((1, 1, d), lambda i, j: (i, 0, 0))
    full = lambda a: pl.BlockSpec(a.shape, lambda i, j: (0,) * a.ndim)
    consts = (n2g, w_glu, b_glu, ln_w, ln_b, w_out_s, w_out_r)
    return pl.pallas_call(
        functools.partial(_mix_out_kernel, head=head),
        grid=(b, l // tm),
        in_specs=[tok(width), pl.BlockSpec((2, 1, tm, width), lambda i, j: (0, i, j, 0)), tok(width), tok(width)]
        + [tok(d), mod, mod, mod] + [full(a) for a in consts],
        out_specs=[tok(d), tok(d)],
        out_shape=[_sds((b, l, d)), _sds((b, l, d))],
        compiler_params=_cparams(("parallel", "parallel")),
        name="mix_out",
    )(s5y, rwy, bonus, gate, x, g1, sh2, sc2, *consts)


def _staircase(k):
    return [(a, b) for a in range(k) for b in range(k) if (a + 1) * (b + 1) <= k]


def _topk_rows(s, k):
    rows = s.shape[0]
    iota = lax.broadcasted_iota(jnp.int32, s.shape, 0)
    vals, idxs = [], []
    for _ in range(k):
        m = jnp.max(s, axis=0, keepdims=True)
        idx = jnp.min(jnp.where(s == m, iota, rows), axis=0, keepdims=True)
        vals.append(m)
        idxs.append(idx)
        s = jnp.where(iota == idx, -jnp.inf, s)
    return jnp.concatenate(vals, axis=0), jnp.concatenate(idxs, axis=0).astype(F32)


def _peer_topk_kernel(hn_ref, wq_ref, keys_ref, sel1_ref, sel2_ref, pad_ref, off_ref, gate_ref, *, heads, nkeys, half,
                      rows_per_expert):
    hb = hn_ref[...].astype(BF16)
    nt = (((1,), (1,)), ((), ()))
    sel1 = sel1_ref[...]
    sel2 = sel2_ref[...]
    pick = lambda sel, a: jnp.dot(sel, a, precision=HIGHEST, preferred_element_type=F32)
    ids, gates = [], []
    for h in range(heads):
        top = []
        for c in range(2):
            row0 = (h * 2 + c) * half
            qt = lax.dot_general(wq_ref[row0:row0 + half, :], hb, nt, preferred_element_type=F32)
            st = _dot(keys_ref[h * 2 + c], qt.astype(BF16))
            top.append(_topk_rows(st, PEER_TOPK))
        (s1, i1), (s2, i2) = top
        cand = pick(sel1, s1) + pick(sel2, s2) + pad_ref[...]
        cid = pick(sel1, i1) * nkeys + pick(sel2, i2)
        best, pos = _topk_rows(cand, PEER_TOPK)
        riota = lax.broadcasted_iota(jnp.int32, cid.shape, 0).astype(F32)
        eid = jnp.concatenate(
            [jnp.sum(jnp.where(riota == pos[r:r + 1], cid, 0.0), axis=0, keepdims=True) for r in range(PEER_TOPK)], axis=0)
        e = jnp.exp(best - jnp.max(best, axis=0, keepdims=True))
        gates.append(e / jnp.sum(e, axis=0, keepdims=True))
        ids.append(eid)
    eid = jnp.concatenate(ids, axis=0).T.astype(jnp.int32)
    off_ref[...] = eid * rows_per_expert
    gate_ref[...] = jnp.concatenate(gates, axis=0).T


def _peer_topk(hn, wq_t, keys, rows_per_expert):
    n, d = hn.shape
    heads = keys.shape[0] // 2
    nkeys, half = keys.shape[1], keys.shape[2]
    tm = min(256, n)
    pairs = _staircase(PEER_TOPK)
    rows = -(-len(pairs) // SUBLANES) * SUBLANES
    sel1 = np.zeros((rows, PEER_TOPK), np.float32)
    sel2 = np.zeros((rows, PEER_TOPK), np.float32)
    pad = np.zeros((rows, 1), np.float32)
    for i, (a, b) in enumerate(pairs):
        sel1[i, a] = 1.0
        sel2[i, b] = 1.0
    pad[len(pairs):] = -np.inf
    slots = heads * PEER_TOPK
    full = lambda a: pl.BlockSpec(a.shape, lambda i: (0,) * a.ndim)
    consts = (wq_t, keys, jnp.asarray(sel1), jnp.asarray(sel2), jnp.asarray(pad))
    return pl.pallas_call(
        functools.partial(_peer_topk_kernel, heads=heads, nkeys=nkeys, half=half, rows_per_expert=rows_per_expert),
        grid=(n // tm,),
        in_specs=[pl.BlockSpec((tm, d), lambda i: (i, 0))] + [full(a) for a in consts],
        out_specs=[pl.BlockSpec((tm, slots), lambda i: (i, 0))] * 2,
        out_shape=[_sds((n, slots), jnp.int32), _sds((n, slots))],
        compiler_params=_cparams(("parallel",)),
        name="peer_topk",
    )(hn, *consts)


def _pack_table(tab):
    e, d = tab.shape
    half = d // 2
    bits = lax.bitcast_convert_type(tab.astype(BF16), jnp.uint16).astype(jnp.uint32)
    packed = ((bits[:, :half] << 16) | bits[:, half:]).reshape(e * (half // LANES), LANES)
    return jnp.pad(packed, ((0, SUBLANES), (0, 0)))


def _expert_halves(tab_ref, off):
    bits = tab_ref[pl.ds(off, SUBLANES), :]
    return pltpu.bitcast(bits & jnp.uint32(0xFFFF0000), F32), pltpu.bitcast(bits << 16, F32)


def _peer_u_kernel(off_ref, h_ref, gate_ref, tab_ref, w_ref, q_ref, r_ref, *, tq, slots):
    half_rows = SUBLANES // 2
    sub = lax.broadcasted_iota(jnp.int32, (SUBLANES, LANES), 0)
    low = sub < half_rows
    groups = slots // SUBLANES
    qrows = slots * half_rows

    ones = jnp.ones((SUBLANES, LANES), BF16)
    nt = (((1,), (1,)), ((), ()))

    tg = PEER_U_GROUP
    ngroups = tq // tg

    def lane_sums(g):
        start = g * tg * qrows if isinstance(g, int) else pl.multiple_of(g * (tg * qrows), tg * qrows)
        sums = lax.dot_general(ones, q_ref[pl.ds(start, tg * qrows), :], nt, preferred_element_type=F32)
        for i in range(tg):
            r_ref[pl.ds(g * tg + i, 1), :] = sums[0:1, i * qrows:(i + 1) * qrows]

    q_ref[(tq - tg) * qrows:, :] = jnp.zeros((tg * qrows, LANES), BF16)

    def token_group(g, carry):
        lane_sums(jnp.where(g == 0, ngroups - 1, g - 1))
        for i in range(tg):
            token(g * tg + i)
        return carry

    def token(t):
        h = h_ref[t]
        h_first = pltpu.bitcast(jnp.where(low, h, 0.0).astype(BF16).astype(F32), jnp.uint32)
        h_second = pltpu.bitcast(jnp.where(low, pltpu.roll(h, half_rows, 0), 0.0).astype(BF16).astype(F32), jnp.uint32)
        h_pair = pltpu.bitcast(h_first | (h_second >> 16), BF16)
        q_tok = q_ref.at[pl.ds(pl.multiple_of(t * qrows, qrows), qrows)]
        grows = SUBLANES * half_rows
        for g8 in range(groups):
            base = t * slots + g8 * SUBLANES
            ps = []
            for r in range(SUBLANES):
                bits = tab_ref[pl.ds(off_ref[0, 0, base + r], SUBLANES), :]
                prod = pltpu.bitcast(pltpu.bitcast(bits, BF16) * h_pair, jnp.uint32)
                ps.append(pltpu.bitcast(prod & jnp.uint32(0xFFFF0000), F32) + pltpu.bitcast(prod << 16, F32))
            pairs = [ps[i] + pltpu.roll(ps[i + 1], half_rows, 0) for i in range(0, SUBLANES, 2)]
            q_tok[g8 * grows:(g8 + 1) * grows, :] = jnp.concatenate(pairs, axis=0).astype(BF16)

    lax.fori_loop(0, ngroups, token_group, 0)
    lane_sums(ngroups - 1)

    fold = (lax.broadcasted_iota(jnp.int32, (qrows, slots), 0) // half_rows
            == lax.broadcasted_iota(jnp.int32, (qrows, slots), 1))
    fold = jnp.where(fold, 1.0, 0.0).astype(BF16)
    rs = r_ref[...]
    rh = rs.astype(BF16)
    rl = (rs - rh.astype(F32)).astype(BF16)
    act = _dot(rh, fold) + _dot(rl, fold)
    w_ref[...] = gate_ref[...] * _gelu(act)


def _smem_block(tq, slots):
    return pl.BlockSpec((1, 1, tq * slots), lambda i: (i, 0, 0), memory_space=pltpu.SMEM)


def _peer_u(off_blocks, hn3, gate, tab):
    n, rows, lanes = hn3.shape
    slots = gate.shape[1]
    tq = off_blocks.shape[-1] // slots
    return pl.pallas_call(
        functools.partial(_peer_u_kernel, tq=tq, slots=slots),
        grid=(n // tq,),
        in_specs=[_smem_block(tq, slots),
                  pl.BlockSpec((tq, rows, lanes), lambda i: (i, 0, 0)),
                  pl.BlockSpec((tq, slots), lambda i: (i, 0)),
                  pl.BlockSpec(tab.shape, lambda i: (0, 0), pipeline_mode=pl.Buffered(1))],
        out_specs=pl.BlockSpec((tq, slots), lambda i: (i, 0)),
        out_shape=_sds((n, slots)),
        scratch_shapes=[pltpu.VMEM((tq * slots * (SUBLANES // 2), lanes), BF16),
                        pltpu.VMEM((tq, slots * (SUBLANES // 2)), F32)],
        compiler_params=_cparams(("parallel",), 52),
        name="peer_u",
    )(off_blocks, hn3, gate, tab)


def _peer_v_kernel(off_ref, w_ref, tab_ref, o_ref, ws_ref, dg_ref, ot_ref, *, tq, slots):
    sub = lax.broadcasted_iota(jnp.int32, (SUBLANES, LANES), 0)
    low = sub < SUBLANES // 2

    eye = lax.broadcasted_iota(jnp.int32, (slots, slots), 0) == lax.broadcasted_iota(jnp.int32, (slots, slots), 1)

    def diag(t, carry):
        wrow = jnp.broadcast_to(w_ref[pl.ds(t, 1), :], (slots, slots))
        dg_ref[pl.ds(pl.multiple_of(t * slots, slots), slots), :] = jnp.where(eye, wrow, 0.0).astype(BF16)
        return carry

    lax.fori_loop(0, tq, diag, 0, unroll=4)
    ws_ref[...] = _dot(dg_ref[...], jnp.ones((slots, LANES), BF16))

    def token(t, carry):
        zero = jnp.zeros((SUBLANES, LANES), F32)
        first, second = [zero, zero], [zero, zero]
        for s in range(slots):
            hi, lo = _expert_halves(tab_ref, off_ref[0, 0, t * slots + s])
            w = jnp.broadcast_to(ws_ref[pl.ds(t * slots + s, 1), :], (SUBLANES, LANES))
            first[s % 2] = first[s % 2] + hi * w
            second[s % 2] = second[s % 2] + lo * w
        tile = jnp.where(low, first[0] + first[1], pltpu.roll(second[0] + second[1], SUBLANES // 2, 0))
        ot_ref[pl.ds(pl.multiple_of(t * SUBLANES, SUBLANES), SUBLANES), :] = tile
        return carry

    lax.fori_loop(0, tq, token, 0)
    for j in range(SUBLANES):
        o_ref[:, j * LANES:(j + 1) * LANES] = ot_ref[pl.ds(j, tq, stride=SUBLANES), :]


def _peer_v(off_blocks, w, tab):
    n, slots = w.shape
    tq = off_blocks.shape[-1] // slots
    lanes = tab.shape[-1]
    assert slots == lanes, "the weight splat uses one (slots, lanes) identity tile"
    return pl.pallas_call(
        functools.partial(_peer_v_kernel, tq=tq, slots=slots),
        grid=(n // tq,),
        in_specs=[_smem_block(tq, slots),
                  pl.BlockSpec((tq, slots), lambda i: (i, 0)),
                  pl.BlockSpec(tab.shape, lambda i: (0, 0), pipeline_mode=pl.Buffered(1))],
        out_specs=pl.BlockSpec((tq, VREG_WORDS), lambda i: (i, 0)),
        out_shape=_sds((n, VREG_WORDS)),
        scratch_shapes=[pltpu.VMEM((tq * slots, lanes), F32), pltpu.VMEM((tq * slots, slots), BF16),
                        pltpu.VMEM((tq * SUBLANES, lanes), F32)],
        compiler_params=_cparams(("parallel",), 52),
        name="peer_v",
    )(off_blocks, w, tab)


def _final_kernel(h1_ref, p_ref, g2_ref, nf_ref, o_ref):
    h = h1_ref[0] + g2_ref[0] * p_ref[0]
    ms = jnp.mean(h * h, axis=-1, keepdims=True)
    o_ref[0] = h * lax.rsqrt(ms + NORM_EPS) * nf_ref[...]


def _final_norm(h1, peer, g2, nfg):
    b, l, d = h1.shape
    tm = min(512, l)
    tok = pl.BlockSpec((1, tm, d), lambda i, j: (i, j, 0))
    return pl.pallas_call(
        _final_kernel,
        grid=(b, l // tm),
        in_specs=[tok, tok, pl.BlockSpec((1, 1, d), lambda i, j: (i, 0, 0)), pl.BlockSpec((1, d), lambda i, j: (0, 0))],
        out_specs=tok,
        out_shape=_sds((b, l, d)),
        compiler_params=_cparams(("parallel", "parallel")),
        name="final_norm",
    )(h1, peer, g2, nfg)


def _pad_cols(a, n):
    return jnp.pad(a, [(0, 0)] * (a.ndim - 1) + [(0, n - a.shape[-1])])


def _layer(h, hc, c, c_ctx, p, norm_f_g):
    b, l, d = h.shape
    lc = hc.shape[1]
    s5w = p['s5_d'].shape[-1]
    rww = p['rw_w0'].shape[-1]
    heads, head = p['rw_r_k'].shape
    n_dec, n_aaa, n_gate = p['rw_w_w2'].shape[1], p['rw_w_a2'].shape[1], p['rw_w_g2'].shape[0]
    n_lora = n_dec + n_aaa + n_gate
    lora_pad = -(-n_lora // LANES) * LANES

    rows = -(-(b + 1) // SUBLANES) * SUBLANES
    cc = jnp.zeros((rows, d), F32).at[:b].set(c).at[b].set(c_ctx)
    mods = _ada_mods(cc, p['w_ada'], p['b_ada'])
    sh1, sc1, g1, sh2, sc2, g2 = [m.reshape(rows, 1, d) for m in jnp.split(mods, 6, axis=-1)]
    lat = lambda m: m[:b]
    ctxm = lambda m: m[b:b + 1]

    w_in = p['w_in']
    w_s5 = w_in[:, :s5w].astype(BF16)
    w_main = w_in[:, s5w:s5w + 3 * rww].astype(BF16)
    w_lora = _pad_cols(w_in[:, s5w + 3 * rww:], lora_pad).astype(BF16)
    n1g = p['norm1_g'].reshape(1, d)
    u_l, zm_l, zl_l = _in_proj(h, lat(sh1), lat(sc1), n1g, w_s5, w_main, w_lora)
    u_c, zm_c, zl_c = _in_proj(hc, ctxm(sh1), ctxm(sc1), n1g, w_s5, w_main, w_lora)

    mu = p['rw_mu']
    mu_main = mu[:3 * rww].reshape(1, -1)
    mu_lora = _pad_cols(mu[3 * rww:], lora_pad).reshape(1, -1)
    place = lambda w, r0: jnp.zeros(w.shape[:-2] + (lora_pad, rww), F32).at[..., r0:r0 + w.shape[-2], :].set(w).astype(BF16)
    rw_params = (place(p['rw_w_g2'], n_dec + n_aaa), place(p['rw_w_w2'], 0), place(p['rw_w_a2'], n_dec),
                 p['rw_w0'].reshape(2, 1, rww), p['rw_a0'].reshape(2, 1, rww),
                 p['rw_k_k'].reshape(1, rww), p['rw_k_a'].reshape(1, rww), p['rw_r_k'].reshape(1, rww))
    prep_params = (mu_main, mu_lora) + rw_params
    ctx_streams = _rwkv_prep(zm_c, zl_c, prep_params, head, lc + l)
    *streams, gate_l, bonus_l = _rwkv_prep(zm_l, zl_l, prep_params, head, lc + l, ctx_streams)
    rw_y = _rwkv_chunked(tuple(streams[:3]), tuple(streams[3:]), lc, head)

    tables = _s5_tables(p['s5_a_re'], p['s5_a_im'], p['s5_log_dt'], p['s5_b_re'], p['s5_b_im'],
                        p['s5_c_re'], p['s5_c_im'], S5_CHUNK)
    s5_y = _s5_mix(u_c, u_l, p['s5_d'], tables)

    w_out = p['w_out']
    h1, hn2 = _mix_out(s5_y, rw_y, bonus_l, gate_l, h, lat(g1), lat(sh2), lat(sc2), p['norm2_g'].reshape(1, d),
                       p['s5_w_glu'].astype(BF16), p['s5_b_glu'].reshape(1, s5w),
                       p['rw_ln_w'].reshape(1, rww), p['rw_ln_b'].reshape(1, rww),
                       w_out[:s5w].astype(BF16), w_out[s5w:].astype(BF16), head)

    n = b * l
    keys = p['peer_keys']
    pheads, _, nkeys, half = keys.shape
    assert d == VREG_WORDS, "peer_u / peer_v hold one token's features in a single (8, 128) tile"
    off, gate = _peer_topk(hn2.reshape(n, d), p['peer_w_q'].T.astype(BF16),
                           keys.reshape(pheads * 2, nkeys, half).astype(BF16), d // (2 * LANES))
    slots = gate.shape[1]
    tq = min(PEER_TILE, n)
    blocks = lambda a: a.reshape(n // tq, 1, tq * slots)
    w = _peer_u(blocks(off), hn2.reshape(n, SUBLANES, LANES), gate, _pack_table(p['peer_u']))
    peer = _peer_v(blocks(off), w, _pack_table(p['peer_v']))

    return _final_norm(h1, peer.reshape(b, l, d), lat(g2), norm_f_g.reshape(1, d))


def kernel(x, c, ctx, c_ctx, w_ada, b_ada, norm1_g, norm2_g, w_in, s5_a_re, s5_a_im, s5_log_dt, s5_b_re, s5_b_im, s5_c_re, s5_c_im, s5_d, s5_w_glu, s5_b_glu, rw_mu, rw_w0, rw_w_w2, rw_a0, rw_w_a2, rw_w_g2, rw_k_k, rw_k_a, rw_r_k, rw_ln_w, rw_ln_b, w_out, peer_w_q, peer_keys, peer_u, peer_v, norm_f_g):
    assert w_ada.shape[0] == 1, "single-layer block: the context stream is never updated"
    p = dict(w_ada=w_ada[0], b_ada=b_ada[0], norm1_g=norm1_g[0], norm2_g=norm2_g[0], w_in=w_in[0],
             s5_a_re=s5_a_re[0], s5_a_im=s5_a_im[0], s5_log_dt=s5_log_dt[0], s5_b_re=s5_b_re[0], s5_b_im=s5_b_im[0],
             s5_c_re=s5_c_re[0], s5_c_im=s5_c_im[0], s5_d=s5_d[0], s5_w_glu=s5_w_glu[0], s5_b_glu=s5_b_glu[0],
             rw_mu=rw_mu[0], rw_w0=rw_w0[0], rw_w_w2=rw_w_w2[0], rw_a0=rw_a0[0], rw_w_a2=rw_w_a2[0],
             rw_w_g2=rw_w_g2[0], rw_k_k=rw_k_k[0], rw_k_a=rw_k_a[0], rw_r_k=rw_r_k[0], rw_ln_w=rw_ln_w[0],
             rw_ln_b=rw_ln_b[0], w_out=w_out[0], peer_w_q=peer_w_q[0], peer_keys=peer_keys[0],
             peer_u=peer_u[0], peer_v=peer_v[0])
    return _layer(x, ctx, c, c_ctx, p, norm_f_g)
```

```python
import functools
import math

import jax
import jax.numpy as jnp
import numpy as np
from jax import lax
from jax.experimental import pallas as pl
from jax.experimental.pallas import tpu as pltpu

F32 = jnp.float32
BF16 = jnp.bfloat16
HIGHEST = lax.Precision.HIGHEST

SUBLANES = 8
LANES = 128
VREG_WORDS = SUBLANES * LANES

GRID_W = 64
NORM_EPS = 1e-6
GN_EPS = 64e-5
KK_EPS = 1e-12
PEER_TOPK = 16

S5_CHUNK = 32
RWKV_CHUNK = 64
RWKV_ROWS = 4
PEER_TILE = 64
PEER_U_GROUP = 4


def _dot(a, b):
    return jnp.dot(a, b, preferred_element_type=F32)


def _cparams(sem, vmem_mb=48):
    return pltpu.CompilerParams(dimension_semantics=sem, vmem_limit_bytes=vmem_mb * 1024 * 1024)


def _sds(shape, dtype=F32):
    return jax.ShapeDtypeStruct(shape, dtype)


def _mods_kernel(c_ref, w_ref, b_ref, o_ref):
    c = c_ref[...]
    s = c * jax.nn.sigmoid(c)
    o_ref[...] = jnp.dot(s, w_ref[...], precision=HIGHEST, preferred_element_type=F32) + b_ref[...]


def _ada_mods(cc, w_ada, b_ada):
    rows, d = cc.shape
    n = w_ada.shape[1]
    bn = d if n % d == 0 else n
    return pl.pallas_call(
        _mods_kernel,
        grid=(n // bn,),
        in_specs=[pl.BlockSpec((rows, d), lambda j: (0, 0)),
                  pl.BlockSpec((d, bn), lambda j: (0, j)),
                  pl.BlockSpec((1, bn), lambda j: (0, j))],
        out_specs=pl.BlockSpec((rows, bn), lambda j: (0, j)),
        out_shape=_sds((rows, n)),
        compiler_params=_cparams(("arbitrary",)),
        name="ada_mods",
    )(cc, w_ada, b_ada.reshape(1, n))


def _inproj_kernel(x_ref, sh_ref, sc_ref, g_ref, ws_ref, wz_ref, wl_ref, u_ref, z_ref, l_ref):
    x = x_ref[0]
    ms = jnp.mean(x * x, axis=-1, keepdims=True)
    hn = (x * lax.rsqrt(ms + NORM_EPS) * g_ref[...]) * (1.0 + sc_ref[0]) + sh_ref[0]
    hb = hn.astype(BF16)
    u_ref[0] = _dot(hb, ws_ref[...])
    z_ref[0] = _dot(hb, wz_ref[...])
    l_ref[0] = _dot(hb, wl_ref[...])


def _in_proj(x, shift, scale, gain, w_s5, w_main, w_lora):
    b, l, d = x.shape
    tm = min(512, l)
    bm = shift.shape[0]
    mod_map = (lambda i, j: (i, 0, 0)) if bm > 1 else (lambda i, j: (0, 0, 0))
    full = lambda a: pl.BlockSpec(a.shape, lambda i, j: (0,) * a.ndim)
    return pl.pallas_call(
        _inproj_kernel,
        grid=(b, l // tm),
        in_specs=[pl.BlockSpec((1, tm, d), lambda i, j: (i, j, 0)),
                  pl.BlockSpec((1, 1, d), mod_map), pl.BlockSpec((1, 1, d), mod_map),
                  full(gain), full(w_s5), full(w_main), full(w_lora)],
        out_specs=[pl.BlockSpec((1, tm, w.shape[1]), lambda i, j: (i, j, 0)) for w in (w_s5, w_main, w_lora)],
        out_shape=[_sds((b, l, w.shape[1])) for w in (w_s5, w_main, w_lora)],
        compiler_params=_cparams(("parallel", "parallel")),
        name="in_proj",
    )(x, shift, scale, gain, w_s5, w_main, w_lora)


def _seg_matrix(width, seg):
    r = lax.broadcasted_iota(jnp.int32, (width, width), 0) // seg
    c = lax.broadcasted_iota(jnp.int32, (width, width), 1) // seg
    return jnp.where(r == c, 1.0, 0.0).astype(BF16)


def _seg_sum(x, seg_m):
    hi = x.astype(BF16)
    lo = (x - hi.astype(F32)).astype(BF16)
    return _dot(hi, seg_m) + _dot(lo, seg_m)


def _rwkv_streams(zz, zl, p, head, outs, with_gate):
    (wg_ref, ww_ref, wa_ref, w0_ref, a0_ref, kk_ref, ka_ref, rk_ref) = p
    width = kk_ref.shape[-1]
    r = zz[:, :width]
    k = zz[:, width:2 * width]
    v = zz[:, 2 * width:3 * width]
    seg_m = _seg_matrix(width, head)
    kk = k * kk_ref[...]
    kk = kk * lax.rsqrt(_seg_sum(kk * kk, seg_m) + KK_EPS)
    outs["r"][0] = r
    outs["v"][0] = v
    outs["kk"][0] = kk
    tl = jnp.tanh(zl).astype(BF16)
    zb = zl.astype(BF16)
    bonus = None
    for d in range(2):
        outs["dec"][d, 0] = -math.exp(-0.5) * jax.nn.sigmoid(w0_ref[d] + _dot(tl, ww_ref[d]))
        a = jax.nn.sigmoid(a0_ref[d] + _dot(zb, wa_ref[d]))
        kd = k * (1.0 + (a - 1.0) * ka_ref[...])
        outs["kd"][d, 0] = kd
        outs["bb"][d, 0] = kk * a
        if with_gate:
            bd = _seg_sum(r * kd * rk_ref[...], seg_m) * v
            bonus = bd if bonus is None else bonus + bd
    if with_gate:
        outs["bonus"][0] = bonus
        outs["g"][0] = _dot(jax.nn.sigmoid(zl).astype(BF16), wg_ref[...])


def _lerp_shift(z, shifted, mu):
    return z + (shifted - z) * mu


def _grid_shift(zc, zp, zn, has_prev, has_next):
    t, c = zc.shape
    tok = lax.broadcasted_iota(jnp.int32, (t, 1), 0) % GRID_W
    col = lax.broadcasted_iota(jnp.int32, (1, c), 1) % 4
    left = jnp.where(tok == 0, 0.0, pltpu.roll(zc, 1, 0))
    right = jnp.where(tok == GRID_W - 1, 0.0, pltpu.roll(zc, t - 1, 0))
    zp = jnp.where(has_prev, zp, 0.0)
    zn = jnp.where(has_next, zn, 0.0)
    if t > GRID_W:
        up = jnp.concatenate([zp, zc[:t - GRID_W]], axis=0)
        down = jnp.concatenate([zc[GRID_W:], zn], axis=0)
    else:
        up, down = zp, zn
    return jnp.where(col == 0, left, jnp.where(col == 1, right, jnp.where(col == 2, up, down)))


def _seq_shift(zc):
    t, c = zc.shape
    tok = lax.broadcasted_iota(jnp.int32, (t, 1), 0)
    col = lax.broadcasted_iota(jnp.int32, (1, c), 1) % 2
    prev = jnp.where(tok == 0, 0.0, pltpu.roll(zc, 1, 0))
    nxt = jnp.where(tok == t - 1, 0.0, pltpu.roll(zc, t - 1, 0))
    return jnp.where(col == 0, prev, nxt)


_STREAMS = ("r", "v", "kk", "dec", "kd", "bb")


def _rwkv_prep_lat_kernel(zc_ref, zp_ref, zn_ref, lc_ref, lp_ref, ln_ref, mum_ref, mul_ref, *rest, head):
    p, o = rest[:8], rest[8 + len(_STREAMS):]
    outs = dict(zip(_STREAMS + ("g", "bonus"), o))
    j = pl.program_id(1)
    has_prev = j > 0
    has_next = j < pl.num_programs(1) - 1
    zc = zc_ref[0]
    lc = lc_ref[0]
    zz = _lerp_shift(zc, _grid_shift(zc, zp_ref[0], zn_ref[0], has_prev, has_next), mum_ref[...])
    zl = _lerp_shift(lc, _grid_shift(lc, lp_ref[0], ln_ref[0], has_prev, has_next), mul_ref[...])
    _rwkv_streams(zz, zl, p, head, outs, True)


def _rwkv_prep_ctx_kernel(zc_ref, lc_ref, mum_ref, mul_ref, *rest, head):
    p, o = rest[:8], rest[8:]
    outs = dict(zip(_STREAMS, o))
    zc = zc_ref[0]
    lc = lc_ref[0]
    zz = _lerp_shift(zc, _seq_shift(zc), mum_ref[...])
    zl = _lerp_shift(lc, _seq_shift(lc), mul_ref[...])
    _rwkv_streams(zz, zl, p, head, outs, False)


def _rwkv_prep(z_main, z_lora, params, head, l_total, ctx_streams=None):
    b, l, cm = z_main.shape
    cl = z_lora.shape[-1]
    width = cm // 3
    full = lambda a: pl.BlockSpec(a.shape, lambda i, j: (0,) * a.ndim)
    grid_mode = ctx_streams is not None
    if grid_mode:
        l_ctx = l_total - l
        rows_per_tile = next(r for r in (8, 4, 2, 1) if l % (r * GRID_W) == 0 and l_ctx % (r * GRID_W) == 0)
        tt = rows_per_tile * GRID_W
        nt = l // tt
        first = l_ctx // tt
        last_row = l // GRID_W - 1
        cur = lambda c: pl.BlockSpec((1, tt, c), lambda i, j: (i, j, 0))
        prev = lambda c: pl.BlockSpec((1, GRID_W, c), lambda i, j: (i, jnp.maximum(j * rows_per_tile - 1, 0), 0))
        nxt = lambda c: pl.BlockSpec((1, GRID_W, c), lambda i, j: (i, jnp.minimum((j + 1) * rows_per_tile, last_row), 0))
        data = [z_main, z_main, z_main, z_lora, z_lora, z_lora]
        data_specs = [cur(cm), prev(cm), nxt(cm), cur(cl), prev(cl), nxt(cl)]
        kern = functools.partial(_rwkv_prep_lat_kernel, head=head)
        extra = list(ctx_streams)
        extra_specs = [pl.BlockSpec(memory_space=pl.ANY)] * len(extra)
        n_in = len(data) + len(params)
        aliases = {n_in + k: k for k in range(len(extra))}
        lat_only = 2
    else:
        tt, nt, first = l, 1, 0
        data = [z_main, z_lora]
        data_specs = [pl.BlockSpec((1, tt, cm), lambda i, j: (i, j, 0)), pl.BlockSpec((1, tt, cl), lambda i, j: (i, j, 0))]
        kern = functools.partial(_rwkv_prep_ctx_kernel, head=head)
        extra, extra_specs, aliases, lat_only = [], [], {}, 0
    single = pl.BlockSpec((1, tt, width), lambda i, j: (i, first + j, 0))
    double = pl.BlockSpec((2, 1, tt, width), lambda i, j: (0, i, first + j, 0))
    local = pl.BlockSpec((1, tt, width), lambda i, j: (i, j, 0))
    out_specs = [single] * 3 + [double] * 3 + [local] * lat_only
    out_shape = [_sds((b, l_total, width))] * 3 + [_sds((2, b, l_total, width))] * 3 + [_sds((b, l, width))] * lat_only
    return pl.pallas_call(
        kern,
        grid=(b, nt),
        in_specs=data_specs + [full(a) for a in params] + extra_specs,
        out_specs=out_specs,
        out_shape=out_shape,
        input_output_aliases=aliases,
        compiler_params=_cparams(("parallel", "parallel")),
        name="rwkv_prep_lat" if grid_mode else "rwkv_prep_ctx",
    )(*data, *params, *extra)


def _split_bf16(x):
    hi = x.astype(BF16)
    return hi, (x - hi.astype(F32)).astype(BF16)


def _rwkv_chunk_kernel(r_ref, v_ref, kk_ref, w_ref, kd_ref, bb_ref, y_ref, s_ref, *, c, head):
    d = pl.program_id(0)
    nb, _, width = r_ref.shape
    pair = 2 * head
    n_pairs = width // pair

    @pl.when(pl.program_id(2) == 0)
    def _():
        s_ref[...] = jnp.zeros_like(s_ref)

    ri = lax.broadcasted_iota(jnp.int32, (c, c), 0)
    ci = lax.broadcasted_iota(jnp.int32, (c, c), 1)
    before = jnp.where(d == 0, ci - ri, ri - ci) < 0
    incl = before | (ri == ci)
    eye = jnp.where(ri == ci, 1.0, 0.0)
    lmat = jnp.broadcast_to(jnp.where(incl, 1.0, 0.0).astype(BF16), (nb, c, c))

    def bmm(a, b, dims=((2,), (1,))):
        return lax.dot_general(a.astype(BF16), b.astype(BF16), (dims, ((0,), (0,))), preferred_element_type=F32)

    nt = ((2,), (2,))

    ld = w_ref[...]
    ld_hi, ld_lo = _split_bf16(ld)
    cum = bmm(lmat, ld_hi) + bmm(lmat, ld_lo)
    gn = jnp.exp(-cum)
    at = -(jnp.exp(cum - ld) * kk_ref[...])
    rt = jnp.exp(cum) * r_ref[...]
    bt = gn * bb_ref[...]
    kt = gn * kd_ref[...]
    vv = v_ref[...]

    def tiles(x):
        return jnp.stack([x[i, :, p * pair:(p + 1) * pair] for i in range(nb) for p in range(n_pairs)], axis=0)

    def per_head(x, select=None):
        xs = [x, x] if select is None else [jnp.where(select == hh, x, 0.0) for hh in range(2)]
        return jnp.stack(xs, axis=1).reshape((2 * x.shape[0],) + x.shape[1:])

    lane_head = lax.broadcasted_iota(jnp.int32, (1, 1, pair), 2) // head
    at_q, rt_q, bt_q, kt_q, v_q = tiles(at), tiles(rt), tiles(bt), tiles(kt), tiles(vv)
    s0 = s_ref[...]
    as0 = per_head(bmm(at_q, s0))
    rs0 = per_head(bmm(rt_q, s0))
    at_h, rt_h, v_h = per_head(at_q), per_head(rt_q), per_head(v_q)
    bt_h, kt_h = per_head(bt_q, lane_head), per_head(kt_q, lane_head)
    a_ab = jnp.where(before, bmm(at_h, bt_h, nt), 0.0)
    a_ak = jnp.where(before, bmm(at_h, kt_h, nt), 0.0)
    a_rb = jnp.where(incl, bmm(rt_h, bt_h, nt), 0.0)
    a_rk = jnp.where(incl, bmm(rt_h, kt_h, nt), 0.0)
    t = eye + a_ab
    pw = a_ab
    for _ in range(c.bit_length() - 2):
        pw = bmm(pw, pw)
        t = t + bmm(t, pw)
    u = bmm(t, as0 + bmm(a_ak, v_h))
    y = rs0 + bmm(a_rb, u) + bmm(a_rk, v_h)

    def merge(x):
        x = x.reshape((x.shape[0] // 2, 2) + x.shape[1:])
        return jnp.where(lane_head == 0, x[:, 0], x[:, 1])

    u_q, y_q = merge(u), merge(y)
    for i in range(nb):
        for p in range(n_pairs):
            y_ref[i, :, p * pair:(p + 1) * pair] = y_q[i * n_pairs + p]
    tn = ((1,), (1,))
    ones = jnp.ones((nb * n_pairs, c, pair), BF16)
    ds = bmm(bt_q, u_q, tn) + bmm(kt_q, v_q, tn)
    tot = bmm(tiles(ld_hi), ones, tn) + bmm(tiles(ld_lo), ones, tn)
    blk = (lax.broadcasted_iota(jnp.int32, (pair, pair), 0) // head
           == lax.broadcasted_iota(jnp.int32, (pair, pair), 1) // head)
    s_ref[...] = jnp.where(blk, jnp.exp(tot) * (s0 + ds), 0.0)


def _rwkv_chunked(shared, directional, l_ctx, head):
    b, ltot, width = shared[0].shape
    c = RWKV_CHUNK
    nb = RWKV_ROWS if b % RWKV_ROWS == 0 else 1
    nch, ncc = ltot // c, l_ctx // c
    nlc = nch - ncc

    def chunk(d, j):
        back = jnp.where(j < ncc, ncc - 1 - j, nch + ncc - 1 - j)
        return jnp.where(d == 0, j, back)

    def out_chunk(d, j):
        o = chunk(d, j)
        return jnp.where(o >= ncc, o - ncc, jnp.where(d == 0, 0, nlc - 1))

    sh_spec = pl.BlockSpec((nb, c, width), lambda d, i, j: (i, chunk(d, j), 0))
    di_spec = pl.BlockSpec((None, nb, c, width), lambda d, i, j: (d, i, chunk(d, j), 0))
    return pl.pallas_call(
        functools.partial(_rwkv_chunk_kernel, c=c, head=head),
        grid=(2, b // nb, nch),
        in_specs=[sh_spec] * 3 + [di_spec] * 3,
        out_specs=pl.BlockSpec((None, nb, c, width), lambda d, i, j: (d, i, out_chunk(d, j), 0)),
        out_shape=_sds((2, b, (nch - ncc) * c, width)),
        scratch_shapes=[pltpu.VMEM((nb * width // (2 * head), 2 * head, 2 * head), F32)],
        compiler_params=_cparams(("arbitrary", "arbitrary", "arbitrary")),
        name="rwkv_chunk",
    )(*shared, *directional)


def _s5_kernel(x_ref, d_ref, m_ref, pin_ref, pout_ref, ar_ref, ai_ref, y_ref, hall_ref, *, nb, n_chunks, n_ctx_chunks):
    x = x_ref[0]
    xb = x.astype(BF16)
    lat0 = n_ctx_chunks * nb
    xl = xb[lat0:]
    y = d_ref[0] * x[lat0:]
    half = ar_ref.shape[-1] // 2
    fwd = list(range(n_chunks))
    bwd = list(range(n_ctx_chunks - 1, -1, -1)) + list(range(n_chunks - 1, n_ctx_chunks - 1, -1))
    for d, order in enumerate((fwd, bwd)):
        v = _dot(xb, pin_ref[d, 0])
        ar = ar_ref[d, 0]
        ai = ai_ref[d, 0]
        h = jnp.zeros((nb, 2 * half), F32)
        for c in order:
            hall_ref[c * nb:(c + 1) * nb, :] = h
            h = h * ar + pltpu.roll(h, half, 1) * ai + v[c * nb:(c + 1) * nb]
        y = y + _dot(xl, m_ref[d, 0]) + _dot(hall_ref[lat0:, :].astype(BF16), pout_ref[d, 0])
    y_ref[0] = y


def _s5_tables(a_re, a_im, log_dt, b_re, b_im, c_re, c_im, tc):
    a_re, a_im, b_re, b_im = (t.astype(F32) for t in (a_re, a_im, b_re, b_im))
    c_re, c_im = c_re.astype(F32), c_im.astype(F32)
    dt = jnp.exp(log_dt.astype(F32))[..., None]
    mag = jnp.exp(dt * a_re)
    ab_re, ab_im = mag * jnp.cos(dt * a_im), mag * jnp.sin(dt * a_im)
    nr, ni = ab_re - 1.0, ab_im
    den = a_re * a_re + a_im * a_im
    cf_re = (nr * a_re + ni * a_im) / den
    cf_im = (ni * a_re - nr * a_im) / den
    bb_re = cf_re[..., None] * b_re - cf_im[..., None] * b_im
    bb_im = cf_re[..., None] * b_im + cf_im[..., None] * b_re

    k = jnp.arange(tc + 1, dtype=F32)[:, None, None, None]
    pmag = jnp.exp(k * (dt * a_re)[None])
    pang = k * (dt * a_im)[None]
    pw_re, pw_im = pmag * jnp.cos(pang), pmag * jnp.sin(pang)

    ab_k_re = pw_re[..., None] * bb_re[None] - pw_im[..., None] * bb_im[None]
    ab_k_im = pw_re[..., None] * bb_im[None] + pw_im[..., None] * bb_re[None]
    ca_k_re = c_re[None] * pw_re[:, :, :, None, :] - c_im[None] * pw_im[:, :, :, None, :]
    ca_k_im = c_re[None] * pw_im[:, :, :, None, :] + c_im[None] * pw_re[:, :, :, None, :]

    taps = (jnp.einsum('dghp,kdgpj->kdghj', c_re, ab_k_re[:tc], precision=HIGHEST)
            - jnp.einsum('dghp,kdgpj->kdghj', c_im, ab_k_im[:tc], precision=HIGHEST))
    s_idx = jnp.arange(tc)[:, None]
    t_idx = jnp.arange(tc)[None, :]
    hc = b_re.shape[-1]
    g = a_re.shape[1]
    p = a_re.shape[2]

    def toeplitz(tp, lag, valid):
        m = jnp.where(valid[:, :, None, None, None], tp[jnp.clip(lag, 0, tc - 1)], 0.0)
        return m.transpose(2, 0, 4, 1, 3).reshape(g, tc * hc, tc * hc)

    m_f = toeplitz(taps[:, 0], t_idx - s_idx, t_idx >= s_idx)
    m_b = toeplitz(taps[:, 1], s_idx - t_idx, s_idx >= t_idx)
    m = jnp.stack([m_f, m_b]).astype(BF16)

    def pin_of(re, im):
        both = jnp.concatenate([re, im], axis=2)
        return both.transpose(1, 0, 3, 2).reshape(g, tc * hc, 2 * p)

    pin = jnp.stack([pin_of(ab_k_re[:tc, 0][::-1], ab_k_im[:tc, 0][::-1]),
                     pin_of(ab_k_re[:tc, 1], ab_k_im[:tc, 1])]).astype(BF16)

    def pout_of(re, im):
        both = jnp.concatenate([re, -im], axis=3)
        return both.transpose(1, 3, 0, 2).reshape(g, 2 * p, tc * hc)

    pout = jnp.stack([pout_of(ca_k_re[1:tc + 1, 0], ca_k_im[1:tc + 1, 0]),
                      pout_of(ca_k_re[1:tc + 1, 1][::-1], ca_k_im[1:tc + 1, 1][::-1])]).astype(BF16)

    at_re, at_im = pw_re[tc], pw_im[tc]
    ar = jnp.concatenate([at_re, at_re], axis=-1)[:, :, None, :]
    ai = jnp.concatenate([-at_im, at_im], axis=-1)[:, :, None, :]
    return m, pin, pout, ar, ai


def _s5_mix(u_ctx, u_lat, s5_d, tables):
    m, pin, pout, ar, ai = tables
    b, lc, width = u_ctx.shape
    l = u_lat.shape[1]
    g = m.shape[1]
    hc = width // g
    tc = S5_CHUNK
    w = tc * hc
    ncc, nlc = lc // tc, l // tc
    nch = ncc + nlc
    u = jnp.concatenate([u_ctx, u_lat], axis=1)
    xg = u.reshape(b, nch, tc, g, hc).transpose(3, 1, 0, 2, 4).reshape(g, nch * b, w)
    dg = jnp.tile(s5_d.reshape(g, 1, hc), (1, tc, 1)).reshape(g, 1, w)
    pgrp = lambda a: pl.BlockSpec((2, 1) + a.shape[2:], lambda i: (0, i, 0, 0))
    y = pl.pallas_call(
        functools.partial(_s5_kernel, nb=b, n_chunks=nch, n_ctx_chunks=ncc),
        grid=(g,),
        in_specs=[pl.BlockSpec((1, nch * b, w), lambda i: (i, 0, 0)),
                  pl.BlockSpec((1, 1, w), lambda i: (i, 0, 0)),
                  pgrp(m), pgrp(pin), pgrp(pout), pgrp(ar), pgrp(ai)],
        out_specs=pl.BlockSpec((1, nlc * b, w), lambda i: (i, 0, 0)),
        out_shape=_sds((g, nlc * b, w)),
        scratch_shapes=[pltpu.VMEM((nch * b, pin.shape[-1]), F32)],
        compiler_params=_cparams(("parallel",), 56),
        name="s5_mix",
    )(xg, dg, m, pin, pout, ar, ai)
    return y.reshape(g, nlc, b, tc, hc).transpose(2, 1, 3, 0, 4).reshape(b, l, width)


def _gelu(x):
    return 0.5 * x * (1.0 + lax.erf(x * (1.0 / math.sqrt(2.0))))


def _mix_out_kernel(s5_ref, rw_ref, bon_ref, g_ref, x_ref, g1_ref, sh2_ref, sc2_ref, n2_ref,
                    wglu_ref, bglu_ref, lnw_ref, lnb_ref, wos_ref, wor_ref, h1_ref, hn_ref, *, head):
    y1 = _gelu(s5_ref[0])
    s5o = y1 * jax.nn.sigmoid(_dot(y1.astype(BF16), wglu_ref[...]) + bglu_ref[...])
    y = rw_ref[0, 0] + rw_ref[1, 0]
    seg_m = _seg_matrix(y.shape[-1], head)
    inv = 1.0 / head
    mu = _seg_sum(y, seg_m) * inv
    yc = y - mu
    var = _seg_sum(yc * yc, seg_m) * inv
    yn = yc * lax.rsqrt(var + GN_EPS) * lnw_ref[...] + lnb_ref[...]
    rwo = (yn + bon_ref[0]) * g_ref[0]
    mix = _dot(s5o.astype(BF16), wos_ref[...]) + _dot(rwo.astype(BF16), wor_ref[...])
    h1 = x_ref[0] + g1_ref[0] * mix
    h1_ref[0] = h1
    ms = jnp.mean(h1 * h1, axis=-1, keepdims=True)
    hn_ref[0] = (h1 * lax.rsqrt(ms + NORM_EPS) * n2_ref[...]) * (1.0 + sc2_ref[0]) + sh2_ref[0]


def _mix_out(s5y, rwy, bonus, gate, x, g1, sh2, sc2, n2g, w_glu, b_glu, ln_w, ln_b, w_out_s, w_out_r, head):
    b, l, d = x.shape
    width = s5y.shape[-1]
    tm = min(512, l)
    tok = lambda c: pl.BlockSpec((1, tm, c), lambda i, j: (i, j, 0))
    mod = pl.BlockSpec((1, 1, d), lambda i, j: (i, 0, 0))
    full = lambda a: pl.BlockSpec(a.shape, lambda i, j: (0,) * a.ndim)
    consts = (n2g, w_glu, b_glu, ln_w, ln_b, w_out_s, w_out_r)
    return pl.pallas_call(
        functools.partial(_mix_out_kernel, head=head),
        grid=(b, l // tm),
        in_specs=[tok(width), pl.BlockSpec((2, 1, tm, width), lambda i, j: (0, i, j, 0)), tok(width), tok(width)]
        + [tok(d), mod, mod, mod] + [full(a) for a in consts],
        out_specs=[tok(d), tok(d)],
        out_shape=[_sds((b, l, d)), _sds((b, l, d))],
        compiler_params=_cparams(("parallel", "parallel")),
        name="mix_out",
    )(s5y, rwy, bonus, gate, x, g1, sh2, sc2, *consts)


def _staircase(k):
    return [(a, b) for a in range(k) for b in range(k) if (a + 1) * (b + 1) <= k]


def _topk_rows(s, k):
    rows = s.shape[0]
    iota = lax.broadcasted_iota(jnp.int32, s.shape, 0)
    vals, idxs = [], []
    for _ in range(k):
        m = jnp.max(s, axis=0, keepdims=True)
        idx = jnp.min(jnp.where(s == m, iota, rows), axis=0, keepdims=True)
        vals.append(m)
        idxs.append(idx)
        s = jnp.where(iota == idx, -jnp.inf, s)
    return jnp.concatenate(vals, axis=0), jnp.concatenate(idxs, axis=0).astype(F32)


def _peer_topk_kernel(hn_ref, wq_ref, keys_ref, sel1_ref, sel2_ref, pad_ref, off_ref, gate_ref, *, heads, nkeys, half,
                      rows_per_expert):
    hb = hn_ref[...].astype(BF16)
    nt = (((1,), (1,)), ((), ()))
    sel1 = sel1_ref[...]
    sel2 = sel2_ref[...]
    pick = lambda sel, a: jnp.dot(sel, a, precision=HIGHEST, preferred_element_type=F32)
    ids, gates = [], []
    for h in range(heads):
        top = []
        for c in range(2):
            row0 = (h * 2 + c) * half
            qt = lax.dot_general(wq_ref[row0:row0 + half, :], hb, nt, preferred_element_type=F32)
            st = _dot(keys_ref[h * 2 + c], qt.astype(BF16))
            top.append(_topk_rows(st, PEER_TOPK))
        (s1, i1), (s2, i2) = top
        cand = pick(sel1, s1) + pick(sel2, s2) + pad_ref[...]
        cid = pick(sel1, i1) * nkeys + pick(sel2, i2)
        best, pos = _topk_rows(cand, PEER_TOPK)
        riota = lax.broadcasted_iota(jnp.int32, cid.shape, 0).astype(F32)
        eid = jnp.concatenate(
            [jnp.sum(jnp.where(riota == pos[r:r + 1], cid, 0.0), axis=0, keepdims=True) for r in range(PEER_TOPK)], axis=0)
        e = jnp.exp(best - jnp.max(best, axis=0, keepdims=True))
        gates.append(e / jnp.sum(e, axis=0, keepdims=True))
        ids.append(eid)
    eid = jnp.concatenate(ids, axis=0).T.astype(jnp.int32)
    off_ref[...] = eid * rows_per_expert
    gate_ref[...] = jnp.concatenate(gates, axis=0).T


def _peer_topk(hn, wq_t, keys, rows_per_expert):
    n, d = hn.shape
    heads = keys.shape[0] // 2
    nkeys, half = keys.shape[1], keys.shape[2]
    tm = min(512, n)
    pairs = _staircase(PEER_TOPK)
    rows = -(-len(pairs) // SUBLANES) * SUBLANES
    sel1 = np.zeros((rows, PEER_TOPK), np.float32)
    sel2 = np.zeros((rows, PEER_TOPK), np.float32)
    pad = np.zeros((rows, 1), np.float32)
    for i, (a, b) in enumerate(pairs):
        sel1[i, a] = 1.0
        sel2[i, b] = 1.0
    pad[len(pairs):] = -np.inf
    slots = heads * PEER_TOPK
    full = lambda a: pl.BlockSpec(a.shape, lambda i: (0,) * a.ndim)
    consts = (wq_t, keys, jnp.asarray(sel1), jnp.asarray(sel2), jnp.asarray(pad))
    return pl.pallas_call(
        functools.partial(_peer_topk_kernel, heads=heads, nkeys=nkeys, half=half, rows_per_expert=rows_per_expert),
        grid=(n // tm,),
        in_specs=[pl.BlockSpec((tm, d), lambda i: (i, 0))] + [full(a) for a in consts],
        out_specs=[pl.BlockSpec((tm, slots), lambda i: (i, 0))] * 2,
        out_shape=[_sds((n, slots), jnp.int32), _sds((n, slots))],
        compiler_params=_cparams(("parallel",)),
        name="peer_topk",
    )(hn, *consts)


def _pack_table(tab):
    e, d = tab.shape
    half = d // 2
    bits = lax.bitcast_convert_type(tab.astype(BF16), jnp.uint16).astype(jnp.uint32)
    packed = ((bits[:, :half] << 16) | bits[:, half:]).reshape(e * (half // LANES), LANES)
    return jnp.pad(packed, ((0, SUBLANES), (0, 0)))


def _expert_halves(tab_ref, off):
    bits = tab_ref[pl.ds(off, SUBLANES), :]
    return pltpu.bitcast(bits & jnp.uint32(0xFFFF0000), F32), pltpu.bitcast(bits << 16, F32)


def _peer_u_kernel(off_ref, h_ref, gate_ref, tab_ref, w_ref, q_ref, r_ref, *, tq, slots):
    half_rows = SUBLANES // 2
    sub = lax.broadcasted_iota(jnp.int32, (SUBLANES, LANES), 0)
    low = sub < half_rows
    groups = slots // SUBLANES
    qrows = slots * half_rows

    ones = jnp.ones((SUBLANES, LANES), BF16)
    nt = (((1,), (1,)), ((), ()))

    tg = PEER_U_GROUP
    ngroups = tq // tg

    def lane_sums(g):
        start = g * tg * qrows if isinstance(g, int) else pl.multiple_of(g * (tg * qrows), tg * qrows)
        sums = lax.dot_general(ones, q_ref[pl.ds(start, tg * qrows), :], nt, preferred_element_type=F32)
        for i in range(tg):
            r_ref[pl.ds(g * tg + i, 1), :] = sums[0:1, i * qrows:(i + 1) * qrows]

    q_ref[(tq - tg) * qrows:, :] = jnp.zeros((tg * qrows, LANES), BF16)

    def token_group(g, carry):
        lane_sums(jnp.where(g == 0, ngroups - 1, g - 1))
        for i in range(tg):
            token(g * tg + i)
        return carry

    def token(t):
        h = h_ref[t]
        h_first = pltpu.bitcast(jnp.where(low, h, 0.0).astype(BF16).astype(F32), jnp.uint32)
        h_second = pltpu.bitcast(jnp.where(low, pltpu.roll(h, half_rows, 0), 0.0).astype(BF16).astype(F32), jnp.uint32)
        h_pair = pltpu.bitcast(h_first | (h_second >> 16), BF16)
        q_tok = q_ref.at[pl.ds(pl.multiple_of(t * qrows, qrows), qrows)]
        grows = SUBLANES * half_rows
        for g8 in range(groups):
            base = t * slots + g8 * SUBLANES
            ps = []
            for r in range(SUBLANES):
                bits = tab_ref[pl.ds(off_ref[0, 0, base + r], SUBLANES), :]
                prod = pltpu.bitcast(pltpu.bitcast(bits, BF16) * h_pair, jnp.uint32)
                ps.append(pltpu.bitcast(prod & jnp.uint32(0xFFFF0000), F32) + pltpu.bitcast(prod << 16, F32))
            pairs = [ps[i] + pltpu.roll(ps[i + 1], half_rows, 0) for i in range(0, SUBLANES, 2)]
            q_tok[g8 * grows:(g8 + 1) * grows, :] = jnp.concatenate(pairs, axis=0).astype(BF16)

    lax.fori_loop(0, ngroups, token_group, 0)
    lane_sums(ngroups - 1)

    fold = (lax.broadcasted_iota(jnp.int32, (qrows, slots), 0) // half_rows
            == lax.broadcasted_iota(jnp.int32, (qrows, slots), 1))
    fold = jnp.where(fold, 1.0, 0.0).astype(BF16)
    rs = r_ref[...]
    rh = rs.astype(BF16)
    rl = (rs - rh.astype(F32)).astype(BF16)
    act = _dot(rh, fold) + _dot(rl, fold)
    w_ref[...] = gate_ref[...] * _gelu(act)


def _smem_block(tq, slots):
    return pl.BlockSpec((1, 1, tq * slots), lambda i: (i, 0, 0), memory_space=pltpu.SMEM)


def _peer_u(off_blocks, hn3, gate, tab):
    n, rows, lanes = hn3.shape
    slots = gate.shape[1]
    tq = off_blocks.shape[-1] // slots
    return pl.pallas_call(
        functools.partial(_peer_u_kernel, tq=tq, slots=slots),
        grid=(n // tq,),
        in_specs=[_smem_block(tq, slots),
                  pl.BlockSpec((tq, rows, lanes), lambda i: (i, 0, 0)),
                  pl.BlockSpec((tq, slots), lambda i: (i, 0)),
                  pl.BlockSpec(tab.shape, lambda i: (0, 0), pipeline_mode=pl.Buffered(1))],
        out_specs=pl.BlockSpec((tq, slots), lambda i: (i, 0)),
        out_shape=_sds((n, slots)),
        scratch_shapes=[pltpu.VMEM((tq * slots * (SUBLANES // 2), lanes), BF16),
                        pltpu.VMEM((tq, slots * (SUBLANES // 2)), F32)],
        compiler_params=_cparams(("parallel",), 52),
        name="peer_u",
    )(off_blocks, hn3, gate, tab)


def _peer_v_kernel(off_ref, w_ref, tab_ref, o_ref, ws_ref, dg_ref, ot_ref, *, tq, slots):
    sub = lax.broadcasted_iota(jnp.int32, (SUBLANES, LANES), 0)
    low = sub < SUBLANES // 2

    eye = lax.broadcasted_iota(jnp.int32, (slots, slots), 0) == lax.broadcasted_iota(jnp.int32, (slots, slots), 1)

    def diag(t, carry):
        wrow = jnp.broadcast_to(w_ref[pl.ds(t, 1), :], (slots, slots))
        dg_ref[pl.ds(pl.multiple_of(t * slots, slots), slots), :] = jnp.where(eye, wrow, 0.0).astype(BF16)
        return carry

    lax.fori_loop(0, tq, diag, 0, unroll=4)
    ws_ref[...] = _dot(dg_ref[...], jnp.ones((slots, LANES), BF16))

    def token(t, carry):
        zero = jnp.zeros((SUBLANES, LANES), F32)
        first, second = [zero, zero], [zero, zero]
        for s in range(slots):
            hi, lo = _expert_halves(tab_ref, off_ref[0, 0, t * slots + s])
            w = jnp.broadcast_to(ws_ref[pl.ds(t * slots + s, 1), :], (SUBLANES, LANES))
            first[s % 2] = first[s % 2] + hi * w
            second[s % 2] = second[s % 2] + lo * w
        tile = jnp.where(low, first[0] + first[1], pltpu.roll(second[0] + second[1], SUBLANES // 2, 0))
        ot_ref[pl.ds(pl.multiple_of(t * SUBLANES, SUBLANES), SUBLANES), :] = tile
        return carry

    lax.fori_loop(0, tq, token, 0)
    for j in range(SUBLANES):
        o_ref[:, j * LANES:(j + 1) * LANES] = ot_ref[pl.ds(j, tq, stride=SUBLANES), :]


def _peer_v(off_blocks, w, tab):
    n, slots = w.shape
    tq = off_blocks.shape[-1] // slots
    lanes = tab.shape[-1]
    assert slots == lanes, "the weight splat uses one (slots, lanes) identity tile"
    return pl.pallas_call(
        functools.partial(_peer_v_kernel, tq=tq, slots=slots),
        grid=(n // tq,),
        in_specs=[_smem_block(tq, slots),
                  pl.BlockSpec((tq, slots), lambda i: (i, 0)),
                  pl.BlockSpec(tab.shape, lambda i: (0, 0), pipeline_mode=pl.Buffered(1))],
        out_specs=pl.BlockSpec((tq, VREG_WORDS), lambda i: (i, 0)),
        out_shape=_sds((n, VREG_WORDS)),
        scratch_shapes=[pltpu.VMEM((tq * slots, lanes), F32), pltpu.VMEM((tq * slots, slots), BF16),
                        pltpu.VMEM((tq * SUBLANES, lanes), F32)],
        compiler_params=_cparams(("parallel",), 52),
        name="peer_v",
    )(off_blocks, w, tab)


def _final_kernel(h1_ref, p_ref, g2_ref, nf_ref, o_ref):
    h = h1_ref[0] + g2_ref[0] * p_ref[0]
    ms = jnp.mean(h * h, axis=-1, keepdims=True)
    o_ref[0] = h * lax.rsqrt(ms + NORM_EPS) * nf_ref[...]


def _final_norm(h1, peer, g2, nfg):
    b, l, d = h1.shape
    tm = min(512, l)
    tok = pl.BlockSpec((1, tm, d), lambda i, j: (i, j, 0))
    return pl.pallas_call(
        _final_kernel,
        grid=(b, l // tm),
        in_specs=[tok, tok, pl.BlockSpec((1, 1, d), lambda i, j: (i, 0, 0)), pl.BlockSpec((1, d), lambda i, j: (0, 0))],
        out_specs=tok,
        out_shape=_sds((b, l, d)),
        compiler_params=_cparams(("parallel", "parallel")),
        name="final_norm",
    )(h1, peer, g2, nfg)


def _pad_cols(a, n):
    return jnp.pad(a, [(0, 0)] * (a.ndim - 1) + [(0, n - a.shape[-1])])


def _layer(h, hc, c, c_ctx, p, norm_f_g):
    b, l, d = h.shape
    lc = hc.shape[1]
    s5w = p['s5_d'].shape[-1]
    rww = p['rw_w0'].shape[-1]
    heads, head = p['rw_r_k'].shape
    n_dec, n_aaa, n_gate = p['rw_w_w2'].shape[1], p['rw_w_a2'].shape[1], p['rw_w_g2'].shape[0]
    n_lora = n_dec + n_aaa + n_gate
    lora_pad = -(-n_lora // LANES) * LANES

    rows = -(-(b + 1) // SUBLANES) * SUBLANES
    cc = jnp.zeros((rows, d), F32).at[:b].set(c).at[b].set(c_ctx)
    mods = _ada_mods(cc, p['w_ada'], p['b_ada'])
    sh1, sc1, g1, sh2, sc2, g2 = [m.reshape(rows, 1, d) for m in jnp.split(mods, 6, axis=-1)]
    lat = lambda m: m[:b]
    ctxm = lambda m: m[b:b + 1]

    w_in = p['w_in']
    w_s5 = w_in[:, :s5w].astype(BF16)
    w_main = w_in[:, s5w:s5w + 3 * rww].astype(BF16)
    w_lora = _pad_cols(w_in[:, s5w + 3 * rww:], lora_pad).astype(BF16)
    n1g = p['norm1_g'].reshape(1, d)
    u_l, zm_l, zl_l = _in_proj(h, lat(sh1), lat(sc1), n1g, w_s5, w_main, w_lora)
    u_c, zm_c, zl_c = _in_proj(hc, ctxm(sh1), ctxm(sc1), n1g, w_s5, w_main, w_lora)

    mu = p['rw_mu']
    mu_main = mu[:3 * rww].reshape(1, -1)
    mu_lora = _pad_cols(mu[3 * rww:], lora_pad).reshape(1, -1)
    place = lambda w, r0: jnp.zeros(w.shape[:-2] + (lora_pad, rww), F32).at[..., r0:r0 + w.shape[-2], :].set(w).astype(BF16)
    rw_params = (place(p['rw_w_g2'], n_dec + n_aaa), place(p['rw_w_w2'], 0), place(p['rw_w_a2'], n_dec),
                 p['rw_w0'].reshape(2, 1, rww), p['rw_a0'].reshape(2, 1, rww),
                 p['rw_k_k'].reshape(1, rww), p['rw_k_a'].reshape(1, rww), p['rw_r_k'].reshape(1, rww))
    prep_params = (mu_main, mu_lora) + rw_params
    ctx_streams = _rwkv_prep(zm_c, zl_c, prep_params, head, lc + l)
    *streams, gate_l, bonus_l = _rwkv_prep(zm_l, zl_l, prep_params, head, lc + l, ctx_streams)
    rw_y = _rwkv_chunked(tuple(streams[:3]), tuple(streams[3:]), lc, head)

    tables = _s5_tables(p['s5_a_re'], p['s5_a_im'], p['s5_log_dt'], p['s5_b_re'], p['s5_b_im'],
                        p['s5_c_re'], p['s5_c_im'], S5_CHUNK)
    s5_y = _s5_mix(u_c, u_l, p['s5_d'], tables)

    w_out = p['w_out']
    h1, hn2 = _mix_out(s5_y, rw_y, bonus_l, gate_l, h, lat(g1), lat(sh2), lat(sc2), p['norm2_g'].reshape(1, d),
                       p['s5_w_glu'].astype(BF16), p['s5_b_glu'].reshape(1, s5w),
                       p['rw_ln_w'].reshape(1, rww), p['rw_ln_b'].reshape(1, rww),
                       w_out[:s5w].astype(BF16), w_out[s5w:].astype(BF16), head)

    n = b * l
    keys = p['peer_keys']
    pheads, _, nkeys, half = keys.shape
    assert d == VREG_WORDS, "peer_u / peer_v hold one token's features in a single (8, 128) tile"
    off, gate = _peer_topk(hn2.reshape(n, d), p['peer_w_q'].T.astype(BF16),
                           keys.reshape(pheads * 2, nkeys, half).astype(BF16), d // (2 * LANES))
    slots = gate.shape[1]
    tq = min(PEER_TILE, n)
    blocks = lambda a: a.reshape(n // tq, 1, tq * slots)
    w = _peer_u(blocks(off), hn2.reshape(n, SUBLANES, LANES), gate, _pack_table(p['peer_u']))
    peer = _peer_v(blocks(off), w, _pack_table(p['peer_v']))

    return _final_norm(h1, peer.reshape(b, l, d), lat(g2), norm_f_g.reshape(1, d))


def kernel(x, c, ctx, c_ctx, w_ada, b_ada, norm1_g, norm2_g, w_in, s5_a_re, s5_a_im, s5_log_dt, s5_b_re, s5_b_im, s5_c_re, s5_c_im, s5_d, s5_w_glu, s5_b_glu, rw_mu, rw_w0, rw_w_w2, rw_a0, rw_w_a2, rw_w_g2, rw_k_k, rw_k_a, rw_r_k, rw_ln_w, rw_ln_b, w_out, peer_w_q, peer_keys, peer_u, peer_v, norm_f_g):
    assert w_ada.shape[0] == 1, "single-layer block: the context stream is never updated"
    p = dict(w_ada=w_ada[0], b_ada=b_ada[0], norm1_g=norm1_g[0], norm2_g=norm2_g[0], w_in=w_in[0],
             s5_a_re=s5_a_re[0], s5_a_im=s5_a_im[0], s5_log_dt=s5_log_dt[0], s5_b_re=s5_b_re[0], s5_b_im=s5_b_im[0],
             s5_c_re=s5_c_re[0], s5_c_im=s5_c_im[0], s5_d=s5_d[0], s5_w_glu=s5_w_glu[0], s5_b_glu=s5_b_glu[0],
             rw_mu=rw_mu[0], rw_w0=rw_w0[0], rw_w_w2=rw_w_w2[0], rw_a0=rw_a0[0], rw_w_a2=rw_w_a2[0],
             rw_w_g2=rw_w_g2[0], rw_k_k=rw_k_k[0], rw_k_a=rw_k_a[0], rw_r_k=rw_r_k[0], rw_ln_w=rw_ln_w[0],
             rw_ln_b=rw_ln_b[0], w_out=w_out[0], peer_w_q=peer_w_q[0], peer_keys=peer_keys[0],
             peer_u=peer_u[0], peer_v=peer_v[0])
    return _layer(x, ctx, c, c_ctx, p, norm_f_g)
```

```python
import functools
import math

import jax
import jax.numpy as jnp
import numpy as np
from jax import lax
from jax.experimental import pallas as pl
from jax.experimental.pallas import tpu as pltpu

F32 = jnp.float32
BF16 = jnp.bfloat16
HIGHEST = lax.Precision.HIGHEST

SUBLANES = 8
LANES = 128
VREG_WORDS = SUBLANES * LANES

GRID_W = 64
NORM_EPS = 1e-6
GN_EPS = 64e-5
KK_EPS = 1e-12
PEER_TOPK = 16

S5_CHUNK = 32
RWKV_CHUNK = 64
RWKV_ROWS = 4
PEER_TILE = 64
PEER_U_GROUP = 4


def _dot(a, b):
    return jnp.dot(a, b, preferred_element_type=F32)


def _cparams(sem, vmem_mb=48):
    return pltpu.CompilerParams(dimension_semantics=sem, vmem_limit_bytes=vmem_mb * 1024 * 1024)


def _sds(shape, dtype=F32):
    return jax.ShapeDtypeStruct(shape, dtype)


def _mods_kernel(c_ref, w_ref, b_ref, o_ref):
    c = c_ref[...]
    s = c * jax.nn.sigmoid(c)
    o_ref[...] = jnp.dot(s, w_ref[...], precision=HIGHEST, preferred_element_type=F32) + b_ref[...]


def _ada_mods(cc, w_ada, b_ada):
    rows, d = cc.shape
    n = w_ada.shape[1]
    bn = d if n % d == 0 else n
    return pl.pallas_call(
        _mods_kernel,
        grid=(n // bn,),
        in_specs=[pl.BlockSpec((rows, d), lambda j: (0, 0)),
                  pl.BlockSpec((d, bn), lambda j: (0, j)),
                  pl.BlockSpec((1, bn), lambda j: (0, j))],
        out_specs=pl.BlockSpec((rows, bn), lambda j: (0, j)),
        out_shape=_sds((rows, n)),
        compiler_params=_cparams(("arbitrary",)),
        name="ada_mods",
    )(cc, w_ada, b_ada.reshape(1, n))


def _inproj_kernel(x_ref, sh_ref, sc_ref, g_ref, ws_ref, wz_ref, wl_ref, u_ref, z_ref, l_ref):
    x = x_ref[0]
    ms = jnp.mean(x * x, axis=-1, keepdims=True)
    hn = (x * lax.rsqrt(ms + NORM_EPS) * g_ref[...]) * (1.0 + sc_ref[0]) + sh_ref[0]
    hb = hn.astype(BF16)
    u_ref[0] = _dot(hb, ws_ref[...])
    z_ref[0] = _dot(hb, wz_ref[...])
    l_ref[0] = _dot(hb, wl_ref[...])


def _in_proj(x, shift, scale, gain, w_s5, w_main, w_lora):
    b, l, d = x.shape
    tm = min(512, l)
    bm = shift.shape[0]
    mod_map = (lambda i, j: (i, 0, 0)) if bm > 1 else (lambda i, j: (0, 0, 0))
    full = lambda a: pl.BlockSpec(a.shape, lambda i, j: (0,) * a.ndim)
    return pl.pallas_call(
        _inproj_kernel,
        grid=(b, l // tm),
        in_specs=[pl.BlockSpec((1, tm, d), lambda i, j: (i, j, 0)),
                  pl.BlockSpec((1, 1, d), mod_map), pl.BlockSpec((1, 1, d), mod_map),
                  full(gain), full(w_s5), full(w_main), full(w_lora)],
        out_specs=[pl.BlockSpec((1, tm, w.shape[1]), lambda i, j: (i, j, 0)) for w in (w_s5, w_main, w_lora)],
        out_shape=[_sds((b, l, w.shape[1])) for w in (w_s5, w_main, w_lora)],
        compiler_params=_cparams(("parallel", "parallel")),
        name="in_proj",
    )(x, shift, scale, gain, w_s5, w_main, w_lora)


def _seg_matrix(width, seg):
    r = lax.broadcasted_iota(jnp.int32, (width, width), 0) // seg
    c = lax.broadcasted_iota(jnp.int32, (width, width), 1) // seg
    return jnp.where(r == c, 1.0, 0.0).astype(BF16)


def _seg_sum(x, seg_m):
    hi = x.astype(BF16)
    lo = (x - hi.astype(F32)).astype(BF16)
    return _dot(hi, seg_m) + _dot(lo, seg_m)


def _rwkv_streams(zz, zl, p, head, outs, with_gate):
    (wg_ref, ww_ref, wa_ref, w0_ref, a0_ref, kk_ref, ka_ref, rk_ref) = p
    width = kk_ref.shape[-1]
    r = zz[:, :width]
    k = zz[:, width:2 * width]
    v = zz[:, 2 * width:3 * width]
    seg_m = _seg_matrix(width, head)
    kk = k * kk_ref[...]
    kk = kk * lax.rsqrt(_seg_sum(kk * kk, seg_m) + KK_EPS)
    outs["r"][0] = r
    outs["v"][0] = v
    outs["kk"][0] = kk
    tl = jnp.tanh(zl).astype(BF16)
    zb = zl.astype(BF16)
    bonus = None
    for d in range(2):
        outs["dec"][d, 0] = -math.exp(-0.5) * jax.nn.sigmoid(w0_ref[d] + _dot(tl, ww_ref[d]))
        a = jax.nn.sigmoid(a0_ref[d] + _dot(zb, wa_ref[d]))
        kd = k * (1.0 + (a - 1.0) * ka_ref[...])
        outs["kd"][d, 0] = kd
        outs["bb"][d, 0] = kk * a
        if with_gate:
            bd = _seg_sum(r * kd * rk_ref[...], seg_m) * v
            bonus = bd if bonus is None else bonus + bd
    if with_gate:
        outs["bonus"][0] = bonus
        outs["g"][0] = _dot(jax.nn.sigmoid(zl).astype(BF16), wg_ref[...])


def _lerp_shift(z, shifted, mu):
    return z + (shifted - z) * mu


def _grid_shift(zc, zp, zn, has_prev, has_next):
    t, c = zc.shape
    tok = lax.broadcasted_iota(jnp.int32, (t, 1), 0) % GRID_W
    col = lax.broadcasted_iota(jnp.int32, (1, c), 1) % 4
    left = jnp.where(tok == 0, 0.0, pltpu.roll(zc, 1, 0))
    right = jnp.where(tok == GRID_W - 1, 0.0, pltpu.roll(zc, t - 1, 0))
    zp = jnp.where(has_prev, zp, 0.0)
    zn = jnp.where(has_next, zn, 0.0)
    if t > GRID_W:
        up = jnp.concatenate([zp, zc[:t - GRID_W]], axis=0)
        down = jnp.concatenate([zc[GRID_W:], zn], axis=0)
    else:
        up, down = zp, zn
    return jnp.where(col == 0, left, jnp.where(col == 1, right, jnp.where(col == 2, up, down)))


def _seq_shift(zc):
    t, c = zc.shape
    tok = lax.broadcasted_iota(jnp.int32, (t, 1), 0)
    col = lax.broadcasted_iota(jnp.int32, (1, c), 1) % 2
    prev = jnp.where(tok == 0, 0.0, pltpu.roll(zc, 1, 0))
    nxt = jnp.where(tok == t - 1, 0.0, pltpu.roll(zc, t - 1, 0))
    return jnp.where(col == 0, prev, nxt)


_STREAMS = ("r", "v", "kk", "dec", "kd", "bb")


def _rwkv_prep_lat_kernel(zc_ref, zp_ref, zn_ref, lc_ref, lp_ref, ln_ref, mum_ref, mul_ref, *rest, head):
    p, o = rest[:8], rest[8 + len(_STREAMS):]
    outs = dict(zip(_STREAMS + ("g", "bonus"), o))
    j = pl.program_id(1)
    has_prev = j > 0
    has_next = j < pl.num_programs(1) - 1
    zc = zc_ref[0]
    lc = lc_ref[0]
    zz = _lerp_shift(zc, _grid_shift(zc, zp_ref[0], zn_ref[0], has_prev, has_next), mum_ref[...])
    zl = _lerp_shift(lc, _grid_shift(lc, lp_ref[0], ln_ref[0], has_prev, has_next), mul_ref[...])
    _rwkv_streams(zz, zl, p, head, outs, True)


def _rwkv_prep_ctx_kernel(zc_ref, lc_ref, mum_ref, mul_ref, *rest, head):
    p, o = rest[:8], rest[8:]
    outs = dict(zip(_STREAMS, o))
    zc = zc_ref[0]
    lc = lc_ref[0]
    zz = _lerp_shift(zc, _seq_shift(zc), mum_ref[...])
    zl = _lerp_shift(lc, _seq_shift(lc), mul_ref[...])
    _rwkv_streams(zz, zl, p, head, outs, False)


def _rwkv_prep(z_main, z_lora, params, head, l_total, ctx_streams=None):
    b, l, cm = z_main.shape
    cl = z_lora.shape[-1]
    width = cm // 3
    full = lambda a: pl.BlockSpec(a.shape, lambda i, j: (0,) * a.ndim)
    grid_mode = ctx_streams is not None
    if grid_mode:
        l_ctx = l_total - l
        rows_per_tile = next(r for r in (8, 4, 2, 1) if l % (r * GRID_W) == 0 and l_ctx % (r * GRID_W) == 0)
        tt = rows_per_tile * GRID_W
        nt = l // tt
        first = l_ctx // tt
        last_row = l // GRID_W - 1
        cur = lambda c: pl.BlockSpec((1, tt, c), lambda i, j: (i, j, 0))
        prev = lambda c: pl.BlockSpec((1, GRID_W, c), lambda i, j: (i, jnp.maximum(j * rows_per_tile - 1, 0), 0))
        nxt = lambda c: pl.BlockSpec((1, GRID_W, c), lambda i, j: (i, jnp.minimum((j + 1) * rows_per_tile, last_row), 0))
        data = [z_main, z_main, z_main, z_lora, z_lora, z_lora]
        data_specs = [cur(cm), prev(cm), nxt(cm), cur(cl), prev(cl), nxt(cl)]
        kern = functools.partial(_rwkv_prep_lat_kernel, head=head)
        extra = list(ctx_streams)
        extra_specs = [pl.BlockSpec(memory_space=pl.ANY)] * len(extra)
        n_in = len(data) + len(params)
        aliases = {n_in + k: k for k in range(len(extra))}
        lat_only = 2
    else:
        tt, nt, first = l, 1, 0
        data = [z_main, z_lora]
        data_specs = [pl.BlockSpec((1, tt, cm), lambda i, j: (i, j, 0)), pl.BlockSpec((1, tt, cl), lambda i, j: (i, j, 0))]
        kern = functools.partial(_rwkv_prep_ctx_kernel, head=head)
        extra, extra_specs, aliases, lat_only = [], [], {}, 0
    single = pl.BlockSpec((1, tt, width), lambda i, j: (i, first + j, 0))
    double = pl.BlockSpec((2, 1, tt, width), lambda i, j: (0, i, first + j, 0))
    local = pl.BlockSpec((1, tt, width), lambda i, j: (i, j, 0))
    out_specs = [single] * 3 + [double] * 3 + [local] * lat_only
    out_shape = [_sds((b, l_total, width))] * 3 + [_sds((2, b, l_total, width))] * 3 + [_sds((b, l, width))] * lat_only
    return pl.pallas_call(
        kern,
        grid=(b, nt),
        in_specs=data_specs + [full(a) for a in params] + extra_specs,
        out_specs=out_specs,
        out_shape=out_shape,
        input_output_aliases=aliases,
        compiler_params=_cparams(("parallel", "parallel")),
        name="rwkv_prep_lat" if grid_mode else "rwkv_prep_ctx",
    )(*data, *params, *extra)


def _split_bf16(x):
    hi = x.astype(BF16)
    return hi, (x - hi.astype(F32)).astype(BF16)


def _rwkv_chunk_kernel(r_ref, v_ref, kk_ref, w_ref, kd_ref, bb_ref, y_ref, s_ref, *, c, head):
    d = pl.program_id(0)
    nb, _, width = r_ref.shape
    pair = 2 * head
    n_pairs = width // pair

    @pl.when(pl.program_id(2) == 0)
    def _():
        s_ref[...] = jnp.zeros_like(s_ref)

    ri = lax.broadcasted_iota(jnp.int32, (c, c), 0)
    ci = lax.broadcasted_iota(jnp.int32, (c, c), 1)
    before = jnp.where(d == 0, ci - ri, ri - ci) < 0
    incl = before | (ri == ci)
    eye = jnp.where(ri == ci, 1.0, 0.0)
    lmat = jnp.broadcast_to(jnp.where(incl, 1.0, 0.0).astype(BF16), (nb, c, c))

    def bmm(a, b, dims=((2,), (1,))):
        return lax.dot_general(a.astype(BF16), b.astype(BF16), (dims, ((0,), (0,))), preferred_element_type=F32)

    nt = ((2,), (2,))

    ld = w_ref[...]
    ld_hi, ld_lo = _split_bf16(ld)
    cum = bmm(lmat, ld_hi) + bmm(lmat, ld_lo)
    gn = jnp.exp(-cum)
    at = -(jnp.exp(cum - ld) * kk_ref[...])
    rt = jnp.exp(cum) * r_ref[...]
    bt = gn * bb_ref[...]
    kt = gn * kd_ref[...]
    vv = v_ref[...]

    def tiles(x):
        return jnp.stack([x[i, :, p * pair:(p + 1) * pair] for i in range(nb) for p in range(n_pairs)], axis=0)

    def per_head(x, select=None):
        xs = [x, x] if select is None else [jnp.where(select == hh, x, 0.0) for hh in range(2)]
        return jnp.stack(xs, axis=1).reshape((2 * x.shape[0],) + x.shape[1:])

    lane_head = lax.broadcasted_iota(jnp.int32, (1, 1, pair), 2) // head
    at_q, rt_q, bt_q, kt_q, v_q = tiles(at), tiles(rt), tiles(bt), tiles(kt), tiles(vv)
    s0 = s_ref[...]
    as0 = per_head(bmm(at_q, s0))
    rs0 = per_head(bmm(rt_q, s0))
    at_h, rt_h, v_h = per_head(at_q), per_head(rt_q), per_head(v_q)
    bt_h, kt_h = per_head(bt_q, lane_head), per_head(kt_q, lane_head)
    a_ab = jnp.where(before, bmm(at_h, bt_h, nt), 0.0)
    a_ak = jnp.where(before, bmm(at_h, kt_h, nt), 0.0)
    a_rb = jnp.where(incl, bmm(rt_h, bt_h, nt), 0.0)
    a_rk = jnp.where(incl, bmm(rt_h, kt_h, nt), 0.0)
    t = eye + a_ab
    pw = a_ab
    for _ in range(c.bit_length() - 2):
        pw = bmm(pw, pw)
        t = t + bmm(t, pw)
    u = bmm(t, as0 + bmm(a_ak, v_h))
    y = rs0 + bmm(a_rb, u) + bmm(a_rk, v_h)

    def merge(x):
        x = x.reshape((x.shape[0] // 2, 2) + x.shape[1:])
        return jnp.where(lane_head == 0, x[:, 0], x[:, 1])

    u_q, y_q = merge(u), merge(y)
    for i in range(nb):
        for p in range(n_pairs):
            y_ref[i, :, p * pair:(p + 1) * pair] = y_q[i * n_pairs + p]
    tn = ((1,), (1,))
    ones = jnp.ones((nb * n_pairs, c, pair), BF16)
    ds = bmm(bt_q, u_q, tn) + bmm(kt_q, v_q, tn)
    tot = bmm(tiles(ld_hi), ones, tn) + bmm(tiles(ld_lo), ones, tn)
    blk = (lax.broadcasted_iota(jnp.int32, (pair, pair), 0) // head
           == lax.broadcasted_iota(jnp.int32, (pair, pair), 1) // head)
    s_ref[...] = jnp.where(blk, jnp.exp(tot) * (s0 + ds), 0.0)


def _rwkv_chunked(shared, directional, l_ctx, head):
    b, ltot, width = shared[0].shape
    c = RWKV_CHUNK
    nb = RWKV_ROWS if b % RWKV_ROWS == 0 else 1
    nch, ncc = ltot // c, l_ctx // c
    nlc = nch - ncc

    def chunk(d, j):
        back = jnp.where(j < ncc, ncc - 1 - j, nch + ncc - 1 - j)
        return jnp.where(d == 0, j, back)

    def out_chunk(d, j):
        o = chunk(d, j)
        return jnp.where(o >= ncc, o - ncc, jnp.where(d == 0, 0, nlc - 1))

    sh_spec = pl.BlockSpec((nb, c, width), lambda d, i, j: (i, chunk(d, j), 0))
    di_spec = pl.BlockSpec((None, nb, c, width), lambda d, i, j: (d, i, chunk(d, j), 0))
    return pl.pallas_call(
        functools.partial(_rwkv_chunk_kernel, c=c, head=head),
        grid=(2, b // nb, nch),
        in_specs=[sh_spec] * 3 + [di_spec] * 3,
        out_specs=pl.BlockSpec((None, nb, c, width), lambda d, i, j: (d, i, out_chunk(d, j), 0)),
        out_shape=_sds((2, b, (nch - ncc) * c, width)),
        scratch_shapes=[pltpu.VMEM((nb * width // (2 * head), 2 * head, 2 * head), F32)],
        compiler_params=_cparams(("arbitrary", "arbitrary", "arbitrary")),
        name="rwkv_chunk",
    )(*shared, *directional)


def _s5_kernel(x_ref, d_ref, m_ref, pin_ref, pout_ref, ar_ref, ai_ref, y_ref, hall_ref, *, nb, n_chunks, n_ctx_chunks):
    x = x_ref[0]
    xb = x.astype(BF16)
    lat0 = n_ctx_chunks * nb
    xl = xb[lat0:]
    y = d_ref[0] * x[lat0:]
    half = ar_ref.shape[-1] // 2
    fwd = list(range(n_chunks))
    bwd = list(range(n_ctx_chunks - 1, -1, -1)) + list(range(n_chunks - 1, n_ctx_chunks - 1, -1))
    for d, order in enumerate((fwd, bwd)):
        v = _dot(xb, pin_ref[d, 0])
        ar = ar_ref[d, 0]
        ai = ai_ref[d, 0]
        h = jnp.zeros((nb, 2 * half), F32)
        for c in order:
            hall_ref[c * nb:(c + 1) * nb, :] = h
            h = h * ar + pltpu.roll(h, half, 1) * ai + v[c * nb:(c + 1) * nb]
        y = y + _dot(xl, m_ref[d, 0]) + _dot(hall_ref[lat0:, :].astype(BF16), pout_ref[d, 0])
    y_ref[0] = y


def _s5_tables(a_re, a_im, log_dt, b_re, b_im, c_re, c_im, tc):
    a_re, a_im, b_re, b_im = (t.astype(F32) for t in (a_re, a_im, b_re, b_im))
    c_re, c_im = c_re.astype(F32), c_im.astype(F32)
    dt = jnp.exp(log_dt.astype(F32))[..., None]
    mag = jnp.exp(dt * a_re)
    ab_re, ab_im = mag * jnp.cos(dt * a_im), mag * jnp.sin(dt * a_im)
    nr, ni = ab_re - 1.0, ab_im
    den = a_re * a_re + a_im * a_im
    cf_re = (nr * a_re + ni * a_im) / den
    cf_im = (ni * a_re - nr * a_im) / den
    bb_re = cf_re[..., None] * b_re - cf_im[..., None] * b_im
    bb_im = cf_re[..., None] * b_im + cf_im[..., None] * b_re

    k = jnp.arange(tc + 1, dtype=F32)[:, None, None, None]
    pmag = jnp.exp(k * (dt * a_re)[None])
    pang = k * (dt * a_im)[None]
    pw_re, pw_im = pmag * jnp.cos(pang), pmag * jnp.sin(pang)

    ab_k_re = pw_re[..., None] * bb_re[None] - pw_im[..., None] * bb_im[None]
    ab_k_im = pw_re[..., None] * bb_im[None] + pw_im[..., None] * bb_re[None]
    ca_k_re = c_re[None] * pw_re[:, :, :, None, :] - c_im[None] * pw_im[:, :, :, None, :]
    ca_k_im = c_re[None] * pw_im[:, :, :, None, :] + c_im[None] * pw_re[:, :, :, None, :]

    taps = (jnp.einsum('dghp,kdgpj->kdghj', c_re, ab_k_re[:tc], precision=HIGHEST)
            - jnp.einsum('dghp,kdgpj->kdghj', c_im, ab_k_im[:tc], precision=HIGHEST))
    s_idx = jnp.arange(tc)[:, None]
    t_idx = jnp.arange(tc)[None, :]
    hc = b_re.shape[-1]
    g = a_re.shape[1]
    p = a_re.shape[2]

    def toeplitz(tp, lag, valid):
        m = jnp.where(valid[:, :, None, None, None], tp[jnp.clip(lag, 0, tc - 1)], 0.0)
        return m.transpose(2, 0, 4, 1, 3).reshape(g, tc * hc, tc * hc)

    m_f = toeplitz(taps[:, 0], t_idx - s_idx, t_idx >= s_idx)
    m_b = toeplitz(taps[:, 1], s_idx - t_idx, s_idx >= t_idx)
    m = jnp.stack([m_f, m_b]).astype(BF16)

    def pin_of(re, im):
        both = jnp.concatenate([re, im], axis=2)
        return both.transpose(1, 0, 3, 2).reshape(g, tc * hc, 2 * p)

    pin = jnp.stack([pin_of(ab_k_re[:tc, 0][::-1], ab_k_im[:tc, 0][::-1]),
                     pin_of(ab_k_re[:tc, 1], ab_k_im[:tc, 1])]).astype(BF16)

    def pout_of(re, im):
        both = jnp.concatenate([re, -im], axis=3)
        return both.transpose(1, 3, 0, 2).reshape(g, 2 * p, tc * hc)

    pout = jnp.stack([pout_of(ca_k_re[1:tc + 1, 0], ca_k_im[1:tc + 1, 0]),
                      pout_of(ca_k_re[1:tc + 1, 1][::-1], ca_k_im[1:tc + 1, 1][::-1])]).astype(BF16)

    at_re, at_im = pw_re[tc], pw_im[tc]
    ar = jnp.concatenate([at_re, at_re], axis=-1)[:, :, None, :]
    ai = jnp.concatenate([-at_im, at_im], axis=-1)[:, :, None, :]
    return m, pin, pout, ar, ai


def _s5_mix(u_ctx, u_lat, s5_d, tables):
    m, pin, pout, ar, ai = tables
    b, lc, width = u_ctx.shape
    l = u_lat.shape[1]
    g = m.shape[1]
    hc = width // g
    tc = S5_CHUNK
    w = tc * hc
    ncc, nlc = lc // tc, l // tc
    nch = ncc + nlc
    u = jnp.concatenate([u_ctx, u_lat], axis=1)
    xg = u.reshape(b, nch, tc, g, hc).transpose(3, 1, 0, 2, 4).reshape(g, nch * b, w)
    dg = jnp.tile(s5_d.reshape(g, 1, hc), (1, tc, 1)).reshape(g, 1, w)
    pgrp = lambda a: pl.BlockSpec((2, 1) + a.shape[2:], lambda i: (0, i, 0, 0))
    y = pl.pallas_call(
        functools.partial(_s5_kernel, nb=b, n_chunks=nch, n_ctx_chunks=ncc),
        grid=(g,),
        in_specs=[pl.BlockSpec((1, nch * b, w), lambda i: (i, 0, 0)),
                  pl.BlockSpec((1, 1, w), lambda i: (i, 0, 0)),
                  pgrp(m), pgrp(pin), pgrp(pout), pgrp(ar), pgrp(ai)],
        out_specs=pl.BlockSpec((1, nlc * b, w), lambda i: (i, 0, 0)),
        out_shape=_sds((g, nlc * b, w)),
        scratch_shapes=[pltpu.VMEM((nch * b, pin.shape[-1]), F32)],
        compiler_params=_cparams(("parallel",), 56),
        name="s5_mix",
    )(xg, dg, m, pin, pout, ar, ai)
    return y.reshape(g, nlc, b, tc, hc).transpose(2, 1, 3, 0, 4).reshape(b, l, width)


def _gelu(x):
    return 0.5 * x * (1.0 + lax.erf(x * (1.0 / math.sqrt(2.0))))


def _mix_out_kernel(s5_ref, rw_ref, bon_ref, g_ref, x_ref, g1_ref, sh2_ref, sc2_ref, n2_ref,
                    wglu_ref, bglu_ref, lnw_ref, lnb_ref, wos_ref, wor_ref, h1_ref, hn_ref, *, head):
    y1 = _gelu(s5_ref[0])
    s5o = y1 * jax.nn.sigmoid(_dot(y1.astype(BF16), wglu_ref[...]) + bglu_ref[...])
    y = rw_ref[0, 0] + rw_ref[1, 0]
    seg_m = _seg_matrix(y.shape[-1], head)
    inv = 1.0 / head
    mu = _seg_sum(y, seg_m) * inv
    yc = y - mu
    var = _seg_sum(yc * yc, seg_m) * inv
    yn = yc * lax.rsqrt(var + GN_EPS) * lnw_ref[...] + lnb_ref[...]
    rwo = (yn + bon_ref[0]) * g_ref[0]
    mix = _dot(s5o.astype(BF16), wos_ref[...]) + _dot(rwo.astype(BF16), wor_ref[...])
    h1 = x_ref[0] + g1_ref[0] * mix
    h1_ref[0] = h1
    ms = jnp.mean(h1 * h1, axis=-1, keepdims=True)
    hn_ref[0] = (h1 * lax.rsqrt(ms + NORM_EPS) * n2_ref[...]) * (1.0 + sc2_ref[0]) + sh2_ref[0]


def _mix_out(s5y, rwy, bonus, gate, x, g1, sh2, sc2, n2g, w_glu, b_glu, ln_w, ln_b, w_out_s, w_out_r, head):
    b, l, d = x.shape
    width = s5y.shape[-1]
    tm = min(512, l)
    tok = lambda c: pl.BlockSpec((1, tm, c), lambda i, j: (i, j, 0))
    mod = pl.BlockSpec((1, 1, d), lambda i, j: (i, 0, 0))
    full = lambda a: pl.BlockSpec(a.shape, lambda i, j: (0,) * a.ndim)
    consts = (n2g, w_glu, b_glu, ln_w, ln_b, w_out_s, w_out_r)
    return pl.pallas_call(
        functools.partial(_mix_out_kernel, head=head),
        grid=(b, l // tm),
        in_specs=[tok(width), pl.BlockSpec((2, 1, tm, width), lambda i, j: (0, i, j, 0)), tok(width), tok(width)]
        + [tok(d), mod, mod, mod] + [full(a) for a in consts],
        out_specs=[tok(d), tok(d)],
        out_shape=[_sds((b, l, d)), _sds((b, l, d))],
        compiler_params=_cparams(("parallel", "parallel")),
        name="mix_out",
    )(s5y, rwy, bonus, gate, x, g1, sh2, sc2, *consts)


def _staircase(k):
    return [(a, b) for a in range(k) for b in range(k) if (a + 1) * (b + 1) <= k]


def _topk_rows(s, k):
    rows = s.shape[0]
    iota = lax.broadcasted_iota(jnp.int32, s.shape, 0)
    vals, idxs = [], []
    for _ in range(k):
        m = jnp.max(s, axis=0, keepdims=True)
        idx = jnp.min(jnp.where(s == m, iota, rows), axis=0, keepdims=True)
        vals.append(m)
        idxs.append(idx)
        s = jnp.where(iota == idx, -jnp.inf, s)
    return jnp.concatenate(vals, axis=0), jnp.concatenate(idxs, axis=0).astype(F32)


def _peer_topk_kernel(hn_ref, wq_ref, keys_ref, sel1_ref, sel2_ref, pad_ref, off_ref, gate_ref, *, heads, nkeys, half,
                      rows_per_expert):
    hb = hn_ref[...].astype(BF16)
    nt = (((1,), (1,)), ((), ()))
    sel1 = sel1_ref[...]
    sel2 = sel2_ref[...]
    pick = lambda sel, a: jnp.dot(sel, a, precision=HIGHEST, preferred_element_type=F32)
    ids, gates = [], []
    for h in range(heads):
        top = []
        for c in range(2):
            row0 = (h * 2 + c) * half
            qt = lax.dot_general(wq_ref[row0:row0 + half, :], hb, nt, preferred_element_type=F32)
            st = _dot(keys_ref[h * 2 + c], qt.astype(BF16))
            top.append(_topk_rows(st, PEER_TOPK))
        (s1, i1), (s2, i2) = top
        cand = pick(sel1, s1) + pick(sel2, s2) + pad_ref[...]
        cid = pick(sel1, i1) * nkeys + pick(sel2, i2)
        best, pos = _topk_rows(cand, PEER_TOPK)
        riota = lax.broadcasted_iota(jnp.int32, cid.shape, 0).astype(F32)
        eid = jnp.concatenate(
            [jnp.sum(jnp.where(riota == pos[r:r + 1], cid, 0.0), axis=0, keepdims=True) for r in range(PEER_TOPK)], axis=0)
        e = jnp.exp(best - jnp.max(best, axis=0, keepdims=True))
        gates.append(e / jnp.sum(e, axis=0, keepdims=True))
        ids.append(eid)
    eid = jnp.concatenate(ids, axis=0).T.astype(jnp.int32)
    eid = jnp.clip(eid, 0, nkeys * nkeys - 1)
    off_ref[...] = eid * rows_per_expert
    gate_ref[...] = jnp.concatenate(gates, axis=0).T


def _peer_topk(hn, wq_t, keys, rows_per_expert):
    n, d = hn.shape
    heads = keys.shape[0] // 2
    nkeys, half = keys.shape[1], keys.shape[2]
    tm = min(512, n)
    pairs = _staircase(PEER_TOPK)
    rows = -(-len(pairs) // SUBLANES) * SUBLANES
    sel1 = np.zeros((rows, PEER_TOPK), np.float32)
    sel2 = np.zeros((rows, PEER_TOPK), np.float32)
    pad = np.zeros((rows, 1), np.float32)
    for i, (a, b) in enumerate(pairs):
        sel1[i, a] = 1.0
        sel2[i, b] = 1.0
    pad[len(pairs):] = -np.inf
    slots = heads * PEER_TOPK
    full = lambda a: pl.BlockSpec(a.shape, lambda i: (0,) * a.ndim)
    consts = (wq_t, keys, jnp.asarray(sel1), jnp.asarray(sel2), jnp.asarray(pad))
    return pl.pallas_call(
        functools.partial(_peer_topk_kernel, heads=heads, nkeys=nkeys, half=half, rows_per_expert=rows_per_expert),
        grid=(n // tm,),
        in_specs=[pl.BlockSpec((tm, d), lambda i: (i, 0))] + [full(a) for a in consts],
        out_specs=[pl.BlockSpec((tm, slots), lambda i: (i, 0))] * 2,
        out_shape=[_sds((n, slots), jnp.int32), _sds((n, slots))],
        compiler_params=_cparams(("parallel",)),
        name="peer_topk",
    )(hn, *consts)


def _pack_table(tab):
    e, d = tab.shape
    half = d // 2
    bits = lax.bitcast_convert_type(tab.astype(BF16), jnp.uint16).astype(jnp.uint32)
    packed = ((bits[:, :half] << 16) | bits[:, half:]).reshape(e * (half // LANES), LANES)
    return jnp.pad(packed, ((0, SUBLANES), (0, 0)))


def _expert_halves(tab_ref, off):
    bits = tab_ref[pl.ds(off, SUBLANES), :]
    return pltpu.bitcast(bits & jnp.uint32(0xFFFF0000), F32), pltpu.bitcast(bits << 16, F32)


def _peer_u_kernel(off_ref, h_ref, gate_ref, tab_ref, w_ref, q_ref, r_ref, *, tq, slots):
    half_rows = SUBLANES // 2
    sub = lax.broadcasted_iota(jnp.int32, (SUBLANES, LANES), 0)
    low = sub < half_rows
    groups = slots // SUBLANES
    qrows = slots * half_rows

    ones = jnp.ones((SUBLANES, LANES), BF16)
    nt = (((1,), (1,)), ((), ()))

    tg = PEER_U_GROUP
    ngroups = tq // tg

    def lane_sums(g):
        start = g * tg * qrows if isinstance(g, int) else pl.multiple_of(g * (tg * qrows), tg * qrows)
        sums = lax.dot_general(ones, q_ref[pl.ds(start, tg * qrows), :], nt, preferred_element_type=F32)
        for i in range(tg):
            r_ref[pl.ds(g * tg + i, 1), :] = sums[0:1, i * qrows:(i + 1) * qrows]

    q_ref[(tq - tg) * qrows:, :] = jnp.zeros((tg * qrows, LANES), BF16)

    def token_group(g, carry):
        lane_sums(jnp.where(g == 0, ngroups - 1, g - 1))
        for i in range(tg):
            token(g * tg + i)
        return carry

    def token(t):
        h = h_ref[t]
        h_first = pltpu.bitcast(jnp.where(low, h, 0.0).astype(BF16).astype(F32), jnp.uint32)
        h_second = pltpu.bitcast(jnp.where(low, pltpu.roll(h, half_rows, 0), 0.0).astype(BF16).astype(F32), jnp.uint32)
        h_pair = pltpu.bitcast(h_first | (h_second >> 16), BF16)
        q_tok = q_ref.at[pl.ds(pl.multiple_of(t * qrows, qrows), qrows)]
        grows = SUBLANES * half_rows
        for g8 in range(groups):
            base = t * slots + g8 * SUBLANES
            ps = []
            for r in range(SUBLANES):
                bits = tab_ref[pl.ds(off_ref[0, 0, base + r], SUBLANES), :]
                prod = pltpu.bitcast(pltpu.bitcast(bits, BF16) * h_pair, jnp.uint32)
                ps.append(pltpu.bitcast(prod & jnp.uint32(0xFFFF0000), F32) + pltpu.bitcast(prod << 16, F32))
            pairs = [ps[i] + pltpu.roll(ps[i + 1], half_rows, 0) for i in range(0, SUBLANES, 2)]
            q_tok[g8 * grows:(g8 + 1) * grows, :] = jnp.concatenate(pairs, axis=0).astype(BF16)

    lax.fori_loop(0, ngroups, token_group, 0)
    lane_sums(ngroups - 1)

    fold = (lax.broadcasted_iota(jnp.int32, (qrows, slots), 0) // half_rows
            == lax.broadcasted_iota(jnp.int32, (qrows, slots), 1))
    fold = jnp.where(fold, 1.0, 0.0).astype(BF16)
    rs = r_ref[...]
    rh = rs.astype(BF16)
    rl = (rs - rh.astype(F32)).astype(BF16)
    act = _dot(rh, fold) + _dot(rl, fold)
    w_ref[...] = gate_ref[...] * _gelu(act)


def _smem_block(tq, slots):
    return pl.BlockSpec((1, 1, tq * slots), lambda i: (i, 0, 0), memory_space=pltpu.SMEM)


def _peer_u(off_blocks, hn3, gate, tab):
    n, rows, lanes = hn3.shape
    slots = gate.shape[1]
    tq = off_blocks.shape[-1] // slots
    return pl.pallas_call(
        functools.partial(_peer_u_kernel, tq=tq, slots=slots),
        grid=(n // tq,),
        in_specs=[_smem_block(tq, slots),
                  pl.BlockSpec((tq, rows, lanes), lambda i: (i, 0, 0)),
                  pl.BlockSpec((tq, slots), lambda i: (i, 0)),
                  pl.BlockSpec(tab.shape, lambda i: (0, 0), pipeline_mode=pl.Buffered(1))],
        out_specs=pl.BlockSpec((tq, slots), lambda i: (i, 0)),
        out_shape=_sds((n, slots)),
        scratch_shapes=[pltpu.VMEM((tq * slots * (SUBLANES // 2), lanes), BF16),
                        pltpu.VMEM((tq, slots * (SUBLANES // 2)), F32)],
        compiler_params=_cparams(("parallel",), 52),
        name="peer_u",
    )(off_blocks, hn3, gate, tab)


def _peer_v_kernel(off_ref, w_ref, tab_ref, h1_ref, g2_ref, nf_ref, o_ref, ws_ref, dg_ref, ot_ref, *, tq, slots):
    sub = lax.broadcasted_iota(jnp.int32, (SUBLANES, LANES), 0)
    low = sub < SUBLANES // 2

    eye = lax.broadcasted_iota(jnp.int32, (slots, slots), 0) == lax.broadcasted_iota(jnp.int32, (slots, slots), 1)

    def diag(t, carry):
        wrow = jnp.broadcast_to(w_ref[pl.ds(t, 1), :], (slots, slots))
        dg_ref[pl.ds(pl.multiple_of(t * slots, slots), slots), :] = jnp.where(eye, wrow, 0.0).astype(BF16)
        return carry

    lax.fori_loop(0, tq, diag, 0, unroll=4)
    ws_ref[...] = _dot(dg_ref[...], jnp.ones((slots, LANES), BF16))

    def token(t, carry):
        zero = jnp.zeros((SUBLANES, LANES), F32)
        first, second = [zero, zero], [zero, zero]
        for s in range(slots):
            hi, lo = _expert_halves(tab_ref, off_ref[0, 0, t * slots + s])
            w = jnp.broadcast_to(ws_ref[pl.ds(t * slots + s, 1), :], (SUBLANES, LANES))
            first[s % 2] = first[s % 2] + hi * w
            second[s % 2] = second[s % 2] + lo * w
        tile = jnp.where(low, first[0] + first[1], pltpu.roll(second[0] + second[1], SUBLANES // 2, 0))
        ot_ref[pl.ds(pl.multiple_of(t * SUBLANES, SUBLANES), SUBLANES), :] = tile
        return carry

    lax.fori_loop(0, tq, token, 0)
    peer = jnp.concatenate([ot_ref[pl.ds(j, tq, stride=SUBLANES), :] for j in range(SUBLANES)], axis=1)
    h = h1_ref[...] + g2_ref[0] * peer
    ms = jnp.mean(h * h, axis=-1, keepdims=True)
    o_ref[...] = h * lax.rsqrt(ms + NORM_EPS) * nf_ref[...]


def _peer_v(off_blocks, w, tab, h1, g2, nfg, tokens_per_row):
    n, slots = w.shape
    tq = off_blocks.shape[-1] // slots
    lanes = tab.shape[-1]
    d = h1.shape[-1]
    assert slots == lanes, "the weight splat uses one (slots, lanes) identity tile"
    assert tokens_per_row % tq == 0, "a token tile must not straddle batch rows (one gate vector per tile)"
    tiles_per_row = tokens_per_row // tq
    return pl.pallas_call(
        functools.partial(_peer_v_kernel, tq=tq, slots=slots),
        grid=(n // tq,),
        in_specs=[_smem_block(tq, slots),
                  pl.BlockSpec((tq, slots), lambda i: (i, 0)),
                  pl.BlockSpec(tab.shape, lambda i: (0, 0), pipeline_mode=pl.Buffered(1)),
                  pl.BlockSpec((tq, d), lambda i: (i, 0)),
                  pl.BlockSpec((1, 1, d), lambda i: (i // tiles_per_row, 0, 0)),
                  pl.BlockSpec((1, d), lambda i: (0, 0))],
        out_specs=pl.BlockSpec((tq, d), lambda i: (i, 0)),
        out_shape=_sds((n, d)),
        scratch_shapes=[pltpu.VMEM((tq * slots, lanes), F32), pltpu.VMEM((tq * slots, slots), BF16),
                        pltpu.VMEM((tq * SUBLANES, lanes), F32)],
        compiler_params=_cparams(("parallel",), 52),
        name="peer_v",
    )(off_blocks, w, tab, h1, g2, nfg)


def _pad_cols(a, n):
    return jnp.pad(a, [(0, 0)] * (a.ndim - 1) + [(0, n - a.shape[-1])])


def _layer(h, hc, c, c_ctx, p, norm_f_g):
    b, l, d = h.shape
    lc = hc.shape[1]
    s5w = p['s5_d'].shape[-1]
    rww = p['rw_w0'].shape[-1]
    heads, head = p['rw_r_k'].shape
    n_dec, n_aaa, n_gate = p['rw_w_w2'].shape[1], p['rw_w_a2'].shape[1], p['rw_w_g2'].shape[0]
    n_lora = n_dec + n_aaa + n_gate
    lora_pad = -(-n_lora // LANES) * LANES

    rows = -(-(b + 1) // SUBLANES) * SUBLANES
    cc = jnp.zeros((rows, d), F32).at[:b].set(c).at[b].set(c_ctx)
    mods = _ada_mods(cc, p['w_ada'], p['b_ada'])
    sh1, sc1, g1, sh2, sc2, g2 = [m.reshape(rows, 1, d) for m in jnp.split(mods, 6, axis=-1)]
    lat = lambda m: m[:b]
    ctxm = lambda m: m[b:b + 1]

    w_in = p['w_in']
    w_s5 = w_in[:, :s5w].astype(BF16)
    w_main = w_in[:, s5w:s5w + 3 * rww].astype(BF16)
    w_lora = _pad_cols(w_in[:, s5w + 3 * rww:], lora_pad).astype(BF16)
    n1g = p['norm1_g'].reshape(1, d)
    u_l, zm_l, zl_l = _in_proj(h, lat(sh1), lat(sc1), n1g, w_s5, w_main, w_lora)
    u_c, zm_c, zl_c = _in_proj(hc, ctxm(sh1), ctxm(sc1), n1g, w_s5, w_main, w_lora)

    mu = p['rw_mu']
    mu_main = mu[:3 * rww].reshape(1, -1)
    mu_lora = _pad_cols(mu[3 * rww:], lora_pad).reshape(1, -1)
    place = lambda w, r0: jnp.zeros(w.shape[:-2] + (lora_pad, rww), F32).at[..., r0:r0 + w.shape[-2], :].set(w).astype(BF16)
    rw_params = (place(p['rw_w_g2'], n_dec + n_aaa), place(p['rw_w_w2'], 0), place(p['rw_w_a2'], n_dec),
                 p['rw_w0'].reshape(2, 1, rww), p['rw_a0'].reshape(2, 1, rww),
                 p['rw_k_k'].reshape(1, rww), p['rw_k_a'].reshape(1, rww), p['rw_r_k'].reshape(1, rww))
    prep_params = (mu_main, mu_lora) + rw_params
    ctx_streams = _rwkv_prep(zm_c, zl_c, prep_params, head, lc + l)
    *streams, gate_l, bonus_l = _rwkv_prep(zm_l, zl_l, prep_params, head, lc + l, ctx_streams)
    rw_y = _rwkv_chunked(tuple(streams[:3]), tuple(streams[3:]), lc, head)

    tables = _s5_tables(p['s5_a_re'], p['s5_a_im'], p['s5_log_dt'], p['s5_b_re'], p['s5_b_im'],
                        p['s5_c_re'], p['s5_c_im'], S5_CHUNK)
    s5_y = _s5_mix(u_c, u_l, p['s5_d'], tables)

    w_out = p['w_out']
    h1, hn2 = _mix_out(s5_y, rw_y, bonus_l, gate_l, h, lat(g1), lat(sh2), lat(sc2), p['norm2_g'].reshape(1, d),
                       p['s5_w_glu'].astype(BF16), p['s5_b_glu'].reshape(1, s5w),
                       p['rw_ln_w'].reshape(1, rww), p['rw_ln_b'].reshape(1, rww),
                       w_out[:s5w].astype(BF16), w_out[s5w:].astype(BF16), head)

    n = b * l
    keys = p['peer_keys']
    pheads, _, nkeys, half = keys.shape
    assert d == VREG_WORDS, "peer_u / peer_v hold one token's features in a single (8, 128) tile"
    off, gate = _peer_topk(hn2.reshape(n, d), p['peer_w_q'].T.astype(BF16),
                           keys.reshape(pheads * 2, nkeys, half).astype(BF16), d // (2 * LANES))
    slots = gate.shape[1]
    tq = min(PEER_TILE, n)
    blocks = lambda a: a.reshape(n // tq, 1, tq * slots)
    w = _peer_u(blocks(off), hn2.reshape(n, SUBLANES, LANES), gate, _pack_table(p['peer_u']))
    out = _peer_v(blocks(off), w, _pack_table(p['peer_v']), h1.reshape(n, d), lat(g2), norm_f_g.reshape(1, d), l)
    return out.reshape(b, l, d)


def kernel(x, c, ctx, c_ctx, w_ada, b_ada, norm1_g, norm2_g, w_in, s5_a_re, s5_a_im, s5_log_dt, s5_b_re, s5_b_im, s5_c_re, s5_c_im, s5_d, s5_w_glu, s5_b_glu, rw_mu, rw_w0, rw_w_w2, rw_a0, rw_w_a2, rw_w_g2, rw_k_k, rw_k_a, rw_r_k, rw_ln_w, rw_ln_b, w_out, peer_w_q, peer_keys, peer_u, peer_v, norm_f_g):
    assert w_ada.shape[0] == 1, "single-layer block: the context stream is never updated"
    p = dict(w_ada=w_ada[0], b_ada=b_ada[0], norm1_g=norm1_g[0], norm2_g=norm2_g[0], w_in=w_in[0],
             s5_a_re=s5_a_re[0], s5_a_im=s5_a_im[0], s5_log_dt=s5_log_dt[0], s5_b_re=s5_b_re[0], s5_b_im=s5_b_im[0],
             s5_c_re=s5_c_re[0], s5_c_im=s5_c_im[0], s5_d=s5_d[0], s5_w_glu=s5_w_glu[0], s5_b_glu=s5_b_glu[0],
             rw_mu=rw_mu[0], rw_w0=rw_w0[0], rw_w_w2=rw_w_w2[0], rw_a0=rw_a0[0], rw_w_a2=rw_w_a2[0],
             rw_w_g2=rw_w_g2[0], rw_k_k=rw_k_k[0], rw_k_a=rw_k_a[0], rw_r_k=rw_r_k[0], rw_ln_w=rw_ln_w[0],
             rw_ln_b=rw_ln_b[0], w_out=w_out[0], peer_w_q=peer_w_q[0], peer_keys=peer_keys[0],
             peer_u=peer_u[0], peer_v=peer_v[0])
    return _layer(x, ctx, c, c_ctx, p, norm_f_g)
```

```python
import functools
import math

import jax
import jax.numpy as jnp
import numpy as np
from jax import lax
from jax.experimental import pallas as pl
from jax.experimental.pallas import tpu as pltpu

F32 = jnp.float32
BF16 = jnp.bfloat16
HIGHEST = lax.Precision.HIGHEST

SUBLANES = 8
LANES = 128
VREG_WORDS = SUBLANES * LANES

GRID_W = 64
NORM_EPS = 1e-6
GN_EPS = 64e-5
KK_EPS = 1e-12
PEER_TOPK = 16

S5_CHUNK = 32
RWKV_CHUNK = 64
RWKV_ROWS = 4
PEER_TILE = 64
PEER_U_GROUP = 4


def _dot(a, b):
    return jnp.dot(a, b, preferred_element_type=F32)


def _cparams(sem, vmem_mb=48):
    return pltpu.CompilerParams(dimension_semantics=sem, vmem_limit_bytes=vmem_mb * 1024 * 1024)


def _sds(shape, dtype=F32):
    return jax.ShapeDtypeStruct(shape, dtype)


def _mods_kernel(c_ref, w_ref, b_ref, o_ref):
    c = c_ref[...]
    s = c * jax.nn.sigmoid(c)
    o_ref[...] = jnp.dot(s, w_ref[...], precision=HIGHEST, preferred_element_type=F32) + b_ref[...]


def _ada_mods(cc, w_ada, b_ada):
    rows, d = cc.shape
    n = w_ada.shape[1]
    bn = d if n % d == 0 else n
    return pl.pallas_call(
        _mods_kernel,
        grid=(n // bn,),
        in_specs=[pl.BlockSpec((rows, d), lambda j: (0, 0)),
                  pl.BlockSpec((d, bn), lambda j: (0, j)),
                  pl.BlockSpec((1, bn), lambda j: (0, j))],
        out_specs=pl.BlockSpec((rows, bn), lambda j: (0, j)),
        out_shape=_sds((rows, n)),
        compiler_params=_cparams(("arbitrary",)),
        name="ada_mods",
    )(cc, w_ada, b_ada.reshape(1, n))


def _inproj_kernel(x_ref, sh_ref, sc_ref, g_ref, ws_ref, wz_ref, wl_ref, u_ref, z_ref, l_ref):
    x = x_ref[0]
    ms = jnp.mean(x * x, axis=-1, keepdims=True)
    hn = (x * lax.rsqrt(ms + NORM_EPS) * g_ref[...]) * (1.0 + sc_ref[0]) + sh_ref[0]
    hb = hn.astype(BF16)
    u_ref[0] = _dot(hb, ws_ref[...])
    z_ref[0] = _dot(hb, wz_ref[...])
    l_ref[0] = _dot(hb, wl_ref[...])


def _in_proj(x, shift, scale, gain, w_s5, w_main, w_lora):
    b, l, d = x.shape
    tm = min(512, l)
    bm = shift.shape[0]
    mod_map = (lambda i, j: (i, 0, 0)) if bm > 1 else (lambda i, j: (0, 0, 0))
    full = lambda a: pl.BlockSpec(a.shape, lambda i, j: (0,) * a.ndim)
    return pl.pallas_call(
        _inproj_kernel,
        grid=(b, l // tm),
        in_specs=[pl.BlockSpec((1, tm, d), lambda i, j: (i, j, 0)),
                  pl.BlockSpec((1, 1, d), mod_map), pl.BlockSpec((1, 1, d), mod_map),
                  full(gain), full(w_s5), full(w_main), full(w_lora)],
        out_specs=[pl.BlockSpec((1, tm, w.shape[1]), lambda i, j: (i, j, 0)) for w in (w_s5, w_main, w_lora)],
        out_shape=[_sds((b, l, w.shape[1])) for w in (w_s5, w_main, w_lora)],
        compiler_params=_cparams(("parallel", "parallel")),
        name="in_proj",
    )(x, shift, scale, gain, w_s5, w_main, w_lora)


def _seg_matrix(width, seg):
    r = lax.broadcasted_iota(jnp.int32, (width, width), 0) // seg
    c = lax.broadcasted_iota(jnp.int32, (width, width), 1) // seg
    return jnp.where(r == c, 1.0, 0.0).astype(BF16)


def _seg_sum(x, seg_m):
    hi = x.astype(BF16)
    lo = (x - hi.astype(F32)).astype(BF16)
    return _dot(hi, seg_m) + _dot(lo, seg_m)


def _rwkv_streams(zz, zl, p, head, outs, with_gate):
    (wg_ref, ww_ref, wa_ref, w0_ref, a0_ref, kk_ref, ka_ref, rk_ref) = p
    width = kk_ref.shape[-1]
    r = zz[:, :width]
    k = zz[:, width:2 * width]
    v = zz[:, 2 * width:3 * width]
    seg_m = _seg_matrix(width, head)
    kk = k * kk_ref[...]
    kk = kk * lax.rsqrt(_seg_sum(kk * kk, seg_m) + KK_EPS)
    outs["r"][0] = r
    outs["v"][0] = v
    outs["kk"][0] = kk
    tl = jnp.tanh(zl).astype(BF16)
    zb = zl.astype(BF16)
    bonus = None
    for d in range(2):
        outs["dec"][d, 0] = -math.exp(-0.5) * jax.nn.sigmoid(w0_ref[d] + _dot(tl, ww_ref[d]))
        a = jax.nn.sigmoid(a0_ref[d] + _dot(zb, wa_ref[d]))
        kd = k * (1.0 + (a - 1.0) * ka_ref[...])
        outs["kd"][d, 0] = kd
        outs["bb"][d, 0] = kk * a
        if with_gate:
            bd = _seg_sum(r * kd * rk_ref[...], seg_m) * v
            bonus = bd if bonus is None else bonus + bd
    if with_gate:
        outs["bonus"][0] = bonus
        outs["g"][0] = _dot(jax.nn.sigmoid(zl).astype(BF16), wg_ref[...])


def _lerp_shift(z, shifted, mu):
    return z + (shifted - z) * mu


def _grid_shift(zc, zp, zn, has_prev, has_next):
    t, c = zc.shape
    tok = lax.broadcasted_iota(jnp.int32, (t, 1), 0) % GRID_W
    col = lax.broadcasted_iota(jnp.int32, (1, c), 1) % 4
    left = jnp.where(tok == 0, 0.0, pltpu.roll(zc, 1, 0))
    right = jnp.where(tok == GRID_W - 1, 0.0, pltpu.roll(zc, t - 1, 0))
    zp = jnp.where(has_prev, zp, 0.0)
    zn = jnp.where(has_next, zn, 0.0)
    if t > GRID_W:
        up = jnp.concatenate([zp, zc[:t - GRID_W]], axis=0)
        down = jnp.concatenate([zc[GRID_W:], zn], axis=0)
    else:
        up, down = zp, zn
    return jnp.where(col == 0, left, jnp.where(col == 1, right, jnp.where(col == 2, up, down)))


def _seq_shift(zc):
    t, c = zc.shape
    tok = lax.broadcasted_iota(jnp.int32, (t, 1), 0)
    col = lax.broadcasted_iota(jnp.int32, (1, c), 1) % 2
    prev = jnp.where(tok == 0, 0.0, pltpu.roll(zc, 1, 0))
    nxt = jnp.where(tok == t - 1, 0.0, pltpu.roll(zc, t - 1, 0))
    return jnp.where(col == 0, prev, nxt)


_STREAMS = ("r", "v", "kk", "dec", "kd", "bb")


def _rwkv_prep_lat_kernel(zc_ref, zp_ref, zn_ref, lc_ref, lp_ref, ln_ref, mum_ref, mul_ref, *rest, head):
    p, o = rest[:8], rest[8 + len(_STREAMS):]
    outs = dict(zip(_STREAMS + ("g", "bonus"), o))
    j = pl.program_id(1)
    has_prev = j > 0
    has_next = j < pl.num_programs(1) - 1
    zc = zc_ref[0]
    lc = lc_ref[0]
    zz = _lerp_shift(zc, _grid_shift(zc, zp_ref[0], zn_ref[0], has_prev, has_next), mum_ref[...])
    zl = _lerp_shift(lc, _grid_shift(lc, lp_ref[0], ln_ref[0], has_prev, has_next), mul_ref[...])
    _rwkv_streams(zz, zl, p, head, outs, True)


def _rwkv_prep_ctx_kernel(zc_ref, lc_ref, mum_ref, mul_ref, *rest, head):
    p, o = rest[:8], rest[8:]
    outs = dict(zip(_STREAMS, o))
    zc = zc_ref[0]
    lc = lc_ref[0]
    zz = _lerp_shift(zc, _seq_shift(zc), mum_ref[...])
    zl = _lerp_shift(lc, _seq_shift(lc), mul_ref[...])
    _rwkv_streams(zz, zl, p, head, outs, False)


def _rwkv_prep(z_main, z_lora, params, head, l_total, ctx_streams=None):
    b, l, cm = z_main.shape
    cl = z_lora.shape[-1]
    width = cm // 3
    full = lambda a: pl.BlockSpec(a.shape, lambda i, j: (0,) * a.ndim)
    grid_mode = ctx_streams is not None
    if grid_mode:
        l_ctx = l_total - l
        rows_per_tile = next(r for r in (8, 4, 2, 1) if l % (r * GRID_W) == 0 and l_ctx % (r * GRID_W) == 0)
        tt = rows_per_tile * GRID_W
        nt = l // tt
        first = l_ctx // tt
        last_row = l // GRID_W - 1
        cur = lambda c: pl.BlockSpec((1, tt, c), lambda i, j: (i, j, 0))
        prev = lambda c: pl.BlockSpec((1, GRID_W, c), lambda i, j: (i, jnp.maximum(j * rows_per_tile - 1, 0), 0))
        nxt = lambda c: pl.BlockSpec((1, GRID_W, c), lambda i, j: (i, jnp.minimum((j + 1) * rows_per_tile, last_row), 0))
        data = [z_main, z_main, z_main, z_lora, z_lora, z_lora]
        data_specs = [cur(cm), prev(cm), nxt(cm), cur(cl), prev(cl), nxt(cl)]
        kern = functools.partial(_rwkv_prep_lat_kernel, head=head)
        extra = list(ctx_streams)
        extra_specs = [pl.BlockSpec(memory_space=pl.ANY)] * len(extra)
        n_in = len(data) + len(params)
        aliases = {n_in + k: k for k in range(len(extra))}
        lat_only = 2
    else:
        tt, nt, first = l, 1, 0
        data = [z_main, z_lora]
        data_specs = [pl.BlockSpec((1, tt, cm), lambda i, j: (i, j, 0)), pl.BlockSpec((1, tt, cl), lambda i, j: (i, j, 0))]
        kern = functools.partial(_rwkv_prep_ctx_kernel, head=head)
        extra, extra_specs, aliases, lat_only = [], [], {}, 0
    single = pl.BlockSpec((1, tt, width), lambda i, j: (i, first + j, 0))
    double = pl.BlockSpec((2, 1, tt, width), lambda i, j: (0, i, first + j, 0))
    local = pl.BlockSpec((1, tt, width), lambda i, j: (i, j, 0))
    out_specs = [single] * 3 + [double] * 3 + [local] * lat_only
    out_shape = [_sds((b, l_total, width))] * 3 + [_sds((2, b, l_total, width))] * 3 + [_sds((b, l, width))] * lat_only
    return pl.pallas_call(
        kern,
        grid=(b, nt),
        in_specs=data_specs + [full(a) for a in params] + extra_specs,
        out_specs=out_specs,
        out_shape=out_shape,
        input_output_aliases=aliases,
        compiler_params=_cparams(("parallel", "parallel")),
        name="rwkv_prep_lat" if grid_mode else "rwkv_prep_ctx",
    )(*data, *params, *extra)


def _split_bf16(x):
    hi = x.astype(BF16)
    return hi, (x - hi.astype(F32)).astype(BF16)


def _rwkv_chunk_kernel(r_ref, v_ref, kk_ref, w_ref, kd_ref, bb_ref, y_ref, s_ref, *, c, head):
    d = pl.program_id(0)
    nb, _, width = r_ref.shape
    pair = 2 * head
    n_pairs = width // pair

    @pl.when(pl.program_id(2) == 0)
    def _():
        s_ref[...] = jnp.zeros_like(s_ref)

    ri = lax.broadcasted_iota(jnp.int32, (c, c), 0)
    ci = lax.broadcasted_iota(jnp.int32, (c, c), 1)
    before = jnp.where(d == 0, ci - ri, ri - ci) < 0
    incl = before | (ri == ci)
    eye = jnp.where(ri == ci, 1.0, 0.0)
    lmat = jnp.broadcast_to(jnp.where(incl, 1.0, 0.0).astype(BF16), (nb, c, c))

    def bmm(a, b, dims=((2,), (1,))):
        return lax.dot_general(a.astype(BF16), b.astype(BF16), (dims, ((0,), (0,))), preferred_element_type=F32)

    nt = ((2,), (2,))

    ld = w_ref[...]
    ld_hi, ld_lo = _split_bf16(ld)
    cum = bmm(lmat, ld_hi) + bmm(lmat, ld_lo)
    gn = jnp.exp(-cum)
    at = -(jnp.exp(cum - ld) * kk_ref[...])
    rt = jnp.exp(cum) * r_ref[...]
    bt = gn * bb_ref[...]
    kt = gn * kd_ref[...]
    vv = v_ref[...]

    def tiles(x):
        return jnp.stack([x[i, :, p * pair:(p + 1) * pair] for i in range(nb) for p in range(n_pairs)], axis=0)

    def per_head(x, select=None):
        xs = [x, x] if select is None else [jnp.where(select == hh, x, 0.0) for hh in range(2)]
        return jnp.stack(xs, axis=1).reshape((2 * x.shape[0],) + x.shape[1:])

    lane_head = lax.broadcasted_iota(jnp.int32, (1, 1, pair), 2) // head
    at_q, rt_q, bt_q, kt_q, v_q = tiles(at), tiles(rt), tiles(bt), tiles(kt), tiles(vv)
    s0 = s_ref[...]
    as0 = per_head(bmm(at_q, s0))
    rs0 = per_head(bmm(rt_q, s0))
    at_h, rt_h, v_h = per_head(at_q), per_head(rt_q), per_head(v_q)
    bt_h, kt_h = per_head(bt_q, lane_head), per_head(kt_q, lane_head)
    a_ab = jnp.where(before, bmm(at_h, bt_h, nt), 0.0)
    a_ak = jnp.where(before, bmm(at_h, kt_h, nt), 0.0)
    a_rb = jnp.where(incl, bmm(rt_h, bt_h, nt), 0.0)
    a_rk = jnp.where(incl, bmm(rt_h, kt_h, nt), 0.0)
    t = eye + a_ab
    pw = a_ab
    for _ in range(c.bit_length() - 2):
        pw = bmm(pw, pw)
        t = t + bmm(t, pw)
    u = bmm(t, as0 + bmm(a_ak, v_h))
    y = rs0 + bmm(a_rb, u) + bmm(a_rk, v_h)

    def merge(x):
        x = x.reshape((x.shape[0] // 2, 2) + x.shape[1:])
        return jnp.where(lane_head == 0, x[:, 0], x[:, 1])

    u_q, y_q = merge(u), merge(y)
    for i in range(nb):
        for p in range(n_pairs):
            y_ref[i, :, p * pair:(p + 1) * pair] = y_q[i * n_pairs + p]
    tn = ((1,), (1,))
    ones = jnp.ones((nb * n_pairs, c, pair), BF16)
    ds = bmm(bt_q, u_q, tn) + bmm(kt_q, v_q, tn)
    tot = bmm(tiles(ld_hi), ones, tn) + bmm(tiles(ld_lo), ones, tn)
    blk = (lax.broadcasted_iota(jnp.int32, (pair, pair), 0) // head
           == lax.broadcasted_iota(jnp.int32, (pair, pair), 1) // head)
    s_ref[...] = jnp.where(blk, jnp.exp(tot) * (s0 + ds), 0.0)


def _rwkv_chunked(shared, directional, l_ctx, head):
    b, ltot, width = shared[0].shape
    c = RWKV_CHUNK
    nb = RWKV_ROWS if b % RWKV_ROWS == 0 else 1
    nch, ncc = ltot // c, l_ctx // c
    nlc = nch - ncc

    def chunk(d, j):
        back = jnp.where(j < ncc, ncc - 1 - j, nch + ncc - 1 - j)
        return jnp.where(d == 0, j, back)

    def out_chunk(d, j):
        o = chunk(d, j)
        return jnp.where(o >= ncc, o - ncc, jnp.where(d == 0, 0, nlc - 1))

    sh_spec = pl.BlockSpec((nb, c, width), lambda d, i, j: (i, chunk(d, j), 0))
    di_spec = pl.BlockSpec((None, nb, c, width), lambda d, i, j: (d, i, chunk(d, j), 0))
    return pl.pallas_call(
        functools.partial(_rwkv_chunk_kernel, c=c, head=head),
        grid=(2, b // nb, nch),
        in_specs=[sh_spec] * 3 + [di_spec] * 3,
        out_specs=pl.BlockSpec((None, nb, c, width), lambda d, i, j: (d, i, out_chunk(d, j), 0)),
        out_shape=_sds((2, b, (nch - ncc) * c, width)),
        scratch_shapes=[pltpu.VMEM((nb * width // (2 * head), 2 * head, 2 * head), F32)],
        compiler_params=_cparams(("arbitrary", "arbitrary", "arbitrary")),
        name="rwkv_chunk",
    )(*shared, *directional)


def _s5_kernel(x_ref, d_ref, m_ref, pin_ref, pout_ref, ar_ref, ai_ref, y_ref, hall_ref, *, nb, n_chunks, n_ctx_chunks):
    x = x_ref[0]
    xb = x.astype(BF16)
    lat0 = n_ctx_chunks * nb
    xl = xb[lat0:]
    y = d_ref[0] * x[lat0:]
    half = ar_ref.shape[-1] // 2
    fwd = list(range(n_chunks))
    bwd = list(range(n_ctx_chunks - 1, -1, -1)) + list(range(n_chunks - 1, n_ctx_chunks - 1, -1))
    for d, order in enumerate((fwd, bwd)):
        v = _dot(xb, pin_ref[d, 0])
        ar = ar_ref[d, 0]
        ai = ai_ref[d, 0]
        h = jnp.zeros((nb, 2 * half), F32)
        for c in order:
            hall_ref[c * nb:(c + 1) * nb, :] = h
            h = h * ar + pltpu.roll(h, half, 1) * ai + v[c * nb:(c + 1) * nb]
        y = y + _dot(xl, m_ref[d, 0]) + _dot(hall_ref[lat0:, :].astype(BF16), pout_ref[d, 0])
    y_ref[0] = y


def _s5_tables(a_re, a_im, log_dt, b_re, b_im, c_re, c_im, tc):
    a_re, a_im, b_re, b_im = (t.astype(F32) for t in (a_re, a_im, b_re, b_im))
    c_re, c_im = c_re.astype(F32), c_im.astype(F32)
    dt = jnp.exp(log_dt.astype(F32))[..., None]
    mag = jnp.exp(dt * a_re)
    ab_re, ab_im = mag * jnp.cos(dt * a_im), mag * jnp.sin(dt * a_im)
    nr, ni = ab_re - 1.0, ab_im
    den = a_re * a_re + a_im * a_im
    cf_re = (nr * a_re + ni * a_im) / den
    cf_im = (ni * a_re - nr * a_im) / den
    bb_re = cf_re[..., None] * b_re - cf_im[..., None] * b_im
    bb_im = cf_re[..., None] * b_im + cf_im[..., None] * b_re

    k = jnp.arange(tc + 1, dtype=F32)[:, None, None, None]
    pmag = jnp.exp(k * (dt * a_re)[None])
    pang = k * (dt * a_im)[None]
    pw_re, pw_im = pmag * jnp.cos(pang), pmag * jnp.sin(pang)

    ab_k_re = pw_re[..., None] * bb_re[None] - pw_im[..., None] * bb_im[None]
    ab_k_im = pw_re[..., None] * bb_im[None] + pw_im[..., None] * bb_re[None]
    ca_k_re = c_re[None] * pw_re[:, :, :, None, :] - c_im[None] * pw_im[:, :, :, None, :]
    ca_k_im = c_re[None] * pw_im[:, :, :, None, :] + c_im[None] * pw_re[:, :, :, None, :]

    taps = (jnp.einsum('dghp,kdgpj->kdghj', c_re, ab_k_re[:tc], precision=HIGHEST)
            - jnp.einsum('dghp,kdgpj->kdghj', c_im, ab_k_im[:tc], precision=HIGHEST))
    s_idx = jnp.arange(tc)[:, None]
    t_idx = jnp.arange(tc)[None, :]
    hc = b_re.shape[-1]
    g = a_re.shape[1]
    p = a_re.shape[2]

    def toeplitz(tp, lag, valid):
        m = jnp.where(valid[:, :, None, None, None], tp[jnp.clip(lag, 0, tc - 1)], 0.0)
        return m.transpose(2, 0, 4, 1, 3).reshape(g, tc * hc, tc * hc)

    m_f = toeplitz(taps[:, 0], t_idx - s_idx, t_idx >= s_idx)
    m_b = toeplitz(taps[:, 1], s_idx - t_idx, s_idx >= t_idx)
    m = jnp.stack([m_f, m_b]).astype(BF16)

    def pin_of(re, im):
        both = jnp.concatenate([re, im], axis=2)
        return both.transpose(1, 0, 3, 2).reshape(g, tc * hc, 2 * p)

    pin = jnp.stack([pin_of(ab_k_re[:tc, 0][::-1], ab_k_im[:tc, 0][::-1]),
                     pin_of(ab_k_re[:tc, 1], ab_k_im[:tc, 1])]).astype(BF16)

    def pout_of(re, im):
        both = jnp.concatenate([re, -im], axis=3)
        return both.transpose(1, 3, 0, 2).reshape(g, 2 * p, tc * hc)

    pout = jnp.stack([pout_of(ca_k_re[1:tc + 1, 0], ca_k_im[1:tc + 1, 0]),
                      pout_of(ca_k_re[1:tc + 1, 1][::-1], ca_k_im[1:tc + 1, 1][::-1])]).astype(BF16)

    at_re, at_im = pw_re[tc], pw_im[tc]
    ar = jnp.concatenate([at_re, at_re], axis=-1)[:, :, None, :]
    ai = jnp.concatenate([-at_im, at_im], axis=-1)[:, :, None, :]
    return m, pin, pout, ar, ai


def _s5_mix(u_ctx, u_lat, s5_d, tables):
    m, pin, pout, ar, ai = tables
    b, lc, width = u_ctx.shape
    l = u_lat.shape[1]
    g = m.shape[1]
    hc = width // g
    tc = S5_CHUNK
    w = tc * hc
    ncc, nlc = lc // tc, l // tc
    nch = ncc + nlc
    u = jnp.concatenate([u_ctx, u_lat], axis=1)
    xg = u.reshape(b, nch, tc, g, hc).transpose(3, 1, 0, 2, 4).reshape(g, nch * b, w)
    dg = jnp.tile(s5_d.reshape(g, 1, hc), (1, tc, 1)).reshape(g, 1, w)
    pgrp = lambda a: pl.BlockSpec((2, 1) + a.shape[2:], lambda i: (0, i, 0, 0))
    y = pl.pallas_call(
        functools.partial(_s5_kernel, nb=b, n_chunks=nch, n_ctx_chunks=ncc),
        grid=(g,),
        in_specs=[pl.BlockSpec((1, nch * b, w), lambda i: (i, 0, 0)),
                  pl.BlockSpec((1, 1, w), lambda i: (i, 0, 0)),
                  pgrp(m), pgrp(pin), pgrp(pout), pgrp(ar), pgrp(ai)],
        out_specs=pl.BlockSpec((1, nlc * b, w), lambda i: (i, 0, 0)),
        out_shape=_sds((g, nlc * b, w)),
        scratch_shapes=[pltpu.VMEM((nch * b, pin.shape[-1]), F32)],
        compiler_params=_cparams(("parallel",), 56),
        name="s5_mix",
    )(xg, dg, m, pin, pout, ar, ai)
    return y.reshape(g, nlc, b, tc, hc).transpose(2, 1, 3, 0, 4).reshape(b, l, width)


def _gelu(x):
    return 0.5 * x * (1.0 + lax.erf(x * (1.0 / math.sqrt(2.0))))


def _mix_out_kernel(s5_ref, rw_ref, bon_ref, g_ref, x_ref, g1_ref, sh2_ref, sc2_ref, n2_ref,
                    wglu_ref, bglu_ref, lnw_ref, lnb_ref, wos_ref, wor_ref, h1_ref, hn_ref, *, head):
    y1 = _gelu(s5_ref[0])
    s5o = y1 * jax.nn.sigmoid(_dot(y1.astype(BF16), wglu_ref[...]) + bglu_ref[...])
    y = rw_ref[0, 0] + rw_ref[1, 0]
    seg_m = _seg_matrix(y.shape[-1], head)
    inv = 1.0 / head
    mu = _seg_sum(y, seg_m) * inv
    yc = y - mu
    var = _seg_sum(yc * yc, seg_m) * inv
    yn = yc * lax.rsqrt(var + GN_EPS) * lnw_ref[...] + lnb_ref[...]
    rwo = (yn + bon_ref[0]) * g_ref[0]
    mix = _dot(s5o.astype(BF16), wos_ref[...]) + _dot(rwo.astype(BF16), wor_ref[...])
    h1 = x_ref[0] + g1_ref[0] * mix
    h1_ref[0] = h1
    ms = jnp.mean(h1 * h1, axis=-1, keepdims=True)
    hn_ref[0] = (h1 * lax.rsqrt(ms + NORM_EPS) * n2_ref[...]) * (1.0 + sc2_ref[0]) + sh2_ref[0]


def _mix_out(s5y, rwy, bonus, gate, x, g1, sh2, sc2, n2g, w_glu, b_glu, ln_w, ln_b, w_out_s, w_out_r, head):
    b, l, d = x.shape
    width = s5y.shape[-1]
    tm = min(512, l)
    tok = lambda c: pl.BlockSpec((1, tm, c), lambda i, j: (i, j, 0))
    mod = pl.BlockSpec((1, 1, d), lambda i, j: (i, 0, 0))
    full = lambda a: pl.BlockSpec(a.shape, lambda i, j: (0,) * a.ndim)
    consts = (n2g, w_glu, b_glu, ln_w, ln_b, w_out_s, w_out_r)
    return pl.pallas_call(
        functools.partial(_mix_out_kernel, head=head),
        grid=(b, l // tm),
        in_specs=[tok(width), pl.BlockSpec((2, 1, tm, width), lambda i, j: (0, i, j, 0)), tok(width), tok(width)]
        + [tok(d), mod, mod, mod] + [full(a) for a in consts],
        out_specs=[tok(d), tok(d)],
        out_shape=[_sds((b, l, d)), _sds((b, l, d))],
        compiler_params=_cparams(("parallel", "parallel")),
        name="mix_out",
    )(s5y, rwy, bonus, gate, x, g1, sh2, sc2, *consts)


def _staircase(k):
    return [(a, b) for a in range(k) for b in range(k) if (a + 1) * (b + 1) <= k]


def _topk_rows(s, k):
    rows = s.shape[0]
    iota = lax.broadcasted_iota(jnp.int32, s.shape, 0).astype(F32)
    vals, idxs = [], []
    for _ in range(k):
        m = jnp.max(s, axis=0, keepdims=True)
        idx = jnp.min(jnp.where(s == m, iota, float(rows)), axis=0, keepdims=True)
        vals.append(m)
        idxs.append(idx)
        s = jnp.where(iota == idx, -jnp.inf, s)
    return jnp.concatenate(vals, axis=0), jnp.concatenate(idxs, axis=0)


def _peer_topk_kernel(hn_ref, wq_ref, keys_ref, sel1_ref, sel2_ref, pad_ref, off_ref, gate_ref, *, heads, nkeys, half,
                      rows_per_expert):
    hb = hn_ref[...].astype(BF16)
    nt = (((1,), (1,)), ((), ()))
    sel1 = sel1_ref[...]
    sel2 = sel2_ref[...]
    pick = lambda sel, a: jnp.dot(sel, a, precision=HIGHEST, preferred_element_type=F32)
    ids, gates = [], []
    qt_all = lax.dot_general(wq_ref[...], hb, nt, preferred_element_type=F32).astype(BF16)
    for h in range(heads):
        top = []
        for c in range(2):
            row0 = (h * 2 + c) * half
            st = _dot(keys_ref[h * 2 + c], qt_all[row0:row0 + half])
            top.append(_topk_rows(st, PEER_TOPK))
        (s1, i1), (s2, i2) = top
        cand = pick(sel1, s1) + pick(sel2, s2) + pad_ref[...]
        cid = pick(sel1, i1) * nkeys + pick(sel2, i2)
        best, pos = _topk_rows(cand, PEER_TOPK)
        riota = lax.broadcasted_iota(jnp.int32, cid.shape, 0).astype(F32)
        eid = jnp.concatenate(
            [jnp.sum(jnp.where(riota == pos[r:r + 1], cid, 0.0), axis=0, keepdims=True) for r in range(PEER_TOPK)], axis=0)
        e = jnp.exp(best - jnp.max(best, axis=0, keepdims=True))
        gates.append(e / jnp.sum(e, axis=0, keepdims=True))
        ids.append(eid)
    eid = jnp.concatenate(ids, axis=0).T.astype(jnp.int32)
    eid = jnp.clip(eid, 0, nkeys * nkeys - 1)
    off_ref[...] = eid * rows_per_expert
    gate_ref[...] = jnp.concatenate(gates, axis=0).T


def _peer_topk(hn, wq_t, keys, rows_per_expert):
    n, d = hn.shape
    heads = keys.shape[0] // 2
    nkeys, half = keys.shape[1], keys.shape[2]
    tm = min(512, n)
    pairs = _staircase(PEER_TOPK)
    rows = -(-len(pairs) // SUBLANES) * SUBLANES
    sel1 = np.zeros((rows, PEER_TOPK), np.float32)
    sel2 = np.zeros((rows, PEER_TOPK), np.float32)
    pad = np.zeros((rows, 1), np.float32)
    for i, (a, b) in enumerate(pairs):
        sel1[i, a] = 1.0
        sel2[i, b] = 1.0
    pad[len(pairs):] = -np.inf
    slots = heads * PEER_TOPK
    full = lambda a: pl.BlockSpec(a.shape, lambda i: (0,) * a.ndim)
    consts = (wq_t, keys, jnp.asarray(sel1), jnp.asarray(sel2), jnp.asarray(pad))
    return pl.pallas_call(
        functools.partial(_peer_topk_kernel, heads=heads, nkeys=nkeys, half=half, rows_per_expert=rows_per_expert),
        grid=(n // tm,),
        in_specs=[pl.BlockSpec((tm, d), lambda i: (i, 0))] + [full(a) for a in consts],
        out_specs=[pl.BlockSpec((tm, slots), lambda i: (i, 0))] * 2,
        out_shape=[_sds((n, slots), jnp.int32), _sds((n, slots))],
        compiler_params=_cparams(("parallel",)),
        name="peer_topk",
    )(hn, *consts)


def _pack_table(tab):
    e, d = tab.shape
    half = d // 2
    bits = lax.bitcast_convert_type(tab.astype(BF16), jnp.uint16).astype(jnp.uint32)
    packed = ((bits[:, :half] << 16) | bits[:, half:]).reshape(e * (half // LANES), LANES)
    return jnp.pad(packed, ((0, SUBLANES), (0, 0)))


def _expert_halves(tab_ref, off):
    bits = tab_ref[pl.ds(off, SUBLANES), :]
    return pltpu.bitcast(bits & jnp.uint32(0xFFFF0000), F32), pltpu.bitcast(bits << 16, F32)


def _peer_u_kernel(off_ref, h_ref, gate_ref, tab_ref, w_ref, q_ref, r_ref, *, tq, slots):
    half_rows = SUBLANES // 2
    sub = lax.broadcasted_iota(jnp.int32, (SUBLANES, LANES), 0)
    low = sub < half_rows
    groups = slots // SUBLANES
    qrows = slots * half_rows

    ones = jnp.ones((SUBLANES, LANES), BF16)
    nt = (((1,), (1,)), ((), ()))

    tg = PEER_U_GROUP
    ngroups = tq // tg

    def lane_sums(g):
        start = g * tg * qrows if isinstance(g, int) else pl.multiple_of(g * (tg * qrows), tg * qrows)
        sums = lax.dot_general(ones, q_ref[pl.ds(start, tg * qrows), :], nt, preferred_element_type=F32)
        for i in range(tg):
            r_ref[pl.ds(g * tg + i, 1), :] = sums[0:1, i * qrows:(i + 1) * qrows]

    q_ref[(tq - tg) * qrows:, :] = jnp.zeros((tg * qrows, LANES), BF16)

    def token_group(g, carry):
        lane_sums(jnp.where(g == 0, ngroups - 1, g - 1))
        for i in range(tg):
            token(g * tg + i)
        return carry

    def token(t):
        h = h_ref[t]
        h_first = pltpu.bitcast(jnp.where(low, h, 0.0).astype(BF16).astype(F32), jnp.uint32)
        h_second = pltpu.bitcast(jnp.where(low, pltpu.roll(h, half_rows, 0), 0.0).astype(BF16).astype(F32), jnp.uint32)
        h_pair = pltpu.bitcast(h_first | (h_second >> 16), BF16)
        q_tok = q_ref.at[pl.ds(pl.multiple_of(t * qrows, qrows), qrows)]
        grows = SUBLANES * half_rows
        for g8 in range(groups):
            base = t * slots + g8 * SUBLANES
            ps = []
            for r in range(SUBLANES):
                bits = tab_ref[pl.ds(off_ref[0, 0, base + r], SUBLANES), :]
                prod = pltpu.bitcast(pltpu.bitcast(bits, BF16) * h_pair, jnp.uint32)
                ps.append(pltpu.bitcast(prod & jnp.uint32(0xFFFF0000), F32) + pltpu.bitcast(prod << 16, F32))
            pairs = [ps[i] + pltpu.roll(ps[i + 1], half_rows, 0) for i in range(0, SUBLANES, 2)]
            q_tok[g8 * grows:(g8 + 1) * grows, :] = jnp.concatenate(pairs, axis=0).astype(BF16)

    lax.fori_loop(0, ngroups, token_group, 0)
    lane_sums(ngroups - 1)

    fold = (lax.broadcasted_iota(jnp.int32, (qrows, slots), 0) // half_rows
            == lax.broadcasted_iota(jnp.int32, (qrows, slots), 1))
    fold = jnp.where(fold, 1.0, 0.0).astype(BF16)
    rs = r_ref[...]
    rh = rs.astype(BF16)
    rl = (rs - rh.astype(F32)).astype(BF16)
    act = _dot(rh, fold) + _dot(rl, fold)
    w_ref[...] = gate_ref[...] * _gelu(act)


def _smem_block(tq, slots):
    return pl.BlockSpec((1, 1, tq * slots), lambda i: (i, 0, 0), memory_space=pltpu.SMEM)


def _peer_u(off_blocks, hn3, gate, tab):
    n, rows, lanes = hn3.shape
    slots = gate.shape[1]
    tq = off_blocks.shape[-1] // slots
    return pl.pallas_call(
        functools.partial(_peer_u_kernel, tq=tq, slots=slots),
        grid=(n // tq,),
        in_specs=[_smem_block(tq, slots),
                  pl.BlockSpec((tq, rows, lanes), lambda i: (i, 0, 0)),
                  pl.BlockSpec((tq, slots), lambda i: (i, 0)),
                  pl.BlockSpec(tab.shape, lambda i: (0, 0), pipeline_mode=pl.Buffered(1))],
        out_specs=pl.BlockSpec((tq, slots), lambda i: (i, 0)),
        out_shape=_sds((n, slots)),
        scratch_shapes=[pltpu.VMEM((tq * slots * (SUBLANES // 2), lanes), BF16),
                        pltpu.VMEM((tq, slots * (SUBLANES // 2)), F32)],
        compiler_params=_cparams(("parallel",), 52),
        name="peer_u",
    )(off_blocks, hn3, gate, tab)


def _peer_v_kernel(off_ref, w_ref, tab_ref, h1_ref, g2_ref, nf_ref, o_ref, ws_ref, dg_ref, ot_ref, *, tq, slots):
    sub = lax.broadcasted_iota(jnp.int32, (SUBLANES, LANES), 0)
    low = sub < SUBLANES // 2

    eye = lax.broadcasted_iota(jnp.int32, (slots, slots), 0) == lax.broadcasted_iota(jnp.int32, (slots, slots), 1)

    def diag(t, carry):
        wrow = jnp.broadcast_to(w_ref[pl.ds(t, 1), :], (slots, slots))
        dg_ref[pl.ds(pl.multiple_of(t * slots, slots), slots), :] = jnp.where(eye, wrow, 0.0).astype(BF16)
        return carry

    lax.fori_loop(0, tq, diag, 0, unroll=4)
    ws_ref[...] = _dot(dg_ref[...], jnp.ones((slots, LANES), BF16))

    def token(t, carry):
        zero = jnp.zeros((SUBLANES, LANES), F32)
        first, second = [zero, zero], [zero, zero]
        for s in range(slots):
            hi, lo = _expert_halves(tab_ref, off_ref[0, 0, t * slots + s])
            w = jnp.broadcast_to(ws_ref[pl.ds(t * slots + s, 1), :], (SUBLANES, LANES))
            first[s % 2] = first[s % 2] + hi * w
            second[s % 2] = second[s % 2] + lo * w
        tile = jnp.where(low, first[0] + first[1], pltpu.roll(second[0] + second[1], SUBLANES // 2, 0))
        ot_ref[pl.ds(pl.multiple_of(t * SUBLANES, SUBLANES), SUBLANES), :] = tile
        return carry

    lax.fori_loop(0, tq, token, 0)
    peer = jnp.concatenate([ot_ref[pl.ds(j, tq, stride=SUBLANES), :] for j in range(SUBLANES)], axis=1)
    h = h1_ref[...] + g2_ref[0] * peer
    ms = jnp.mean(h * h, axis=-1, keepdims=True)
    o_ref[...] = h * lax.rsqrt(ms + NORM_EPS) * nf_ref[...]


def _peer_v(off_blocks, w, tab, h1, g2, nfg, tokens_per_row):
    n, slots = w.shape
    tq = off_blocks.shape[-1] // slots
    lanes = tab.shape[-1]
    d = h1.shape[-1]
    assert slots == lanes, "the weight splat uses one (slots, lanes) identity tile"
    assert tokens_per_row % tq == 0, "a token tile must not straddle batch rows (one gate vector per tile)"
    tiles_per_row = tokens_per_row // tq
    return pl.pallas_call(
        functools.partial(_peer_v_kernel, tq=tq, slots=slots),
        grid=(n // tq,),
        in_specs=[_smem_block(tq, slots),
                  pl.BlockSpec((tq, slots), lambda i: (i, 0)),
                  pl.BlockSpec(tab.shape, lambda i: (0, 0), pipeline_mode=pl.Buffered(1)),
                  pl.BlockSpec((tq, d), lambda i: (i, 0)),
                  pl.BlockSpec((1, 1, d), lambda i: (i // tiles_per_row, 0, 0)),
                  pl.BlockSpec((1, d), lambda i: (0, 0))],
        out_specs=pl.BlockSpec((tq, d), lambda i: (i, 0)),
        out_shape=_sds((n, d)),
        scratch_shapes=[pltpu.VMEM((tq * slots, lanes), F32), pltpu.VMEM((tq * slots, slots), BF16),
                        pltpu.VMEM((tq * SUBLANES, lanes), F32)],
        compiler_params=_cparams(("parallel",), 52),
        name="peer_v",
    )(off_blocks, w, tab, h1, g2, nfg)


def _pad_cols(a, n):
    return jnp.pad(a, [(0, 0)] * (a.ndim - 1) + [(0, n - a.shape[-1])])


def _layer(h, hc, c, c_ctx, p, norm_f_g):
    b, l, d = h.shape
    lc = hc.shape[1]
    s5w = p['s5_d'].shape[-1]
    rww = p['rw_w0'].shape[-1]
    heads, head = p['rw_r_k'].shape
    n_dec, n_aaa, n_gate = p['rw_w_w2'].shape[1], p['rw_w_a2'].shape[1], p['rw_w_g2'].shape[0]
    n_lora = n_dec + n_aaa + n_gate
    lora_pad = -(-n_lora // LANES) * LANES

    rows = -(-(b + 1) // SUBLANES) * SUBLANES
    cc = jnp.zeros((rows, d), F32).at[:b].set(c).at[b].set(c_ctx)
    mods = _ada_mods(cc, p['w_ada'], p['b_ada'])
    sh1, sc1, g1, sh2, sc2, g2 = [m.reshape(rows, 1, d) for m in jnp.split(mods, 6, axis=-1)]
    lat = lambda m: m[:b]
    ctxm = lambda m: m[b:b + 1]

    w_in = p['w_in']
    w_s5 = w_in[:, :s5w].astype(BF16)
    w_main = w_in[:, s5w:s5w + 3 * rww].astype(BF16)
    w_lora = _pad_cols(w_in[:, s5w + 3 * rww:], lora_pad).astype(BF16)
    n1g = p['norm1_g'].reshape(1, d)
    u_l, zm_l, zl_l = _in_proj(h, lat(sh1), lat(sc1), n1g, w_s5, w_main, w_lora)
    u_c, zm_c, zl_c = _in_proj(hc, ctxm(sh1), ctxm(sc1), n1g, w_s5, w_main, w_lora)

    mu = p['rw_mu']
    mu_main = mu[:3 * rww].reshape(1, -1)
    mu_lora = _pad_cols(mu[3 * rww:], lora_pad).reshape(1, -1)
    place = lambda w, r0: jnp.zeros(w.shape[:-2] + (lora_pad, rww), F32).at[..., r0:r0 + w.shape[-2], :].set(w).astype(BF16)
    rw_params = (place(p['rw_w_g2'], n_dec + n_aaa), place(p['rw_w_w2'], 0), place(p['rw_w_a2'], n_dec),
                 p['rw_w0'].reshape(2, 1, rww), p['rw_a0'].reshape(2, 1, rww),
                 p['rw_k_k'].reshape(1, rww), p['rw_k_a'].reshape(1, rww), p['rw_r_k'].reshape(1, rww))
    prep_params = (mu_main, mu_lora) + rw_params
    ctx_streams = _rwkv_prep(zm_c, zl_c, prep_params, head, lc + l)
    *streams, gate_l, bonus_l = _rwkv_prep(zm_l, zl_l, prep_params, head, lc + l, ctx_streams)
    rw_y = _rwkv_chunked(tuple(streams[:3]), tuple(streams[3:]), lc, head)

    tables = _s5_tables(p['s5_a_re'], p['s5_a_im'], p['s5_log_dt'], p['s5_b_re'], p['s5_b_im'],
                        p['s5_c_re'], p['s5_c_im'], S5_CHUNK)
    s5_y = _s5_mix(u_c, u_l, p['s5_d'], tables)

    w_out = p['w_out']
    h1, hn2 = _mix_out(s5_y, rw_y, bonus_l, gate_l, h, lat(g1), lat(sh2), lat(sc2), p['norm2_g'].reshape(1, d),
                       p['s5_w_glu'].astype(BF16), p['s5_b_glu'].reshape(1, s5w),
                       p['rw_ln_w'].reshape(1, rww), p['rw_ln_b'].reshape(1, rww),
                       w_out[:s5w].astype(BF16), w_out[s5w:].astype(BF16), head)

    n = b * l
    keys = p['peer_keys']
    pheads, _, nkeys, half = keys.shape
    assert d == VREG_WORDS, "peer_u / peer_v hold one token's features in a single (8, 128) tile"
    off, gate = _peer_topk(hn2.reshape(n, d), p['peer_w_q'].T.astype(BF16),
                           keys.reshape(pheads * 2, nkeys, half).astype(BF16), d // (2 * LANES))
    slots = gate.shape[1]
    tq = min(PEER_TILE, n)
    blocks = lambda a: a.reshape(n // tq, 1, tq * slots)
    w = _peer_u(blocks(off), hn2.reshape(n, SUBLANES, LANES), gate, _pack_table(p['peer_u']))
    out = _peer_v(blocks(off), w, _pack_table(p['peer_v']), h1.reshape(n, d), lat(g2), norm_f_g.reshape(1, d), l)
    return out.reshape(b, l, d)


def kernel(x, c, ctx, c_ctx, w_ada, b_ada, norm1_g, norm2_g, w_in, s5_a_re, s5_a_im, s5_log_dt, s5_b_re, s5_b_im, s5_c_re, s5_c_im, s5_d, s5_w_glu, s5_b_glu, rw_mu, rw_w0, rw_w_w2, rw_a0, rw_w_a2, rw_w_g2, rw_k_k, rw_k_a, rw_r_k, rw_ln_w, rw_ln_b, w_out, peer_w_q, peer_keys, peer_u, peer_v, norm_f_g):
    assert w_ada.shape[0] == 1, "single-layer block: the context stream is never updated"
    p = dict(w_ada=w_ada[0], b_ada=b_ada[0], norm1_g=norm1_g[0], norm2_g=norm2_g[0], w_in=w_in[0],
             s5_a_re=s5_a_re[0], s5_a_im=s5_a_im[0], s5_log_dt=s5_log_dt[0], s5_b_re=s5_b_re[0], s5_b_im=s5_b_im[0],
             s5_c_re=s5_c_re[0], s5_c_im=s5_c_im[0], s5_d=s5_d[0], s5_w_glu=s5_w_glu[0], s5_b_glu=s5_b_glu[0],
             rw_mu=rw_mu[0], rw_w0=rw_w0[0], rw_w_w2=rw_w_w2[0], rw_a0=rw_a0[0], rw_w_a2=rw_w_a2[0],
             rw_w_g2=rw_w_g2[0], rw_k_k=rw_k_k[0], rw_k_a=rw_k_a[0], rw_r_k=rw_r_k[0], rw_ln_w=rw_ln_w[0],
             rw_ln_b=rw_ln_b[0], w_out=w_out[0], peer_w_q=peer_w_q[0], peer_keys=peer_keys[0],
             peer_u=peer_u[0], peer_v=peer_v[0])
    return _layer(x, ctx, c, c_ctx, p, norm_f_g)
```

```python
import functools
import math

import jax
import jax.numpy as jnp
import numpy as np
from jax import lax
from jax.experimental import pallas as pl
from jax.experimental.pallas import tpu as pltpu

F32 = jnp.float32
BF16 = jnp.bfloat16
HIGHEST = lax.Precision.HIGHEST

SUBLANES = 8
LANES = 128
VREG_WORDS = SUBLANES * LANES

GRID_W = 64
NORM_EPS = 1e-6
GN_EPS = 64e-5
KK_EPS = 1e-12
PEER_TOPK = 16

S5_CHUNK = 32
RWKV_CHUNK = 64
RWKV_ROWS = 4
PEER_TILE = 128
PEER_U_GROUP = 4


def _dot(a, b):
    return jnp.dot(a, b, preferred_element_type=F32)


def _cparams(sem, vmem_mb=48):
    return pltpu.CompilerParams(dimension_semantics=sem, vmem_limit_bytes=vmem_mb * 1024 * 1024)


def _sds(shape, dtype=F32):
    return jax.ShapeDtypeStruct(shape, dtype)


def _mods_kernel(c_ref, w_ref, b_ref, o_ref):
    c = c_ref[...]
    s = c * jax.nn.sigmoid(c)
    o_ref[...] = jnp.dot(s, w_ref[...], precision=HIGHEST, preferred_element_type=F32) + b_ref[...]


def _ada_mods(cc, w_ada, b_ada):
    rows, d = cc.shape
    n = w_ada.shape[1]
    bn = d if n % d == 0 else n
    return pl.pallas_call(
        _mods_kernel,
        grid=(n // bn,),
        in_specs=[pl.BlockSpec((rows, d), lambda j: (0, 0)),
                  pl.BlockSpec((d, bn), lambda j: (0, j)),
                  pl.BlockSpec((1, bn), lambda j: (0, j))],
        out_specs=pl.BlockSpec((rows, bn), lambda j: (0, j)),
        out_shape=_sds((rows, n)),
        compiler_params=_cparams(("arbitrary",)),
        name="ada_mods",
    )(cc, w_ada, b_ada.reshape(1, n))


def _inproj_kernel(x_ref, sh_ref, sc_ref, g_ref, ws_ref, wz_ref, wl_ref, u_ref, z_ref, l_ref):
    x = x_ref[0]
    ms = jnp.mean(x * x, axis=-1, keepdims=True)
    hn = (x * lax.rsqrt(ms + NORM_EPS) * g_ref[...]) * (1.0 + sc_ref[0]) + sh_ref[0]
    hb = hn.astype(BF16)
    u_ref[0] = _dot(hb, ws_ref[...])
    z_ref[0] = _dot(hb, wz_ref[...])
    l_ref[0] = _dot(hb, wl_ref[...])


def _in_proj(x, shift, scale, gain, w_s5, w_main, w_lora):
    b, l, d = x.shape
    tm = min(512, l)
    bm = shift.shape[0]
    mod_map = (lambda i, j: (i, 0, 0)) if bm > 1 else (lambda i, j: (0, 0, 0))
    full = lambda a: pl.BlockSpec(a.shape, lambda i, j: (0,) * a.ndim)
    return pl.pallas_call(
        _inproj_kernel,
        grid=(b, l // tm),
        in_specs=[pl.BlockSpec((1, tm, d), lambda i, j: (i, j, 0)),
                  pl.BlockSpec((1, 1, d), mod_map), pl.BlockSpec((1, 1, d), mod_map),
                  full(gain), full(w_s5), full(w_main), full(w_lora)],
        out_specs=[pl.BlockSpec((1, tm, w.shape[1]), lambda i, j: (i, j, 0)) for w in (w_s5, w_main, w_lora)],
        out_shape=[_sds((b, l, w.shape[1])) for w in (w_s5, w_main, w_lora)],
        compiler_params=_cparams(("parallel", "parallel")),
        name="in_proj",
    )(x, shift, scale, gain, w_s5, w_main, w_lora)


def _seg_matrix(width, seg):
    r = lax.broadcasted_iota(jnp.int32, (width, width), 0) // seg
    c = lax.broadcasted_iota(jnp.int32, (width, width), 1) // seg
    return jnp.where(r == c, 1.0, 0.0).astype(BF16)


def _seg_sum(x, seg_m):
    hi = x.astype(BF16)
    lo = (x - hi.astype(F32)).astype(BF16)
    return _dot(hi, seg_m) + _dot(lo, seg_m)


def _rwkv_streams(zz, zl, p, head, outs, with_gate):
    (wg_ref, ww_ref, wa_ref, w0_ref, a0_ref, kk_ref, ka_ref, rk_ref) = p
    width = kk_ref.shape[-1]
    r = zz[:, :width]
    k = zz[:, width:2 * width]
    v = zz[:, 2 * width:3 * width]
    seg_m = _seg_matrix(width, head)
    kk = k * kk_ref[...]
    kk = kk * lax.rsqrt(_seg_sum(kk * kk, seg_m) + KK_EPS)
    outs["r"][0] = r
    outs["v"][0] = v
    outs["kk"][0] = kk
    tl = jnp.tanh(zl).astype(BF16)
    zb = zl.astype(BF16)
    bonus = None
    for d in range(2):
        outs["dec"][d, 0] = -math.exp(-0.5) * jax.nn.sigmoid(w0_ref[d] + _dot(tl, ww_ref[d]))
        a = jax.nn.sigmoid(a0_ref[d] + _dot(zb, wa_ref[d]))
        kd = k * (1.0 + (a - 1.0) * ka_ref[...])
        outs["kd"][d, 0] = kd
        outs["bb"][d, 0] = kk * a
        if with_gate:
            bd = _seg_sum(r * kd * rk_ref[...], seg_m) * v
            bonus = bd if bonus is None else bonus + bd
    if with_gate:
        outs["bonus"][0] = bonus
        outs["g"][0] = _dot(jax.nn.sigmoid(zl).astype(BF16), wg_ref[...])


def _lerp_shift(z, shifted, mu):
    return z + (shifted - z) * mu


def _grid_shift(zc, zp, zn, has_prev, has_next):
    t, c = zc.shape
    tok = lax.broadcasted_iota(jnp.int32, (t, 1), 0) % GRID_W
    col = lax.broadcasted_iota(jnp.int32, (1, c), 1) % 4
    left = jnp.where(tok == 0, 0.0, pltpu.roll(zc, 1, 0))
    right = jnp.where(tok == GRID_W - 1, 0.0, pltpu.roll(zc, t - 1, 0))
    zp = jnp.where(has_prev, zp, 0.0)
    zn = jnp.where(has_next, zn, 0.0)
    if t > GRID_W:
        up = jnp.concatenate([zp, zc[:t - GRID_W]], axis=0)
        down = jnp.concatenate([zc[GRID_W:], zn], axis=0)
    else:
        up, down = zp, zn
    return jnp.where(col == 0, left, jnp.where(col == 1, right, jnp.where(col == 2, up, down)))


def _seq_shift(zc):
    t, c = zc.shape
    tok = lax.broadcasted_iota(jnp.int32, (t, 1), 0)
    col = lax.broadcasted_iota(jnp.int32, (1, c), 1) % 2
    prev = jnp.where(tok == 0, 0.0, pltpu.roll(zc, 1, 0))
    nxt = jnp.where(tok == t - 1, 0.0, pltpu.roll(zc, t - 1, 0))
    return jnp.where(col == 0, prev, nxt)


_STREAMS = ("r", "v", "kk", "dec", "kd", "bb")


def _rwkv_prep_lat_kernel(zc_ref, zp_ref, zn_ref, lc_ref, lp_ref, ln_ref, mum_ref, mul_ref, *rest, head):
    p, o = rest[:8], rest[8 + len(_STREAMS):]
    outs = dict(zip(_STREAMS + ("g", "bonus"), o))
    j = pl.program_id(1)
    has_prev = j > 0
    has_next = j < pl.num_programs(1) - 1
    zc = zc_ref[0]
    lc = lc_ref[0]
    zz = _lerp_shift(zc, _grid_shift(zc, zp_ref[0], zn_ref[0], has_prev, has_next), mum_ref[...])
    zl = _lerp_shift(lc, _grid_shift(lc, lp_ref[0], ln_ref[0], has_prev, has_next), mul_ref[...])
    _rwkv_streams(zz, zl, p, head, outs, True)


def _rwkv_prep_ctx_kernel(zc_ref, lc_ref, mum_ref, mul_ref, *rest, head):
    p, o = rest[:8], rest[8:]
    outs = dict(zip(_STREAMS, o))
    zc = zc_ref[0]
    lc = lc_ref[0]
    zz = _lerp_shift(zc, _seq_shift(zc), mum_ref[...])
    zl = _lerp_shift(lc, _seq_shift(lc), mul_ref[...])
    _rwkv_streams(zz, zl, p, head, outs, False)


def _rwkv_prep(z_main, z_lora, params, head, l_total, ctx_streams=None):
    b, l, cm = z_main.shape
    cl = z_lora.shape[-1]
    width = cm // 3
    full = lambda a: pl.BlockSpec(a.shape, lambda i, j: (0,) * a.ndim)
    grid_mode = ctx_streams is not None
    if grid_mode:
        l_ctx = l_total - l
        rows_per_tile = next(r for r in (8, 4, 2, 1) if l % (r * GRID_W) == 0 and l_ctx % (r * GRID_W) == 0)
        tt = rows_per_tile * GRID_W
        nt = l // tt
        first = l_ctx // tt
        last_row = l // GRID_W - 1
        cur = lambda c: pl.BlockSpec((1, tt, c), lambda i, j: (i, j, 0))
        prev = lambda c: pl.BlockSpec((1, GRID_W, c), lambda i, j: (i, jnp.maximum(j * rows_per_tile - 1, 0), 0))
        nxt = lambda c: pl.BlockSpec((1, GRID_W, c), lambda i, j: (i, jnp.minimum((j + 1) * rows_per_tile, last_row), 0))
        data = [z_main, z_main, z_main, z_lora, z_lora, z_lora]
        data_specs = [cur(cm), prev(cm), nxt(cm), cur(cl), prev(cl), nxt(cl)]
        kern = functools.partial(_rwkv_prep_lat_kernel, head=head)
        extra = list(ctx_streams)
        extra_specs = [pl.BlockSpec(memory_space=pl.ANY)] * len(extra)
        n_in = len(data) + len(params)
        aliases = {n_in + k: k for k in range(len(extra))}
        lat_only = 2
    else:
        tt, nt, first = l, 1, 0
        data = [z_main, z_lora]
        data_specs = [pl.BlockSpec((1, tt, cm), lambda i, j: (i, j, 0)), pl.BlockSpec((1, tt, cl), lambda i, j: (i, j, 0))]
        kern = functools.partial(_rwkv_prep_ctx_kernel, head=head)
        extra, extra_specs, aliases, lat_only = [], [], {}, 0
    single = pl.BlockSpec((1, tt, width), lambda i, j: (i, first + j, 0))
    double = pl.BlockSpec((2, 1, tt, width), lambda i, j: (0, i, first + j, 0))
    local = pl.BlockSpec((1, tt, width), lambda i, j: (i, j, 0))
    out_specs = [single] * 3 + [double] * 3 + [local] * lat_only
    out_shape = [_sds((b, l_total, width))] * 3 + [_sds((2, b, l_total, width))] * 3 + [_sds((b, l, width))] * lat_only
    return pl.pallas_call(
        kern,
        grid=(b, nt),
        in_specs=data_specs + [full(a) for a in params] + extra_specs,
        out_specs=out_specs,
        out_shape=out_shape,
        input_output_aliases=aliases,
        compiler_params=_cparams(("parallel", "parallel")),
        name="rwkv_prep_lat" if grid_mode else "rwkv_prep_ctx",
    )(*data, *params, *extra)


def _split_bf16(x):
    hi = x.astype(BF16)
    return hi, (x - hi.astype(F32)).astype(BF16)


def _rwkv_chunk_kernel(r_ref, v_ref, kk_ref, w_ref, kd_ref, bb_ref, y_ref, s_ref, *, c, head):
    d = pl.program_id(0)
    nb, _, width = r_ref.shape
    pair = 2 * head
    n_pairs = width // pair

    @pl.when(pl.program_id(2) == 0)
    def _():
        s_ref[...] = jnp.zeros_like(s_ref)

    ri = lax.broadcasted_iota(jnp.int32, (c, c), 0)
    ci = lax.broadcasted_iota(jnp.int32, (c, c), 1)
    before = jnp.where(d == 0, ci - ri, ri - ci) < 0
    incl = before | (ri == ci)
    eye = jnp.where(ri == ci, 1.0, 0.0)
    lmat = jnp.broadcast_to(jnp.where(incl, 1.0, 0.0).astype(BF16), (nb, c, c))

    def bmm(a, b, dims=((2,), (1,))):
        return lax.dot_general(a.astype(BF16), b.astype(BF16), (dims, ((0,), (0,))), preferred_element_type=F32)

    nt = ((2,), (2,))

    ld = w_ref[...]
    ld_hi, ld_lo = _split_bf16(ld)
    cum = bmm(lmat, ld_hi) + bmm(lmat, ld_lo)
    gn = jnp.exp(-cum)
    at = -(jnp.exp(cum - ld) * kk_ref[...])
    rt = jnp.exp(cum) * r_ref[...]
    bt = gn * bb_ref[...]
    kt = gn * kd_ref[...]
    vv = v_ref[...]

    def tiles(x):
        return jnp.stack([x[i, :, p * pair:(p + 1) * pair] for i in range(nb) for p in range(n_pairs)], axis=0)

    def per_head(x, select=None):
        xs = [x, x] if select is None else [jnp.where(select == hh, x, 0.0) for hh in range(2)]
        return jnp.stack(xs, axis=1).reshape((2 * x.shape[0],) + x.shape[1:])

    lane_head = lax.broadcasted_iota(jnp.int32, (1, 1, pair), 2) // head
    at_q, rt_q, bt_q, kt_q, v_q = tiles(at), tiles(rt), tiles(bt), tiles(kt), tiles(vv)
    s0 = s_ref[...]
    as0 = per_head(bmm(at_q, s0))
    rs0 = per_head(bmm(rt_q, s0))
    at_h, rt_h, v_h = per_head(at_q), per_head(rt_q), per_head(v_q)
    bt_h, kt_h = per_head(bt_q, lane_head), per_head(kt_q, lane_head)
    a_ab = jnp.where(before, bmm(at_h, bt_h, nt), 0.0)
    a_ak = jnp.where(before, bmm(at_h, kt_h, nt), 0.0)
    a_rb = jnp.where(incl, bmm(rt_h, bt_h, nt), 0.0)
    a_rk = jnp.where(incl, bmm(rt_h, kt_h, nt), 0.0)
    t = eye + a_ab
    pw = a_ab
    for _ in range(c.bit_length() - 2):
        pw = bmm(pw, pw)
        t = t + bmm(t, pw)
    u = bmm(t, as0 + bmm(a_ak, v_h))
    y = rs0 + bmm(a_rb, u) + bmm(a_rk, v_h)

    def merge(x):
        x = x.reshape((x.shape[0] // 2, 2) + x.shape[1:])
        return jnp.where(lane_head == 0, x[:, 0], x[:, 1])

    u_q, y_q = merge(u), merge(y)
    for i in range(nb):
        for p in range(n_pairs):
            y_ref[i, :, p * pair:(p + 1) * pair] = y_q[i * n_pairs + p]
    tn = ((1,), (1,))
    ones = jnp.ones((nb * n_pairs, c, pair), BF16)
    ds = bmm(bt_q, u_q, tn) + bmm(kt_q, v_q, tn)
    tot = bmm(tiles(ld_hi), ones, tn) + bmm(tiles(ld_lo), ones, tn)
    blk = (lax.broadcasted_iota(jnp.int32, (pair, pair), 0) // head
           == lax.broadcasted_iota(jnp.int32, (pair, pair), 1) // head)
    s_ref[...] = jnp.where(blk, jnp.exp(tot) * (s0 + ds), 0.0)


def _rwkv_chunked(shared, directional, l_ctx, head):
    b, ltot, width = shared[0].shape
    c = RWKV_CHUNK
    nb = RWKV_ROWS if b % RWKV_ROWS == 0 else 1
    nch, ncc = ltot // c, l_ctx // c
    nlc = nch - ncc

    def chunk(d, j):
        back = jnp.where(j < ncc, ncc - 1 - j, nch + ncc - 1 - j)
        return jnp.where(d == 0, j, back)

    def out_chunk(d, j):
        o = chunk(d, j)
        return jnp.where(o >= ncc, o - ncc, jnp.where(d == 0, 0, nlc - 1))

    sh_spec = pl.BlockSpec((nb, c, width), lambda d, i, j: (i, chunk(d, j), 0))
    di_spec = pl.BlockSpec((None, nb, c, width), lambda d, i, j: (d, i, chunk(d, j), 0))
    return pl.pallas_call(
        functools.partial(_rwkv_chunk_kernel, c=c, head=head),
        grid=(2, b // nb, nch),
        in_specs=[sh_spec] * 3 + [di_spec] * 3,
        out_specs=pl.BlockSpec((None, nb, c, width), lambda d, i, j: (d, i, out_chunk(d, j), 0)),
        out_shape=_sds((2, b, (nch - ncc) * c, width)),
        scratch_shapes=[pltpu.VMEM((nb * width // (2 * head), 2 * head, 2 * head), F32)],
        compiler_params=_cparams(("arbitrary", "arbitrary", "arbitrary")),
        name="rwkv_chunk",
    )(*shared, *directional)


def _s5_kernel(x_ref, d_ref, m_ref, pin_ref, pout_ref, ar_ref, ai_ref, y_ref, hall_ref, *, nb, n_chunks, n_ctx_chunks):
    x = x_ref[0]
    xb = x.astype(BF16)
    lat0 = n_ctx_chunks * nb
    xl = xb[lat0:]
    y = d_ref[0] * x[lat0:]
    half = ar_ref.shape[-1] // 2
    fwd = list(range(n_chunks))
    bwd = list(range(n_ctx_chunks - 1, -1, -1)) + list(range(n_chunks - 1, n_ctx_chunks - 1, -1))
    for d, order in enumerate((fwd, bwd)):
        v = _dot(xb, pin_ref[d, 0])
        ar = ar_ref[d, 0]
        ai = ai_ref[d, 0]
        h = jnp.zeros((nb, 2 * half), F32)
        for c in order:
            hall_ref[c * nb:(c + 1) * nb, :] = h
            h = h * ar + pltpu.roll(h, half, 1) * ai + v[c * nb:(c + 1) * nb]
        y = y + _dot(xl, m_ref[d, 0]) + _dot(hall_ref[lat0:, :].astype(BF16), pout_ref[d, 0])
    y_ref[0] = y


def _s5_tables(a_re, a_im, log_dt, b_re, b_im, c_re, c_im, tc):
    a_re, a_im, b_re, b_im = (t.astype(F32) for t in (a_re, a_im, b_re, b_im))
    c_re, c_im = c_re.astype(F32), c_im.astype(F32)
    dt = jnp.exp(log_dt.astype(F32))[..., None]
    mag = jnp.exp(dt * a_re)
    ab_re, ab_im = mag * jnp.cos(dt * a_im), mag * jnp.sin(dt * a_im)
    nr, ni = ab_re - 1.0, ab_im
    den = a_re * a_re + a_im * a_im
    cf_re = (nr * a_re + ni * a_im) / den
    cf_im = (ni * a_re - nr * a_im) / den
    bb_re = cf_re[..., None] * b_re - cf_im[..., None] * b_im
    bb_im = cf_re[..., None] * b_im + cf_im[..., None] * b_re

    k = jnp.arange(tc + 1, dtype=F32)[:, None, None, None]
    pmag = jnp.exp(k * (dt * a_re)[None])
    pang = k * (dt * a_im)[None]
    pw_re, pw_im = pmag * jnp.cos(pang), pmag * jnp.sin(pang)

    ab_k_re = pw_re[..., None] * bb_re[None] - pw_im[..., None] * bb_im[None]
    ab_k_im = pw_re[..., None] * bb_im[None] + pw_im[..., None] * bb_re[None]
    ca_k_re = c_re[None] * pw_re[:, :, :, None, :] - c_im[None] * pw_im[:, :, :, None, :]
    ca_k_im = c_re[None] * pw_im[:, :, :, None, :] + c_im[None] * pw_re[:, :, :, None, :]

    taps = (jnp.einsum('dghp,kdgpj->kdghj', c_re, ab_k_re[:tc], precision=HIGHEST)
            - jnp.einsum('dghp,kdgpj->kdghj', c_im, ab_k_im[:tc], precision=HIGHEST))
    s_idx = jnp.arange(tc)[:, None]
    t_idx = jnp.arange(tc)[None, :]
    hc = b_re.shape[-1]
    g = a_re.shape[1]
    p = a_re.shape[2]

    def toeplitz(tp, lag, valid):
        m = jnp.where(valid[:, :, None, None, None], tp[jnp.clip(lag, 0, tc - 1)], 0.0)
        return m.transpose(2, 0, 4, 1, 3).reshape(g, tc * hc, tc * hc)

    m_f = toeplitz(taps[:, 0], t_idx - s_idx, t_idx >= s_idx)
    m_b = toeplitz(taps[:, 1], s_idx - t_idx, s_idx >= t_idx)
    m = jnp.stack([m_f, m_b]).astype(BF16)

    def pin_of(re, im):
        both = jnp.concatenate([re, im], axis=2)
        return both.transpose(1, 0, 3, 2).reshape(g, tc * hc, 2 * p)

    pin = jnp.stack([pin_of(ab_k_re[:tc, 0][::-1], ab_k_im[:tc, 0][::-1]),
                     pin_of(ab_k_re[:tc, 1], ab_k_im[:tc, 1])]).astype(BF16)

    def pout_of(re, im):
        both = jnp.concatenate([re, -im], axis=3)
        return both.transpose(1, 3, 0, 2).reshape(g, 2 * p, tc * hc)

    pout = jnp.stack([pout_of(ca_k_re[1:tc + 1, 0], ca_k_im[1:tc + 1, 0]),
                      pout_of(ca_k_re[1:tc + 1, 1][::-1], ca_k_im[1:tc + 1, 1][::-1])]).astype(BF16)

    at_re, at_im = pw_re[tc], pw_im[tc]
    ar = jnp.concatenate([at_re, at_re], axis=-1)[:, :, None, :]
    ai = jnp.concatenate([-at_im, at_im], axis=-1)[:, :, None, :]
    return m, pin, pout, ar, ai


def _s5_mix(u_ctx, u_lat, s5_d, tables):
    m, pin, pout, ar, ai = tables
    b, lc, width = u_ctx.shape
    l = u_lat.shape[1]
    g = m.shape[1]
    hc = width // g
    tc = S5_CHUNK
    w = tc * hc
    ncc, nlc = lc // tc, l // tc
    nch = ncc + nlc
    u = jnp.concatenate([u_ctx, u_lat], axis=1)
    xg = u.reshape(b, nch, tc, g, hc).transpose(3, 1, 0, 2, 4).reshape(g, nch * b, w)
    dg = jnp.tile(s5_d.reshape(g, 1, hc), (1, tc, 1)).reshape(g, 1, w)
    pgrp = lambda a: pl.BlockSpec((2, 1) + a.shape[2:], lambda i: (0, i, 0, 0))
    y = pl.pallas_call(
        functools.partial(_s5_kernel, nb=b, n_chunks=nch, n_ctx_chunks=ncc),
        grid=(g,),
        in_specs=[pl.BlockSpec((1, nch * b, w), lambda i: (i, 0, 0)),
                  pl.BlockSpec((1, 1, w), lambda i: (i, 0, 0)),
                  pgrp(m), pgrp(pin), pgrp(pout), pgrp(ar), pgrp(ai)],
        out_specs=pl.BlockSpec((1, nlc * b, w), lambda i: (i, 0, 0)),
        out_shape=_sds((g, nlc * b, w)),
        scratch_shapes=[pltpu.VMEM((nch * b, pin.shape[-1]), F32)],
        compiler_params=_cparams(("parallel",), 56),
        name="s5_mix",
    )(xg, dg, m, pin, pout, ar, ai)
    return y.reshape(g, nlc, b, tc, hc).transpose(2, 1, 3, 0, 4).reshape(b, l, width)


def _gelu(x):
    return 0.5 * x * (1.0 + lax.erf(x * (1.0 / math.sqrt(2.0))))


def _mix_out_kernel(s5_ref, rw_ref, bon_ref, g_ref, x_ref, g1_ref, sh2_ref, sc2_ref, n2_ref,
                    wglu_ref, bglu_ref, lnw_ref, lnb_ref, wos_ref, wor_ref, h1_ref, hn_ref, *, head):
    y1 = _gelu(s5_ref[0])
    s5o = y1 * jax.nn.sigmoid(_dot(y1.astype(BF16), wglu_ref[...]) + bglu_ref[...])
    y = rw_ref[0, 0] + rw_ref[1, 0]
    seg_m = _seg_matrix(y.shape[-1], head)
    inv = 1.0 / head
    mu = _seg_sum(y, seg_m) * inv
    yc = y - mu
    var = _seg_sum(yc * yc, seg_m) * inv
    yn = yc * lax.rsqrt(var + GN_EPS) * lnw_ref[...] + lnb_ref[...]
    rwo = (yn + bon_ref[0]) * g_ref[0]
    mix = _dot(s5o.astype(BF16), wos_ref[...]) + _dot(rwo.astype(BF16), wor_ref[...])
    h1 = x_ref[0] + g1_ref[0] * mix
    h1_ref[0] = h1
    ms = jnp.mean(h1 * h1, axis=-1, keepdims=True)
    hn_ref[0] = (h1 * lax.rsqrt(ms + NORM_EPS) * n2_ref[...]) * (1.0 + sc2_ref[0]) + sh2_ref[0]


def _mix_out(s5y, rwy, bonus, gate, x, g1, sh2, sc2, n2g, w_glu, b_glu, ln_w, ln_b, w_out_s, w_out_r, head):
    b, l, d = x.shape
    width = s5y.shape[-1]
    tm = min(512, l)
    tok = lambda c: pl.BlockSpec((1, tm, c), lambda i, j: (i, j, 0))
    mod = pl.BlockSpec((1, 1, d), lambda i, j: (i, 0, 0))
    full = lambda a: pl.BlockSpec(a.shape, lambda i, j: (0,) * a.ndim)
    consts = (n2g, w_glu, b_glu, ln_w, ln_b, w_out_s, w_out_r)
    return pl.pallas_call(
        functools.partial(_mix_out_kernel, head=head),
        grid=(b, l // tm),
        in_specs=[tok(width), pl.BlockSpec((2, 1, tm, width), lambda i, j: (0, i, j, 0)), tok(width), tok(width)]
        + [tok(d), mod, mod, mod] + [full(a) for a in consts],
        out_specs=[tok(d), tok(d)],
        out_shape=[_sds((b, l, d)), _sds((b, l, d))],
        compiler_params=_cparams(("parallel", "parallel")),
        name="mix_out",
    )(s5y, rwy, bonus, gate, x, g1, sh2, sc2, *consts)


def _staircase(k):
    return [(a, b) for a in range(k) for b in range(k) if (a + 1) * (b + 1) <= k]


def _topk_rows(s, k):
    rows = s.shape[0]
    iota = lax.broadcasted_iota(jnp.int32, s.shape, 0).astype(F32)
    vals, idxs = [], []
    for _ in range(k):
        m = jnp.max(s, axis=0, keepdims=True)
        idx = jnp.min(jnp.where(s == m, iota, float(rows)), axis=0, keepdims=True)
        vals.append(m)
        idxs.append(idx)
        s = jnp.where(iota == idx, -jnp.inf, s)
    return jnp.concatenate(vals, axis=0), jnp.concatenate(idxs, axis=0)


def _peer_topk_kernel(hn_ref, wq_ref, keys_ref, sel1_ref, sel2_ref, pad_ref, off_ref, gate_ref, *, heads, nkeys, half,
                      rows_per_expert):
    hb = hn_ref[...].astype(BF16)
    nt = (((1,), (1,)), ((), ()))
    sel1 = sel1_ref[...]
    sel2 = sel2_ref[...]
    pick = lambda sel, a: jnp.dot(sel, a, precision=HIGHEST, preferred_element_type=F32)
    ids, gates = [], []
    qt_all = lax.dot_general(wq_ref[...], hb, nt, preferred_element_type=F32).astype(BF16)
    for h in range(heads):
        top = []
        for c in range(2):
            row0 = (h * 2 + c) * half
            st = _dot(keys_ref[h * 2 + c], qt_all[row0:row0 + half])
            top.append(_topk_rows(st, PEER_TOPK))
        (s1, i1), (s2, i2) = top
        cand = pick(sel1, s1) + pick(sel2, s2) + pad_ref[...]
        cid = pick(sel1, i1) * nkeys + pick(sel2, i2)
        best, pos = _topk_rows(cand, PEER_TOPK)
        riota = lax.broadcasted_iota(jnp.int32, cid.shape, 0).astype(F32)
        eid = jnp.concatenate(
            [jnp.sum(jnp.where(riota == pos[r:r + 1], cid, 0.0), axis=0, keepdims=True) for r in range(PEER_TOPK)], axis=0)
        e = jnp.exp(best - jnp.max(best, axis=0, keepdims=True))
        gates.append(e / jnp.sum(e, axis=0, keepdims=True))
        ids.append(eid)
    eid = jnp.concatenate(ids, axis=0).T.astype(jnp.int32)
    eid = jnp.clip(eid, 0, nkeys * nkeys - 1)
    off_ref[...] = eid * rows_per_expert
    gate_ref[...] = jnp.concatenate(gates, axis=0).T


def _peer_topk(hn, wq_t, keys, rows_per_expert):
    n, d = hn.shape
    heads = keys.shape[0] // 2
    nkeys, half = keys.shape[1], keys.shape[2]
    tm = min(512, n)
    pairs = _staircase(PEER_TOPK)
    rows = -(-len(pairs) // SUBLANES) * SUBLANES
    sel1 = np.zeros((rows, PEER_TOPK), np.float32)
    sel2 = np.zeros((rows, PEER_TOPK), np.float32)
    pad = np.zeros((rows, 1), np.float32)
    for i, (a, b) in enumerate(pairs):
        sel1[i, a] = 1.0
        sel2[i, b] = 1.0
    pad[len(pairs):] = -np.inf
    slots = heads * PEER_TOPK
    full = lambda a: pl.BlockSpec(a.shape, lambda i: (0,) * a.ndim)
    consts = (wq_t, keys, jnp.asarray(sel1), jnp.asarray(sel2), jnp.asarray(pad))
    return pl.pallas_call(
        functools.partial(_peer_topk_kernel, heads=heads, nkeys=nkeys, half=half, rows_per_expert=rows_per_expert),
        grid=(n // tm,),
        in_specs=[pl.BlockSpec((tm, d), lambda i: (i, 0))] + [full(a) for a in consts],
        out_specs=[pl.BlockSpec((tm, slots), lambda i: (i, 0))] * 2,
        out_shape=[_sds((n, slots), jnp.int32), _sds((n, slots))],
        compiler_params=_cparams(("parallel",)),
        name="peer_topk",
    )(hn, *consts)


def _pack_table(tab):
    e, d = tab.shape
    half = d // 2
    bits = lax.bitcast_convert_type(tab.astype(BF16), jnp.uint16).astype(jnp.uint32)
    packed = ((bits[:, :half] << 16) | bits[:, half:]).reshape(e * (half // LANES), LANES)
    return jnp.pad(packed, ((0, SUBLANES), (0, 0)))


def _expert_halves(tab_ref, off):
    bits = tab_ref[pl.ds(off, SUBLANES), :]
    return pltpu.bitcast(bits & jnp.uint32(0xFFFF0000), F32), pltpu.bitcast(bits << 16, F32)


def _peer_u_kernel(off_ref, h_ref, gate_ref, tab_ref, w_ref, q_ref, r_ref, *, tq, slots):
    half_rows = SUBLANES // 2
    sub = lax.broadcasted_iota(jnp.int32, (SUBLANES, LANES), 0)
    low = sub < half_rows
    groups = slots // SUBLANES
    qrows = slots * half_rows

    ones = jnp.ones((SUBLANES, LANES), BF16)
    nt = (((1,), (1,)), ((), ()))

    tg = PEER_U_GROUP
    ngroups = tq // tg

    grp_rows = tg * qrows

    def slot_start(g):
        return (g % 2) * grp_rows if isinstance(g, int) else pl.multiple_of((g % 2) * grp_rows, grp_rows)

    def lane_sums(g):
        sums = lax.dot_general(ones, q_ref[pl.ds(slot_start(g), grp_rows), :], nt, preferred_element_type=F32)
        for i in range(tg):
            r_ref[pl.ds(g * tg + i, 1), :] = sums[0:1, i * qrows:(i + 1) * qrows]

    assert ngroups % 2 == 0
    q_ref[grp_rows:, :] = jnp.zeros((grp_rows, LANES), BF16)

    def token_group(g, carry):
        lane_sums(jnp.where(g == 0, ngroups - 1, g - 1))
        for i in range(tg):
            token(g * tg + i, slot_start(g) + i * qrows)
        return carry

    def token(t, qstart):
        h = h_ref[t]
        h_first = pltpu.bitcast(jnp.where(low, h, 0.0).astype(BF16).astype(F32), jnp.uint32)
        h_second = pltpu.bitcast(jnp.where(low, pltpu.roll(h, half_rows, 0), 0.0).astype(BF16).astype(F32), jnp.uint32)
        h_pair = pltpu.bitcast(h_first | (h_second >> 16), BF16)
        q_tok = q_ref.at[pl.ds(pl.multiple_of(qstart, qrows), qrows)]
        grows = SUBLANES * half_rows
        for g8 in range(groups):
            base = t * slots + g8 * SUBLANES
            ps = []
            for r in range(SUBLANES):
                bits = tab_ref[pl.ds(off_ref[0, 0, base + r], SUBLANES), :]
                prod = pltpu.bitcast(pltpu.bitcast(bits, BF16) * h_pair, jnp.uint32)
                ps.append(pltpu.bitcast(prod & jnp.uint32(0xFFFF0000), F32) + pltpu.bitcast(prod << 16, F32))
            pairs = [ps[i] + pltpu.roll(ps[i + 1], half_rows, 0) for i in range(0, SUBLANES, 2)]
            q_tok[g8 * grows:(g8 + 1) * grows, :] = jnp.concatenate(pairs, axis=0).astype(BF16)

    lax.fori_loop(0, ngroups, token_group, 0)
    lane_sums(ngroups - 1)

    fold = (lax.broadcasted_iota(jnp.int32, (qrows, slots), 0) // half_rows
            == lax.broadcasted_iota(jnp.int32, (qrows, slots), 1))
    fold = jnp.where(fold, 1.0, 0.0).astype(BF16)
    rs = r_ref[...]
    rh = rs.astype(BF16)
    rl = (rs - rh.astype(F32)).astype(BF16)
    act = _dot(rh, fold) + _dot(rl, fold)
    w_ref[...] = gate_ref[...] * _gelu(act)


def _smem_block(tq, slots):
    return pl.BlockSpec((1, 1, tq * slots), lambda i: (i, 0, 0), memory_space=pltpu.SMEM)


def _peer_u(off_blocks, hn3, gate, tab):
    n, rows, lanes = hn3.shape
    slots = gate.shape[1]
    tq = off_blocks.shape[-1] // slots
    return pl.pallas_call(
        functools.partial(_peer_u_kernel, tq=tq, slots=slots),
        grid=(n // tq,),
        in_specs=[_smem_block(tq, slots),
                  pl.BlockSpec((tq, rows, lanes), lambda i: (i, 0, 0)),
                  pl.BlockSpec((tq, slots), lambda i: (i, 0)),
                  pl.BlockSpec(tab.shape, lambda i: (0, 0), pipeline_mode=pl.Buffered(1))],
        out_specs=pl.BlockSpec((tq, slots), lambda i: (i, 0)),
        out_shape=_sds((n, slots)),
        scratch_shapes=[pltpu.VMEM((2 * PEER_U_GROUP * slots * (SUBLANES // 2), lanes), BF16),
                        pltpu.VMEM((tq, slots * (SUBLANES // 2)), F32)],
        compiler_params=_cparams(("parallel",), 52),
        name="peer_u",
    )(off_blocks, hn3, gate, tab)


def _peer_v_kernel(off_ref, w_ref, tab_ref, h1_ref, g2_ref, nf_ref, o_ref, ws_ref, dg_ref, ot_ref, *, tq, slots):
    sub = lax.broadcasted_iota(jnp.int32, (SUBLANES, LANES), 0)
    low = sub < SUBLANES // 2

    eye = lax.broadcasted_iota(jnp.int32, (slots, slots), 0) == lax.broadcasted_iota(jnp.int32, (slots, slots), 1)

    def diag(t, carry):
        wrow = jnp.broadcast_to(w_ref[pl.ds(t, 1), :], (slots, slots))
        dg_ref[pl.ds(pl.multiple_of(t * slots, slots), slots), :] = jnp.where(eye, wrow, 0.0).astype(BF16)
        return carry

    lax.fori_loop(0, tq, diag, 0, unroll=4)
    ws_ref[...] = _dot(dg_ref[...], jnp.ones((slots, LANES), BF16))

    def token(t, carry):
        zero = jnp.zeros((SUBLANES, LANES), F32)
        first, second = [zero, zero], [zero, zero]
        for s in range(slots):
            hi, lo = _expert_halves(tab_ref, off_ref[0, 0, t * slots + s])
            w = jnp.broadcast_to(ws_ref[pl.ds(t * slots + s, 1), :], (SUBLANES, LANES))
            first[s % 2] = first[s % 2] + hi * w
            second[s % 2] = second[s % 2] + lo * w
        tile = jnp.where(low, first[0] + first[1], pltpu.roll(second[0] + second[1], SUBLANES // 2, 0))
        ot_ref[pl.ds(pl.multiple_of(t * SUBLANES, SUBLANES), SUBLANES), :] = tile
        return carry

    lax.fori_loop(0, tq, token, 0)
    peer = jnp.concatenate([ot_ref[pl.ds(j, tq, stride=SUBLANES), :] for j in range(SUBLANES)], axis=1)
    h = h1_ref[...] + g2_ref[0] * peer
    ms = jnp.mean(h * h, axis=-1, keepdims=True)
    o_ref[...] = h * lax.rsqrt(ms + NORM_EPS) * nf_ref[...]


def _peer_v(off_blocks, w, tab, h1, g2, nfg, tokens_per_row):
    n, slots = w.shape
    tq = off_blocks.shape[-1] // slots
    lanes = tab.shape[-1]
    d = h1.shape[-1]
    assert slots == lanes, "the weight splat uses one (slots, lanes) identity tile"
    assert tokens_per_row % tq == 0, "a token tile must not straddle batch rows (one gate vector per tile)"
    tiles_per_row = tokens_per_row // tq
    return pl.pallas_call(
        functools.partial(_peer_v_kernel, tq=tq, slots=slots),
        grid=(n // tq,),
        in_specs=[_smem_block(tq, slots),
                  pl.BlockSpec((tq, slots), lambda i: (i, 0)),
                  pl.BlockSpec(tab.shape, lambda i: (0, 0), pipeline_mode=pl.Buffered(1)),
                  pl.BlockSpec((tq, d), lambda i: (i, 0)),
                  pl.BlockSpec((1, 1, d), lambda i: (i // tiles_per_row, 0, 0)),
                  pl.BlockSpec((1, d), lambda i: (0, 0))],
        out_specs=pl.BlockSpec((tq, d), lambda i: (i, 0)),
        out_shape=_sds((n, d)),
        scratch_shapes=[pltpu.VMEM((tq * slots, lanes), F32), pltpu.VMEM((tq * slots, slots), BF16),
                        pltpu.VMEM((tq * SUBLANES, lanes), F32)],
        compiler_params=_cparams(("parallel",), 52),
        name="peer_v",
    )(off_blocks, w, tab, h1, g2, nfg)


def _pad_cols(a, n):
    return jnp.pad(a, [(0, 0)] * (a.ndim - 1) + [(0, n - a.shape[-1])])


def _layer(h, hc, c, c_ctx, p, norm_f_g):
    b, l, d = h.shape
    lc = hc.shape[1]
    s5w = p['s5_d'].shape[-1]
    rww = p['rw_w0'].shape[-1]
    heads, head = p['rw_r_k'].shape
    n_dec, n_aaa, n_gate = p['rw_w_w2'].shape[1], p['rw_w_a2'].shape[1], p['rw_w_g2'].shape[0]
    n_lora = n_dec + n_aaa + n_gate
    lora_pad = -(-n_lora // LANES) * LANES

    rows = -(-(b + 1) // SUBLANES) * SUBLANES
    cc = jnp.zeros((rows, d), F32).at[:b].set(c).at[b].set(c_ctx)
    mods = _ada_mods(cc, p['w_ada'], p['b_ada'])
    sh1, sc1, g1, sh2, sc2, g2 = [m.reshape(rows, 1, d) for m in jnp.split(mods, 6, axis=-1)]
    lat = lambda m: m[:b]
    ctxm = lambda m: m[b:b + 1]

    w_in = p['w_in']
    w_s5 = w_in[:, :s5w].astype(BF16)
    w_main = w_in[:, s5w:s5w + 3 * rww].astype(BF16)
    w_lora = _pad_cols(w_in[:, s5w + 3 * rww:], lora_pad).astype(BF16)
    n1g = p['norm1_g'].reshape(1, d)
    u_l, zm_l, zl_l = _in_proj(h, lat(sh1), lat(sc1), n1g, w_s5, w_main, w_lora)
    u_c, zm_c, zl_c = _in_proj(hc, ctxm(sh1), ctxm(sc1), n1g, w_s5, w_main, w_lora)

    mu = p['rw_mu']
    mu_main = mu[:3 * rww].reshape(1, -1)
    mu_lora = _pad_cols(mu[3 * rww:], lora_pad).reshape(1, -1)
    place = lambda w, r0: jnp.zeros(w.shape[:-2] + (lora_pad, rww), F32).at[..., r0:r0 + w.shape[-2], :].set(w).astype(BF16)
    rw_params = (place(p['rw_w_g2'], n_dec + n_aaa), place(p['rw_w_w2'], 0), place(p['rw_w_a2'], n_dec),
                 p['rw_w0'].reshape(2, 1, rww), p['rw_a0'].reshape(2, 1, rww),
                 p['rw_k_k'].reshape(1, rww), p['rw_k_a'].reshape(1, rww), p['rw_r_k'].reshape(1, rww))
    prep_params = (mu_main, mu_lora) + rw_params
    ctx_streams = _rwkv_prep(zm_c, zl_c, prep_params, head, lc + l)
    *streams, gate_l, bonus_l = _rwkv_prep(zm_l, zl_l, prep_params, head, lc + l, ctx_streams)
    rw_y = _rwkv_chunked(tuple(streams[:3]), tuple(streams[3:]), lc, head)

    tables = _s5_tables(p['s5_a_re'], p['s5_a_im'], p['s5_log_dt'], p['s5_b_re'], p['s5_b_im'],
                        p['s5_c_re'], p['s5_c_im'], S5_CHUNK)
    s5_y = _s5_mix(u_c, u_l, p['s5_d'], tables)

    w_out = p['w_out']
    h1, hn2 = _mix_out(s5_y, rw_y, bonus_l, gate_l, h, lat(g1), lat(sh2), lat(sc2), p['norm2_g'].reshape(1, d),
                       p['s5_w_glu'].astype(BF16), p['s5_b_glu'].reshape(1, s5w),
                       p['rw_ln_w'].reshape(1, rww), p['rw_ln_b'].reshape(1, rww),
                       w_out[:s5w].astype(BF16), w_out[s5w:].astype(BF16), head)

    n = b * l
    keys = p['peer_keys']
    pheads, _, nkeys, half = keys.shape
    assert d == VREG_WORDS, "peer_u / peer_v hold one token's features in a single (8, 128) tile"
    off, gate = _peer_topk(hn2.reshape(n, d), p['peer_w_q'].T.astype(BF16),
                           keys.reshape(pheads * 2, nkeys, half).astype(BF16), d // (2 * LANES))
    slots = gate.shape[1]
    tq = min(PEER_TILE, n)
    blocks = lambda a: a.reshape(n // tq, 1, tq * slots)
    w = _peer_u(blocks(off), hn2.reshape(n, SUBLANES, LANES), gate, _pack_table(p['peer_u']))
    out = _peer_v(blocks(off), w, _pack_table(p['peer_v']), h1.reshape(n, d), lat(g2), norm_f_g.reshape(1, d), l)
    return out.reshape(b, l, d)


def kernel(x, c, ctx, c_ctx, w_ada, b_ada, norm1_g, norm2_g, w_in, s5_a_re, s5_a_im, s5_log_dt, s5_b_re, s5_b_im, s5_c_re, s5_c_im, s5_d, s5_w_glu, s5_b_glu, rw_mu, rw_w0, rw_w_w2, rw_a0, rw_w_a2, rw_w_g2, rw_k_k, rw_k_a, rw_r_k, rw_ln_w, rw_ln_b, w_out, peer_w_q, peer_keys, peer_u, peer_v, norm_f_g):
    assert w_ada.shape[0] == 1, "single-layer block: the context stream is never updated"
    p = dict(w_ada=w_ada[0], b_ada=b_ada[0], norm1_g=norm1_g[0], norm2_g=norm2_g[0], w_in=w_in[0],
             s5_a_re=s5_a_re[0], s5_a_im=s5_a_im[0], s5_log_dt=s5_log_dt[0], s5_b_re=s5_b_re[0], s5_b_im=s5_b_im[0],
             s5_c_re=s5_c_re[0], s5_c_im=s5_c_im[0], s5_d=s5_d[0], s5_w_glu=s5_w_glu[0], s5_b_glu=s5_b_glu[0],
             rw_mu=rw_mu[0], rw_w0=rw_w0[0], rw_w_w2=rw_w_w2[0], rw_a0=rw_a0[0], rw_w_a2=rw_w_a2[0],
             rw_w_g2=rw_w_g2[0], rw_k_k=rw_k_k[0], rw_k_a=rw_k_a[0], rw_r_k=rw_r_k[0], rw_ln_w=rw_ln_w[0],
             rw_ln_b=rw_ln_b[0], w_out=w_out[0], peer_w_q=peer_w_q[0], peer_keys=peer_keys[0],
             peer_u=peer_u[0], peer_v=peer_v[0])
    return _layer(x, ctx, c, c_ctx, p, norm_f_g)
```

```python
import functools
import math

import jax
import jax.numpy as jnp
import numpy as np
from jax import lax
from jax.experimental import pallas as pl
from jax.experimental.pallas import tpu as pltpu

F32 = jnp.float32
BF16 = jnp.bfloat16
HIGHEST = lax.Precision.HIGHEST

SUBLANES = 8
LANES = 128
VREG_WORDS = SUBLANES * LANES

GRID_W = 64
NORM_EPS = 1e-6
GN_EPS = 64e-5
KK_EPS = 1e-12
PEER_TOPK = 16

S5_CHUNK = 32
RWKV_CHUNK = 64
RWKV_ROWS = 4
PEER_TILE = 128
PEER_U_GROUP = 4


def _dot(a, b):
    return jnp.dot(a, b, preferred_element_type=F32)


def _cparams(sem, vmem_mb=48):
    return pltpu.CompilerParams(dimension_semantics=sem, vmem_limit_bytes=vmem_mb * 1024 * 1024)


def _sds(shape, dtype=F32):
    return jax.ShapeDtypeStruct(shape, dtype)


def _mods_kernel(c_ref, w_ref, b_ref, o_ref):
    c = c_ref[...]
    s = c * jax.nn.sigmoid(c)
    o_ref[...] = jnp.dot(s, w_ref[...], precision=HIGHEST, preferred_element_type=F32) + b_ref[...]


def _ada_mods(cc, w_ada, b_ada):
    rows, d = cc.shape
    n = w_ada.shape[1]
    bn = d if n % d == 0 else n
    return pl.pallas_call(
        _mods_kernel,
        grid=(n // bn,),
        in_specs=[pl.BlockSpec((rows, d), lambda j: (0, 0)),
                  pl.BlockSpec((d, bn), lambda j: (0, j)),
                  pl.BlockSpec((1, bn), lambda j: (0, j))],
        out_specs=pl.BlockSpec((rows, bn), lambda j: (0, j)),
        out_shape=_sds((rows, n)),
        compiler_params=_cparams(("arbitrary",)),
        name="ada_mods",
    )(cc, w_ada, b_ada.reshape(1, n))


def _inproj_kernel(x_ref, sh_ref, sc_ref, g_ref, ws_ref, wz_ref, wl_ref, u_ref, z_ref, l_ref):
    x = x_ref[0]
    ms = jnp.mean(x * x, axis=-1, keepdims=True)
    hn = (x * lax.rsqrt(ms + NORM_EPS) * g_ref[...]) * (1.0 + sc_ref[0]) + sh_ref[0]
    hb = hn.astype(BF16)
    u_ref[0] = _dot(hb, ws_ref[...])
    z_ref[0] = _dot(hb, wz_ref[...])
    l_ref[0] = _dot(hb, wl_ref[...])


def _in_proj(x, shift, scale, gain, w_s5, w_main, w_lora):
    b, l, d = x.shape
    tm = min(512, l)
    bm = shift.shape[0]
    mod_map = (lambda i, j: (i, 0, 0)) if bm > 1 else (lambda i, j: (0, 0, 0))
    full = lambda a: pl.BlockSpec(a.shape, lambda i, j: (0,) * a.ndim)
    return pl.pallas_call(
        _inproj_kernel,
        grid=(b, l // tm),
        in_specs=[pl.BlockSpec((1, tm, d), lambda i, j: (i, j, 0)),
                  pl.BlockSpec((1, 1, d), mod_map), pl.BlockSpec((1, 1, d), mod_map),
                  full(gain), full(w_s5), full(w_main), full(w_lora)],
        out_specs=[pl.BlockSpec((1, tm, w.shape[1]), lambda i, j: (i, j, 0)) for w in (w_s5, w_main, w_lora)],
        out_shape=[_sds((b, l, w.shape[1])) for w in (w_s5, w_main, w_lora)],
        compiler_params=_cparams(("parallel", "parallel")),
        name="in_proj",
    )(x, shift, scale, gain, w_s5, w_main, w_lora)


def _seg_matrix(width, seg):
    r = lax.broadcasted_iota(jnp.int32, (width, width), 0) // seg
    c = lax.broadcasted_iota(jnp.int32, (width, width), 1) // seg
    return jnp.where(r == c, 1.0, 0.0).astype(BF16)


def _seg_sum(x, seg_m):
    hi = x.astype(BF16)
    lo = (x - hi.astype(F32)).astype(BF16)
    return _dot(hi, seg_m) + _dot(lo, seg_m)


def _rwkv_streams(zz, zl, p, head, outs, with_gate):
    (wg_ref, ww_ref, wa_ref, w0_ref, a0_ref, kk_ref, ka_ref, rk_ref) = p
    width = kk_ref.shape[-1]
    r = zz[:, :width]
    k = zz[:, width:2 * width]
    v = zz[:, 2 * width:3 * width]
    seg_m = _seg_matrix(width, head)
    kk = k * kk_ref[...]
    kk = kk * lax.rsqrt(_seg_sum(kk * kk, seg_m) + KK_EPS)
    outs["r"][0] = r
    outs["v"][0] = v
    outs["kk"][0] = kk
    tl = jnp.tanh(zl).astype(BF16)
    zb = zl.astype(BF16)
    bonus = None
    for d in range(2):
        outs["dec"][d, 0] = -math.exp(-0.5) * jax.nn.sigmoid(w0_ref[d] + _dot(tl, ww_ref[d]))
        a = jax.nn.sigmoid(a0_ref[d] + _dot(zb, wa_ref[d]))
        kd = k * (1.0 + (a - 1.0) * ka_ref[...])
        outs["kd"][d, 0] = kd
        outs["bb"][d, 0] = kk * a
        if with_gate:
            bd = _seg_sum(r * kd * rk_ref[...], seg_m) * v
            bonus = bd if bonus is None else bonus + bd
    if with_gate:
        outs["bonus"][0] = bonus
        outs["g"][0] = _dot(jax.nn.sigmoid(zl).astype(BF16), wg_ref[...])


def _lerp_shift(z, shifted, mu):
    return z + (shifted - z) * mu


def _grid_shift(zc, zp, zn, has_prev, has_next):
    t, c = zc.shape
    tok = lax.broadcasted_iota(jnp.int32, (t, 1), 0) % GRID_W
    col = lax.broadcasted_iota(jnp.int32, (1, c), 1) % 4
    left = jnp.where(tok == 0, 0.0, pltpu.roll(zc, 1, 0))
    right = jnp.where(tok == GRID_W - 1, 0.0, pltpu.roll(zc, t - 1, 0))
    zp = jnp.where(has_prev, zp, 0.0)
    zn = jnp.where(has_next, zn, 0.0)
    if t > GRID_W:
        up = jnp.concatenate([zp, zc[:t - GRID_W]], axis=0)
        down = jnp.concatenate([zc[GRID_W:], zn], axis=0)
    else:
        up, down = zp, zn
    return jnp.where(col == 0, left, jnp.where(col == 1, right, jnp.where(col == 2, up, down)))


def _seq_shift(zc):
    t, c = zc.shape
    tok = lax.broadcasted_iota(jnp.int32, (t, 1), 0)
    col = lax.broadcasted_iota(jnp.int32, (1, c), 1) % 2
    prev = jnp.where(tok == 0, 0.0, pltpu.roll(zc, 1, 0))
    nxt = jnp.where(tok == t - 1, 0.0, pltpu.roll(zc, t - 1, 0))
    return jnp.where(col == 0, prev, nxt)


_STREAMS = ("r", "v", "kk", "dec", "kd", "bb")


def _rwkv_prep_lat_kernel(zc_ref, zp_ref, zn_ref, lc_ref, lp_ref, ln_ref, mum_ref, mul_ref, *rest, head):
    p, o = rest[:8], rest[8 + len(_STREAMS):]
    outs = dict(zip(_STREAMS + ("g", "bonus"), o))
    j = pl.program_id(1)
    has_prev = j > 0
    has_next = j < pl.num_programs(1) - 1
    zc = zc_ref[0]
    lc = lc_ref[0]
    zz = _lerp_shift(zc, _grid_shift(zc, zp_ref[0], zn_ref[0], has_prev, has_next), mum_ref[...])
    zl = _lerp_shift(lc, _grid_shift(lc, lp_ref[0], ln_ref[0], has_prev, has_next), mul_ref[...])
    _rwkv_streams(zz, zl, p, head, outs, True)


def _rwkv_prep_ctx_kernel(zc_ref, lc_ref, mum_ref, mul_ref, *rest, head):
    p, o = rest[:8], rest[8:]
    outs = dict(zip(_STREAMS, o))
    zc = zc_ref[0]
    lc = lc_ref[0]
    zz = _lerp_shift(zc, _seq_shift(zc), mum_ref[...])
    zl = _lerp_shift(lc, _seq_shift(lc), mul_ref[...])
    _rwkv_streams(zz, zl, p, head, outs, False)


def _rwkv_prep(z_main, z_lora, params, head, l_total, ctx_streams=None):
    b, l, cm = z_main.shape
    cl = z_lora.shape[-1]
    width = cm // 3
    full = lambda a: pl.BlockSpec(a.shape, lambda i, j: (0,) * a.ndim)
    grid_mode = ctx_streams is not None
    if grid_mode:
        l_ctx = l_total - l
        rows_per_tile = next(r for r in (8, 4, 2, 1) if l % (r * GRID_W) == 0 and l_ctx % (r * GRID_W) == 0)
        tt = rows_per_tile * GRID_W
        nt = l // tt
        first = l_ctx // tt
        last_row = l // GRID_W - 1
        cur = lambda c: pl.BlockSpec((1, tt, c), lambda i, j: (i, j, 0))
        prev = lambda c: pl.BlockSpec((1, GRID_W, c), lambda i, j: (i, jnp.maximum(j * rows_per_tile - 1, 0), 0))
        nxt = lambda c: pl.BlockSpec((1, GRID_W, c), lambda i, j: (i, jnp.minimum((j + 1) * rows_per_tile, last_row), 0))
        data = [z_main, z_main, z_main, z_lora, z_lora, z_lora]
        data_specs = [cur(cm), prev(cm), nxt(cm), cur(cl), prev(cl), nxt(cl)]
        kern = functools.partial(_rwkv_prep_lat_kernel, head=head)
        extra = list(ctx_streams)
        extra_specs = [pl.BlockSpec(memory_space=pl.ANY)] * len(extra)
        n_in = len(data) + len(params)
        aliases = {n_in + k: k for k in range(len(extra))}
        lat_only = 2
    else:
        tt, nt, first = l, 1, 0
        data = [z_main, z_lora]
        data_specs = [pl.BlockSpec((1, tt, cm), lambda i, j: (i, j, 0)), pl.BlockSpec((1, tt, cl), lambda i, j: (i, j, 0))]
        kern = functools.partial(_rwkv_prep_ctx_kernel, head=head)
        extra, extra_specs, aliases, lat_only = [], [], {}, 0
    single = pl.BlockSpec((1, tt, width), lambda i, j: (i, first + j, 0))
    double = pl.BlockSpec((2, 1, tt, width), lambda i, j: (0, i, first + j, 0))
    local = pl.BlockSpec((1, tt, width), lambda i, j: (i, j, 0))
    out_specs = [single] * 3 + [double] * 3 + [local] * lat_only
    out_shape = [_sds((b, l_total, width))] * 3 + [_sds((2, b, l_total, width))] * 3 + [_sds((b, l, width))] * lat_only
    return pl.pallas_call(
        kern,
        grid=(b, nt),
        in_specs=data_specs + [full(a) for a in params] + extra_specs,
        out_specs=out_specs,
        out_shape=out_shape,
        input_output_aliases=aliases,
        compiler_params=_cparams(("parallel", "parallel")),
        name="rwkv_prep_lat" if grid_mode else "rwkv_prep_ctx",
    )(*data, *params, *extra)


def _split_bf16(x):
    hi = x.astype(BF16)
    return hi, (x - hi.astype(F32)).astype(BF16)


def _rwkv_chunk_kernel(r_ref, v_ref, kk_ref, w_ref, kd_ref, bb_ref, y_ref, s_ref, *, c, head):
    d = pl.program_id(0)
    nb, _, width = r_ref.shape
    pair = 2 * head
    n_pairs = width // pair

    @pl.when(pl.program_id(2) == 0)
    def _():
        s_ref[...] = jnp.zeros_like(s_ref)

    ri = lax.broadcasted_iota(jnp.int32, (c, c), 0)
    ci = lax.broadcasted_iota(jnp.int32, (c, c), 1)
    before = jnp.where(d == 0, ci - ri, ri - ci) < 0
    incl = before | (ri == ci)
    eye = jnp.where(ri == ci, 1.0, 0.0)
    lmat = jnp.broadcast_to(jnp.where(incl, 1.0, 0.0).astype(BF16), (nb, c, c))

    def bmm(a, b, dims=((2,), (1,))):
        return lax.dot_general(a.astype(BF16), b.astype(BF16), (dims, ((0,), (0,))), preferred_element_type=F32)

    nt = ((2,), (2,))

    ld = w_ref[...]
    ld_hi, ld_lo = _split_bf16(ld)
    cum = bmm(lmat, ld_hi) + bmm(lmat, ld_lo)
    gn = jnp.exp(-cum)
    at = -(jnp.exp(cum - ld) * kk_ref[...])
    rt = jnp.exp(cum) * r_ref[...]
    bt = gn * bb_ref[...]
    kt = gn * kd_ref[...]
    vv = v_ref[...]

    def tiles(x):
        return jnp.stack([x[i, :, p * pair:(p + 1) * pair] for i in range(nb) for p in range(n_pairs)], axis=0)

    def per_head(x, select=None):
        xs = [x, x] if select is None else [jnp.where(select == hh, x, 0.0) for hh in range(2)]
        return jnp.stack(xs, axis=1).reshape((2 * x.shape[0],) + x.shape[1:])

    lane_head = lax.broadcasted_iota(jnp.int32, (1, 1, pair), 2) // head
    at_q, rt_q, bt_q, kt_q, v_q = tiles(at), tiles(rt), tiles(bt), tiles(kt), tiles(vv)
    s0 = s_ref[...]
    as0 = per_head(bmm(at_q, s0))
    rs0 = per_head(bmm(rt_q, s0))
    at_h, rt_h, v_h = per_head(at_q), per_head(rt_q), per_head(v_q)
    bt_h, kt_h = per_head(bt_q, lane_head), per_head(kt_q, lane_head)
    a_ab = jnp.where(before, bmm(at_h, bt_h, nt), 0.0)
    a_ak = jnp.where(before, bmm(at_h, kt_h, nt), 0.0)
    a_rb = jnp.where(incl, bmm(rt_h, bt_h, nt), 0.0)
    a_rk = jnp.where(incl, bmm(rt_h, kt_h, nt), 0.0)
    t = eye + a_ab
    pw = a_ab
    for _ in range(c.bit_length() - 2):
        pw = bmm(pw, pw)
        t = t + bmm(t, pw)
    u = bmm(t, as0 + bmm(a_ak, v_h))
    y = rs0 + bmm(a_rb, u) + bmm(a_rk, v_h)

    def merge(x):
        x = x.reshape((x.shape[0] // 2, 2) + x.shape[1:])
        return jnp.where(lane_head == 0, x[:, 0], x[:, 1])

    u_q, y_q = merge(u), merge(y)
    for i in range(nb):
        for p in range(n_pairs):
            y_ref[i, :, p * pair:(p + 1) * pair] = y_q[i * n_pairs + p]
    tn = ((1,), (1,))
    ones = jnp.ones((nb * n_pairs, c, pair), BF16)
    ds = bmm(bt_q, u_q, tn) + bmm(kt_q, v_q, tn)
    tot = bmm(tiles(ld_hi), ones, tn) + bmm(tiles(ld_lo), ones, tn)
    blk = (lax.broadcasted_iota(jnp.int32, (pair, pair), 0) // head
           == lax.broadcasted_iota(jnp.int32, (pair, pair), 1) // head)
    s_ref[...] = jnp.where(blk, jnp.exp(tot) * (s0 + ds), 0.0)


def _rwkv_chunked(shared, directional, l_ctx, head):
    b, ltot, width = shared[0].shape
    c = RWKV_CHUNK
    nb = RWKV_ROWS if b % RWKV_ROWS == 0 else 1
    nch, ncc = ltot // c, l_ctx // c
    nlc = nch - ncc

    def chunk(d, j):
        back = jnp.where(j < ncc, ncc - 1 - j, nch + ncc - 1 - j)
        return jnp.where(d == 0, j, back)

    def out_chunk(d, j):
        o = chunk(d, j)
        return jnp.where(o >= ncc, o - ncc, jnp.where(d == 0, 0, nlc - 1))

    sh_spec = pl.BlockSpec((nb, c, width), lambda d, i, j: (i, chunk(d, j), 0))
    di_spec = pl.BlockSpec((None, nb, c, width), lambda d, i, j: (d, i, chunk(d, j), 0))
    return pl.pallas_call(
        functools.partial(_rwkv_chunk_kernel, c=c, head=head),
        grid=(2, b // nb, nch),
        in_specs=[sh_spec] * 3 + [di_spec] * 3,
        out_specs=pl.BlockSpec((None, nb, c, width), lambda d, i, j: (d, i, out_chunk(d, j), 0)),
        out_shape=_sds((2, b, (nch - ncc) * c, width)),
        scratch_shapes=[pltpu.VMEM((nb * width // (2 * head), 2 * head, 2 * head), F32)],
        compiler_params=_cparams(("arbitrary", "arbitrary", "arbitrary")),
        name="rwkv_chunk",
    )(*shared, *directional)


def _s5_kernel(x_ref, d_ref, m_ref, pin_ref, pout_ref, ar_ref, ai_ref, y_ref, hall_ref, *, nb, n_chunks, n_ctx_chunks):
    x = x_ref[0]
    xb = x.astype(BF16)
    lat0 = n_ctx_chunks * nb
    xl = xb[lat0:]
    y = d_ref[0] * x[lat0:]
    ns = ar_ref.shape[-1]
    half = ns // 2
    fwd = list(range(n_chunks))
    bwd = list(range(n_ctx_chunks - 1, -1, -1)) + list(range(n_chunks - 1, n_ctx_chunks - 1, -1))
    v = _dot(xb, jnp.concatenate([pin_ref[0, 0], pin_ref[1, 0]], axis=1))
    for d, order in enumerate((fwd, bwd)):
        ar = ar_ref[d, 0]
        ai = ai_ref[d, 0]
        h = jnp.zeros((nb, ns), F32)
        for c in order:
            hall_ref[c * nb:(c + 1) * nb, d * ns:(d + 1) * ns] = h
            h = h * ar + pltpu.roll(h, half, 1) * ai + v[c * nb:(c + 1) * nb, d * ns:(d + 1) * ns]
    pout = jnp.concatenate([pout_ref[0, 0], pout_ref[1, 0]], axis=0)
    y_ref[0] = y + _dot(xl, m_ref[0]) + _dot(hall_ref[lat0:, :].astype(BF16), pout)


def _s5_tables(a_re, a_im, log_dt, b_re, b_im, c_re, c_im, tc):
    a_re, a_im, b_re, b_im = (t.astype(F32) for t in (a_re, a_im, b_re, b_im))
    c_re, c_im = c_re.astype(F32), c_im.astype(F32)
    dt = jnp.exp(log_dt.astype(F32))[..., None]
    mag = jnp.exp(dt * a_re)
    ab_re, ab_im = mag * jnp.cos(dt * a_im), mag * jnp.sin(dt * a_im)
    nr, ni = ab_re - 1.0, ab_im
    den = a_re * a_re + a_im * a_im
    cf_re = (nr * a_re + ni * a_im) / den
    cf_im = (ni * a_re - nr * a_im) / den
    bb_re = cf_re[..., None] * b_re - cf_im[..., None] * b_im
    bb_im = cf_re[..., None] * b_im + cf_im[..., None] * b_re

    k = jnp.arange(tc + 1, dtype=F32)[:, None, None, None]
    pmag = jnp.exp(k * (dt * a_re)[None])
    pang = k * (dt * a_im)[None]
    pw_re, pw_im = pmag * jnp.cos(pang), pmag * jnp.sin(pang)

    ab_k_re = pw_re[..., None] * bb_re[None] - pw_im[..., None] * bb_im[None]
    ab_k_im = pw_re[..., None] * bb_im[None] + pw_im[..., None] * bb_re[None]
    ca_k_re = c_re[None] * pw_re[:, :, :, None, :] - c_im[None] * pw_im[:, :, :, None, :]
    ca_k_im = c_re[None] * pw_im[:, :, :, None, :] + c_im[None] * pw_re[:, :, :, None, :]

    taps = (jnp.einsum('dghp,kdgpj->kdghj', c_re, ab_k_re[:tc], precision=HIGHEST)
            - jnp.einsum('dghp,kdgpj->kdghj', c_im, ab_k_im[:tc], precision=HIGHEST))
    s_idx = jnp.arange(tc)[:, None]
    t_idx = jnp.arange(tc)[None, :]
    hc = b_re.shape[-1]
    g = a_re.shape[1]
    p = a_re.shape[2]

    def toeplitz(tp, lag, valid):
        m = jnp.where(valid[:, :, None, None, None], tp[jnp.clip(lag, 0, tc - 1)], 0.0)
        return m.transpose(2, 0, 4, 1, 3).reshape(g, tc * hc, tc * hc)

    m_f = toeplitz(taps[:, 0], t_idx - s_idx, t_idx >= s_idx)
    m_b = toeplitz(taps[:, 1], s_idx - t_idx, s_idx >= t_idx)
    m = (m_f + m_b).astype(BF16)

    def pin_of(re, im):
        both = jnp.concatenate([re, im], axis=2)
        return both.transpose(1, 0, 3, 2).reshape(g, tc * hc, 2 * p)

    pin = jnp.stack([pin_of(ab_k_re[:tc, 0][::-1], ab_k_im[:tc, 0][::-1]),
                     pin_of(ab_k_re[:tc, 1], ab_k_im[:tc, 1])]).astype(BF16)

    def pout_of(re, im):
        both = jnp.concatenate([re, -im], axis=3)
        return both.transpose(1, 3, 0, 2).reshape(g, 2 * p, tc * hc)

    pout = jnp.stack([pout_of(ca_k_re[1:tc + 1, 0], ca_k_im[1:tc + 1, 0]),
                      pout_of(ca_k_re[1:tc + 1, 1][::-1], ca_k_im[1:tc + 1, 1][::-1])]).astype(BF16)

    at_re, at_im = pw_re[tc], pw_im[tc]
    ar = jnp.concatenate([at_re, at_re], axis=-1)[:, :, None, :]
    ai = jnp.concatenate([-at_im, at_im], axis=-1)[:, :, None, :]
    return m, pin, pout, ar, ai


def _s5_mix(u_ctx, u_lat, s5_d, tables):
    m, pin, pout, ar, ai = tables
    b, lc, width = u_ctx.shape
    l = u_lat.shape[1]
    g = m.shape[0]
    hc = width // g
    tc = S5_CHUNK
    w = tc * hc
    ncc, nlc = lc // tc, l // tc
    nch = ncc + nlc
    u = jnp.concatenate([u_ctx, u_lat], axis=1)
    xg = u.reshape(b, nch, tc, g, hc).transpose(3, 1, 0, 2, 4).reshape(g, nch * b, w)
    dg = jnp.tile(s5_d.reshape(g, 1, hc), (1, tc, 1)).reshape(g, 1, w)
    pgrp = lambda a: pl.BlockSpec((2, 1) + a.shape[2:], lambda i: (0, i, 0, 0))
    y = pl.pallas_call(
        functools.partial(_s5_kernel, nb=b, n_chunks=nch, n_ctx_chunks=ncc),
        grid=(g,),
        in_specs=[pl.BlockSpec((1, nch * b, w), lambda i: (i, 0, 0)),
                  pl.BlockSpec((1, 1, w), lambda i: (i, 0, 0)),
                  pl.BlockSpec((1, w, w), lambda i: (i, 0, 0)), pgrp(pin), pgrp(pout), pgrp(ar), pgrp(ai)],
        out_specs=pl.BlockSpec((1, nlc * b, w), lambda i: (i, 0, 0)),
        out_shape=_sds((g, nlc * b, w)),
        scratch_shapes=[pltpu.VMEM((nch * b, 2 * pin.shape[-1]), F32)],
        compiler_params=_cparams(("parallel",), 56),
        name="s5_mix",
    )(xg, dg, m, pin, pout, ar, ai)
    return y.reshape(g, nlc, b, tc, hc).transpose(2, 1, 3, 0, 4).reshape(b, l, width)


def _gelu(x):
    return 0.5 * x * (1.0 + lax.erf(x * (1.0 / math.sqrt(2.0))))


def _mix_out_kernel(s5_ref, rw_ref, bon_ref, g_ref, x_ref, g1_ref, sh2_ref, sc2_ref, n2_ref,
                    wglu_ref, bglu_ref, lnw_ref, lnb_ref, wos_ref, wor_ref, h1_ref, hn_ref, *, head):
    y1 = _gelu(s5_ref[0])
    s5o = y1 * jax.nn.sigmoid(_dot(y1.astype(BF16), wglu_ref[...]) + bglu_ref[...])
    y = rw_ref[0, 0] + rw_ref[1, 0]
    seg_m = _seg_matrix(y.shape[-1], head)
    inv = 1.0 / head
    mu = _seg_sum(y, seg_m) * inv
    yc = y - mu
    var = _seg_sum(yc * yc, seg_m) * inv
    yn = yc * lax.rsqrt(var + GN_EPS) * lnw_ref[...] + lnb_ref[...]
    rwo = (yn + bon_ref[0]) * g_ref[0]
    mix = _dot(s5o.astype(BF16), wos_ref[...]) + _dot(rwo.astype(BF16), wor_ref[...])
    h1 = x_ref[0] + g1_ref[0] * mix
    h1_ref[0] = h1
    ms = jnp.mean(h1 * h1, axis=-1, keepdims=True)
    hn_ref[0] = (h1 * lax.rsqrt(ms + NORM_EPS) * n2_ref[...]) * (1.0 + sc2_ref[0]) + sh2_ref[0]


def _mix_out(s5y, rwy, bonus, gate, x, g1, sh2, sc2, n2g, w_glu, b_glu, ln_w, ln_b, w_out_s, w_out_r, head):
    b, l, d = x.shape
    width = s5y.shape[-1]
    tm = min(512, l)
    tok = lambda c: pl.BlockSpec((1, tm, c), lambda i, j: (i, j, 0))
    mod = pl.BlockSpec((1, 1, d), lambda i, j: (i, 0, 0))
    full = lambda a: pl.BlockSpec(a.shape, lambda i, j: (0,) * a.ndim)
    consts = (n2g, w_glu, b_glu, ln_w, ln_b, w_out_s, w_out_r)
    return pl.pallas_call(
        functools.partial(_mix_out_kernel, head=head),
        grid=(b, l // tm),
        in_specs=[tok(width), pl.BlockSpec((2, 1, tm, width), lambda i, j: (0, i, j, 0)), tok(width), tok(width)]
        + [tok(d), mod, mod, mod] + [full(a) for a in consts],
        out_specs=[tok(d), tok(d)],
        out_shape=[_sds((b, l, d)), _sds((b, l, d))],
        compiler_params=_cparams(("parallel", "parallel")),
        name="mix_out",
    )(s5y, rwy, bonus, gate, x, g1, sh2, sc2, *consts)


def _staircase(k):
    return [(a, b) for a in range(k) for b in range(k) if (a + 1) * (b + 1) <= k]


def _topk_rows(s, k):
    rows = s.shape[0]
    iota = lax.broadcasted_iota(jnp.int32, s.shape, 0).astype(F32)
    vals, idxs = [], []
    for _ in range(k):
        m = jnp.max(s, axis=0, keepdims=True)
        idx = jnp.min(jnp.where(s == m, iota, float(rows)), axis=0, keepdims=True)
        vals.append(m)
        idxs.append(idx)
        s = jnp.where(iota == idx, -jnp.inf, s)
    return jnp.concatenate(vals, axis=0), jnp.concatenate(idxs, axis=0)


def _peer_topk_kernel(hn_ref, wq_ref, keys_ref, sel1_ref, sel2_ref, pad_ref, off_ref, gate_ref, *, heads, nkeys, half,
                      rows_per_expert):
    hb = hn_ref[...].astype(BF16)
    nt = (((1,), (1,)), ((), ()))
    sel1 = sel1_ref[...]
    sel2 = sel2_ref[...]
    pick = lambda sel, a: jnp.dot(sel, a, precision=HIGHEST, preferred_element_type=F32)
    ids, gates = [], []
    qt_all = lax.dot_general(wq_ref[...], hb, nt, preferred_element_type=F32).astype(BF16)
    for h in range(heads):
        top = []
        for c in range(2):
            row0 = (h * 2 + c) * half
            st = _dot(keys_ref[h * 2 + c], qt_all[row0:row0 + half])
            top.append(_topk_rows(st, PEER_TOPK))
        (s1, i1), (s2, i2) = top
        cand = pick(sel1, s1) + pick(sel2, s2) + pad_ref[...]
        cid = pick(sel1, i1) * nkeys + pick(sel2, i2)
        best, pos = _topk_rows(cand, PEER_TOPK)
        riota = lax.broadcasted_iota(jnp.int32, cid.shape, 0).astype(F32)
        eid = jnp.concatenate(
            [jnp.sum(jnp.where(riota == pos[r:r + 1], cid, 0.0), axis=0, keepdims=True) for r in range(PEER_TOPK)], axis=0)
        e = jnp.exp(best - jnp.max(best, axis=0, keepdims=True))
        gates.append(e / jnp.sum(e, axis=0, keepdims=True))
        ids.append(eid)
    eid = jnp.concatenate(ids, axis=0).T.astype(jnp.int32)
    eid = jnp.clip(eid, 0, nkeys * nkeys - 1)
    off_ref[...] = eid * rows_per_expert
    gate_ref[...] = jnp.concatenate(gates, axis=0).T


def _peer_topk(hn, wq_t, keys, rows_per_expert):
    n, d = hn.shape
    heads = keys.shape[0] // 2
    nkeys, half = keys.shape[1], keys.shape[2]
    tm = min(512, n)
    pairs = _staircase(PEER_TOPK)
    rows = -(-len(pairs) // SUBLANES) * SUBLANES
    sel1 = np.zeros((rows, PEER_TOPK), np.float32)
    sel2 = np.zeros((rows, PEER_TOPK), np.float32)
    pad = np.zeros((rows, 1), np.float32)
    for i, (a, b) in enumerate(pairs):
        sel1[i, a] = 1.0
        sel2[i, b] = 1.0
    pad[len(pairs):] = -np.inf
    slots = heads * PEER_TOPK
    full = lambda a: pl.BlockSpec(a.shape, lambda i: (0,) * a.ndim)
    consts = (wq_t, keys, jnp.asarray(sel1), jnp.asarray(sel2), jnp.asarray(pad))
    return pl.pallas_call(
        functools.partial(_peer_topk_kernel, heads=heads, nkeys=nkeys, half=half, rows_per_expert=rows_per_expert),
        grid=(n // tm,),
        in_specs=[pl.BlockSpec((tm, d), lambda i: (i, 0))] + [full(a) for a in consts],
        out_specs=[pl.BlockSpec((tm, slots), lambda i: (i, 0))] * 2,
        out_shape=[_sds((n, slots), jnp.int32), _sds((n, slots))],
        compiler_params=_cparams(("parallel",)),
        name="peer_topk",
    )(hn, *consts)


def _pack_table(tab):
    e, d = tab.shape
    half = d // 2
    bits = lax.bitcast_convert_type(tab.astype(BF16), jnp.uint16).astype(jnp.uint32)
    packed = ((bits[:, :half] << 16) | bits[:, half:]).reshape(e * (half // LANES), LANES)
    return jnp.pad(packed, ((0, SUBLANES), (0, 0)))


def _expert_halves(tab_ref, off):
    bits = tab_ref[pl.ds(off, SUBLANES), :]
    return pltpu.bitcast(bits & jnp.uint32(0xFFFF0000), F32), pltpu.bitcast(bits << 16, F32)


def _peer_u_kernel(off_ref, h_ref, gate_ref, tab_ref, w_ref, q_ref, r_ref, *, tq, slots):
    half_rows = SUBLANES // 2
    sub = lax.broadcasted_iota(jnp.int32, (SUBLANES, LANES), 0)
    low = sub < half_rows
    groups = slots // SUBLANES
    qrows = slots * half_rows

    ones = jnp.ones((SUBLANES, LANES), BF16)
    nt = (((1,), (1,)), ((), ()))

    tg = PEER_U_GROUP
    ngroups = tq // tg

    grp_rows = tg * qrows

    def slot_start(g):
        return (g % 2) * grp_rows if isinstance(g, int) else pl.multiple_of((g % 2) * grp_rows, grp_rows)

    def lane_sums(g):
        sums = lax.dot_general(ones, q_ref[pl.ds(slot_start(g), grp_rows), :], nt, preferred_element_type=F32)
        for i in range(tg):
            r_ref[pl.ds(g * tg + i, 1), :] = sums[0:1, i * qrows:(i + 1) * qrows]

    assert ngroups % 2 == 0
    q_ref[grp_rows:, :] = jnp.zeros((grp_rows, LANES), BF16)

    def token_group(g, carry):
        lane_sums(jnp.where(g == 0, ngroups - 1, g - 1))
        for i in range(tg):
            token(g * tg + i, slot_start(g) + i * qrows)
        return carry

    def token(t, qstart):
        h = h_ref[t]
        h_first = pltpu.bitcast(jnp.where(low, h, 0.0).astype(BF16).astype(F32), jnp.uint32)
        h_second = pltpu.bitcast(jnp.where(low, pltpu.roll(h, half_rows, 0), 0.0).astype(BF16).astype(F32), jnp.uint32)
        h_pair = pltpu.bitcast(h_first | (h_second >> 16), BF16)
        q_tok = q_ref.at[pl.ds(pl.multiple_of(qstart, qrows), qrows)]
        grows = SUBLANES * half_rows
        for g8 in range(groups):
            base = t * slots + g8 * SUBLANES
            ps = []
            for r in range(SUBLANES):
                bits = tab_ref[pl.ds(off_ref[0, 0, base + r], SUBLANES), :]
                prod = pltpu.bitcast(pltpu.bitcast(bits, BF16) * h_pair, jnp.uint32)
                ps.append(pltpu.bitcast(prod & jnp.uint32(0xFFFF0000), F32) + pltpu.bitcast(prod << 16, F32))
            pairs = [ps[i] + pltpu.roll(ps[i + 1], half_rows, 0) for i in range(0, SUBLANES, 2)]
            q_tok[g8 * grows:(g8 + 1) * grows, :] = jnp.concatenate(pairs, axis=0).astype(BF16)

    lax.fori_loop(0, ngroups, token_group, 0)
    lane_sums(ngroups - 1)

    fold = (lax.broadcasted_iota(jnp.int32, (qrows, slots), 0) // half_rows
            == lax.broadcasted_iota(jnp.int32, (qrows, slots), 1))
    fold = jnp.where(fold, 1.0, 0.0).astype(BF16)
    rs = r_ref[...]
    rh = rs.astype(BF16)
    rl = (rs - rh.astype(F32)).astype(BF16)
    act = _dot(rh, fold) + _dot(rl, fold)
    w_ref[...] = gate_ref[...] * _gelu(act)


def _smem_block(tq, slots):
    return pl.BlockSpec((1, 1, tq * slots), lambda i: (i, 0, 0), memory_space=pltpu.SMEM)


def _peer_u(off_blocks, hn3, gate, tab):
    n, rows, lanes = hn3.shape
    slots = gate.shape[1]
    tq = off_blocks.shape[-1] // slots
    return pl.pallas_call(
        functools.partial(_peer_u_kernel, tq=tq, slots=slots),
        grid=(n // tq,),
        in_specs=[_smem_block(tq, slots),
                  pl.BlockSpec((tq, rows, lanes), lambda i: (i, 0, 0)),
                  pl.BlockSpec((tq, slots), lambda i: (i, 0)),
                  pl.BlockSpec(tab.shape, lambda i: (0, 0), pipeline_mode=pl.Buffered(1))],
        out_specs=pl.BlockSpec((tq, slots), lambda i: (i, 0)),
        out_shape=_sds((n, slots)),
        scratch_shapes=[pltpu.VMEM((2 * PEER_U_GROUP * slots * (SUBLANES // 2), lanes), BF16),
                        pltpu.VMEM((tq, slots * (SUBLANES // 2)), F32)],
        compiler_params=_cparams(("parallel",), 52),
        name="peer_u",
    )(off_blocks, hn3, gate, tab)


def _peer_v_kernel(off_ref, w_ref, tab_ref, h1_ref, g2_ref, nf_ref, o_ref, ws_ref, dg_ref, ot_ref, *, tq, slots):
    sub = lax.broadcasted_iota(jnp.int32, (SUBLANES, LANES), 0)
    low = sub < SUBLANES // 2

    eye = lax.broadcasted_iota(jnp.int32, (slots, slots), 0) == lax.broadcasted_iota(jnp.int32, (slots, slots), 1)

    def diag(t, carry):
        wrow = jnp.broadcast_to(w_ref[pl.ds(t, 1), :], (slots, slots))
        dg_ref[pl.ds(pl.multiple_of(t * slots, slots), slots), :] = jnp.where(eye, wrow, 0.0).astype(BF16)
        return carry

    lax.fori_loop(0, tq, diag, 0, unroll=4)
    ws_ref[...] = _dot(dg_ref[...], jnp.ones((slots, LANES), BF16))

    def token(t, carry):
        zero = jnp.zeros((SUBLANES, LANES), F32)
        first, second = [zero, zero], [zero, zero]
        for s in range(slots):
            hi, lo = _expert_halves(tab_ref, off_ref[0, 0, t * slots + s])
            w = jnp.broadcast_to(ws_ref[pl.ds(t * slots + s, 1), :], (SUBLANES, LANES))
            first[s % 2] = first[s % 2] + hi * w
            second[s % 2] = second[s % 2] + lo * w
        tile = jnp.where(low, first[0] + first[1], pltpu.roll(second[0] + second[1], SUBLANES // 2, 0))
        ot_ref[pl.ds(pl.multiple_of(t * SUBLANES, SUBLANES), SUBLANES), :] = tile
        return carry

    lax.fori_loop(0, tq, token, 0)
    peer = jnp.concatenate([ot_ref[pl.ds(j, tq, stride=SUBLANES), :] for j in range(SUBLANES)], axis=1)
    h = h1_ref[...] + g2_ref[0] * peer
    ms = jnp.mean(h * h, axis=-1, keepdims=True)
    o_ref[...] = h * lax.rsqrt(ms + NORM_EPS) * nf_ref[...]


def _peer_v(off_blocks, w, tab, h1, g2, nfg, tokens_per_row):
    n, slots = w.shape
    tq = off_blocks.shape[-1] // slots
    lanes = tab.shape[-1]
    d = h1.shape[-1]
    assert slots == lanes, "the weight splat uses one (slots, lanes) identity tile"
    assert tokens_per_row % tq == 0, "a token tile must not straddle batch rows (one gate vector per tile)"
    tiles_per_row = tokens_per_row // tq
    return pl.pallas_call(
        functools.partial(_peer_v_kernel, tq=tq, slots=slots),
        grid=(n // tq,),
        in_specs=[_smem_block(tq, slots),
                  pl.BlockSpec((tq, slots), lambda i: (i, 0)),
                  pl.BlockSpec(tab.shape, lambda i: (0, 0), pipeline_mode=pl.Buffered(1)),
                  pl.BlockSpec((tq, d), lambda i: (i, 0)),
                  pl.BlockSpec((1, 1, d), lambda i: (i // tiles_per_row, 0, 0)),
                  pl.BlockSpec((1, d), lambda i: (0, 0))],
        out_specs=pl.BlockSpec((tq, d), lambda i: (i, 0)),
        out_shape=_sds((n, d)),
        scratch_shapes=[pltpu.VMEM((tq * slots, lanes), F32), pltpu.VMEM((tq * slots, slots), BF16),
                        pltpu.VMEM((tq * SUBLANES, lanes), F32)],
        compiler_params=_cparams(("parallel",), 52),
        name="peer_v",
    )(off_blocks, w, tab, h1, g2, nfg)


def _pad_cols(a, n):
    return jnp.pad(a, [(0, 0)] * (a.ndim - 1) + [(0, n - a.shape[-1])])


def _layer(h, hc, c, c_ctx, p, norm_f_g):
    b, l, d = h.shape
    lc = hc.shape[1]
    s5w = p['s5_d'].shape[-1]
    rww = p['rw_w0'].shape[-1]
    heads, head = p['rw_r_k'].shape
    n_dec, n_aaa, n_gate = p['rw_w_w2'].shape[1], p['rw_w_a2'].shape[1], p['rw_w_g2'].shape[0]
    n_lora = n_dec + n_aaa + n_gate
    lora_pad = -(-n_lora // LANES) * LANES

    rows = -(-(b + 1) // SUBLANES) * SUBLANES
    cc = jnp.zeros((rows, d), F32).at[:b].set(c).at[b].set(c_ctx)
    mods = _ada_mods(cc, p['w_ada'], p['b_ada'])
    sh1, sc1, g1, sh2, sc2, g2 = [m.reshape(rows, 1, d) for m in jnp.split(mods, 6, axis=-1)]
    lat = lambda m: m[:b]
    ctxm = lambda m: m[b:b + 1]

    w_in = p['w_in']
    w_s5 = w_in[:, :s5w].astype(BF16)
    w_main = w_in[:, s5w:s5w + 3 * rww].astype(BF16)
    w_lora = _pad_cols(w_in[:, s5w + 3 * rww:], lora_pad).astype(BF16)
    n1g = p['norm1_g'].reshape(1, d)
    u_l, zm_l, zl_l = _in_proj(h, lat(sh1), lat(sc1), n1g, w_s5, w_main, w_lora)
    u_c, zm_c, zl_c = _in_proj(hc, ctxm(sh1), ctxm(sc1), n1g, w_s5, w_main, w_lora)

    mu = p['rw_mu']
    mu_main = mu[:3 * rww].reshape(1, -1)
    mu_lora = _pad_cols(mu[3 * rww:], lora_pad).reshape(1, -1)
    place = lambda w, r0: jnp.zeros(w.shape[:-2] + (lora_pad, rww), F32).at[..., r0:r0 + w.shape[-2], :].set(w).astype(BF16)
    rw_params = (place(p['rw_w_g2'], n_dec + n_aaa), place(p['rw_w_w2'], 0), place(p['rw_w_a2'], n_dec),
                 p['rw_w0'].reshape(2, 1, rww), p['rw_a0'].reshape(2, 1, rww),
                 p['rw_k_k'].reshape(1, rww), p['rw_k_a'].reshape(1, rww), p['rw_r_k'].reshape(1, rww))
    prep_params = (mu_main, mu_lora) + rw_params
    ctx_streams = _rwkv_prep(zm_c, zl_c, prep_params, head, lc + l)
    *streams, gate_l, bonus_l = _rwkv_prep(zm_l, zl_l, prep_params, head, lc + l, ctx_streams)
    rw_y = _rwkv_chunked(tuple(streams[:3]), tuple(streams[3:]), lc, head)

    tables = _s5_tables(p['s5_a_re'], p['s5_a_im'], p['s5_log_dt'], p['s5_b_re'], p['s5_b_im'],
                        p['s5_c_re'], p['s5_c_im'], S5_CHUNK)
    s5_y = _s5_mix(u_c, u_l, p['s5_d'], tables)

    w_out = p['w_out']
    h1, hn2 = _mix_out(s5_y, rw_y, bonus_l, gate_l, h, lat(g1), lat(sh2), lat(sc2), p['norm2_g'].reshape(1, d),
                       p['s5_w_glu'].astype(BF16), p['s5_b_glu'].reshape(1, s5w),
                       p['rw_ln_w'].reshape(1, rww), p['rw_ln_b'].reshape(1, rww),
                       w_out[:s5w].astype(BF16), w_out[s5w:].astype(BF16), head)

    n = b * l
    keys = p['peer_keys']
    pheads, _, nkeys, half = keys.shape
    assert d == VREG_WORDS, "peer_u / peer_v hold one token's features in a single (8, 128) tile"
    off, gate = _peer_topk(hn2.reshape(n, d), p['peer_w_q'].T.astype(BF16),
                           keys.reshape(pheads * 2, nkeys, half).astype(BF16), d // (2 * LANES))
    slots = gate.shape[1]
    tq = min(PEER_TILE, n)
    blocks = lambda a: a.reshape(n // tq, 1, tq * slots)
    w = _peer_u(blocks(off), hn2.reshape(n, SUBLANES, LANES), gate, _pack_table(p['peer_u']))
    out = _peer_v(blocks(off), w, _pack_table(p['peer_v']), h1.reshape(n, d), lat(g2), norm_f_g.reshape(1, d), l)
    return out.reshape(b, l, d)


def kernel(x, c, ctx, c_ctx, w_ada, b_ada, norm1_g, norm2_g, w_in, s5_a_re, s5_a_im, s5_log_dt, s5_b_re, s5_b_im, s5_c_re, s5_c_im, s5_d, s5_w_glu, s5_b_glu, rw_mu, rw_w0, rw_w_w2, rw_a0, rw_w_a2, rw_w_g2, rw_k_k, rw_k_a, rw_r_k, rw_ln_w, rw_ln_b, w_out, peer_w_q, peer_keys, peer_u, peer_v, norm_f_g):
    assert w_ada.shape[0] == 1, "single-layer block: the context stream is never updated"
    p = dict(w_ada=w_ada[0], b_ada=b_ada[0], norm1_g=norm1_g[0], norm2_g=norm2_g[0], w_in=w_in[0],
             s5_a_re=s5_a_re[0], s5_a_im=s5_a_im[0], s5_log_dt=s5_log_dt[0], s5_b_re=s5_b_re[0], s5_b_im=s5_b_im[0],
             s5_c_re=s5_c_re[0], s5_c_im=s5_c_im[0], s5_d=s5_d[0], s5_w_glu=s5_w_glu[0], s5_b_glu=s5_b_glu[0],
             rw_mu=rw_mu[0], rw_w0=rw_w0[0], rw_w_w2=rw_w_w2[0], rw_a0=rw_a0[0], rw_w_a2=rw_w_a2[0],
             rw_w_g2=rw_w_g2[0], rw_k_k=rw_k_k[0], rw_k_a=rw_k_a[0], rw_r_k=rw_r_k[0], rw_ln_w=rw_ln_w[0],
             rw_ln_b=rw_ln_b[0], w_out=w_out[0], peer_w_q=peer_w_q[0], peer_keys=peer_keys[0],
             peer_u=peer_u[0], peer_v=peer_v[0])
    return _layer(x, ctx, c, c_ctx, p, norm_f_g)
```
